```python
import jax, jax.numpy as jnp
from jax import lax
import numpy as np

D_MODEL = 2048
BATCH = 4
SEQ = 2048
DEPTH = 4

N_MIXERS = 2
ROPE_THETA = 500000.0
NORM_EPS = 1e-6
BLOCK = 128
POS_OFFSET_MAX = 1024
NEG_INF = -1e30

MLA_HEADS = 16
MLA_Q_RANK = 512
MLA_KV_RANK = 512
MLA_NOPE = 128
MLA_ROPE = 64
MLA_V = 128
MLA_WIDTH = MLA_HEADS * MLA_V
MLA_IN = MLA_Q_RANK + MLA_KV_RANK + MLA_ROPE + MLA_WIDTH

SWA_Q_HEADS = 32
SWA_KV_HEADS = 4
SWA_GROUP = SWA_Q_HEADS // SWA_KV_HEADS
SWA_HEAD_DIM = 64
SWA_WINDOW = 128
SWA_ROPE_DIM = SWA_HEAD_DIM // 4
SWA_WIDTH = SWA_Q_HEADS * SWA_HEAD_DIM
SWA_KV_WIDTH = SWA_KV_HEADS * SWA_HEAD_DIM
SWA_IN = SWA_WIDTH + 2 * SWA_KV_WIDTH + SWA_WIDTH

N_MLA = (DEPTH + 1) // 2
N_SWA = DEPTH // 2

kernel_name = "interleaved_mla_swa_sink_gated_trunk"


def rmsnorm(x, g):
    xf = x.astype(jnp.float32)
    y = xf * lax.rsqrt(jnp.mean(xf * xf, axis=-1, keepdims=True) + NORM_EPS)
    return (y * g.astype(jnp.float32)).astype(x.dtype)


def rope(x, positions, rot_dim):
    half = rot_dim // 2
    inv_freq = ROPE_THETA ** (-jnp.arange(0, rot_dim, 2, dtype=jnp.float32) / rot_dim)
    ang = positions.astype(jnp.float32)[..., None] * inv_freq
    cos = jnp.cos(ang)[:, :, None, :]
    sin = jnp.sin(ang)[:, :, None, :]
    xf = x.astype(jnp.float32)
    x1, x2, rest = xf[..., :half], xf[..., half:rot_dim], xf[..., rot_dim:]
    out = jnp.concatenate([x1 * cos - x2 * sin, x2 * cos + x1 * sin, rest], axis=-1)
    return out.astype(x.dtype)


def mla_causal_attention(q_nope, q_rope, k_nope, k_rope, v):
    B, S, H, _ = q_nope.shape
    nb = S // BLOCK
    scale = (MLA_NOPE + MLA_ROPE) ** -0.5
    key_idx = jnp.arange(S)

    def one_block(i):
        start = i * BLOCK
        qn = lax.dynamic_slice_in_dim(q_nope, start, BLOCK, axis=1)
        qr = lax.dynamic_slice_in_dim(q_rope, start, BLOCK, axis=1)
        s = (jnp.einsum('bqhd,bkhd->bhqk', qn, k_nope)
             + jnp.einsum('bqhr,bkr->bhqk', qr, k_rope)).astype(jnp.float32) * scale
        q_idx = start + jnp.arange(BLOCK)
        causal = key_idx[None, :] <= q_idx[:, None]
        s = jnp.where(causal, s, NEG_INF)
        p = jax.nn.softmax(s, axis=-1).astype(v.dtype)
        return jnp.einsum('bhqk,bkhd->bqhd', p, v)

    out = lax.map(one_block, jnp.arange(nb))
    return out.transpose(1, 0, 2, 3, 4).reshape(B, S, H * MLA_V)


def mla_mixer(h, positions, w_in, q_norm, w_uq, kv_norm, w_ukv):
    B, S, _ = h.shape
    proj = h @ w_in
    c_q, c_kv, k_rope, gate = jnp.split(
        proj, [MLA_Q_RANK, MLA_Q_RANK + MLA_KV_RANK, MLA_Q_RANK + MLA_KV_RANK + MLA_ROPE], axis=-1)
    q = (rmsnorm(c_q, q_norm) @ w_uq).reshape(B, S, MLA_HEADS, MLA_NOPE + MLA_ROPE)
    q_nope = q[..., :MLA_NOPE]
    q_rope = rope(q[..., MLA_NOPE:], positions, MLA_ROPE)
    kv = (rmsnorm(c_kv, kv_norm) @ w_ukv).reshape(B, S, MLA_HEADS, MLA_NOPE + MLA_V)
    k_nope, v = kv[..., :MLA_NOPE], kv[..., MLA_NOPE:]
    k_rope = rope(k_rope[:, :, None, :], positions, MLA_ROPE)[:, :, 0, :]
    y = mla_causal_attention(q_nope, q_rope, k_nope, k_rope, v)
    return y, gate


def swa_sink_attention(q, k, v, sinks):
    B, S, _, D = q.shape
    nb = S // BLOCK
    qb = q.reshape(B, nb, BLOCK, SWA_KV_HEADS, SWA_GROUP, D)

    def band(t):
        tb = t.reshape(B, nb, BLOCK, SWA_KV_HEADS, D)
        prev = jnp.concatenate([jnp.zeros_like(tb[:, :1]), tb[:, :-1]], axis=1)
        return jnp.concatenate([prev, tb], axis=2)

    kw, vw = band(k), band(v)
    s = jnp.einsum('bnqhgd,bnkhd->bnhgqk', qb, kw).astype(jnp.float32) * (D ** -0.5)
    blk = jnp.arange(nb)[:, None, None]
    qi = jnp.arange(BLOCK)[None, :, None]
    kk = jnp.arange(2 * BLOCK)[None, None, :]
    diff = qi + BLOCK - kk
    key_abs = blk * BLOCK - BLOCK + kk
    valid = (diff >= 0) & (diff < SWA_WINDOW) & (key_abs >= 0)
    s = jnp.where(valid[None, :, None, None], s, NEG_INF)
    sink = sinks.astype(jnp.float32).reshape(1, 1, SWA_KV_HEADS, SWA_GROUP, 1, 1)
    m = jnp.maximum(jnp.max(s, axis=-1, keepdims=True), sink)
    e = jnp.exp(s - m)
    p = e / (jnp.sum(e, axis=-1, keepdims=True) + jnp.exp(sink - m))
    o = jnp.einsum('bnhgqk,bnkhd->bnqhgd', p.astype(v.dtype), vw)
    return o.reshape(B, S, SWA_WIDTH)


def swa_mixer(h, positions, w_in, sinks):
    B, S, _ = h.shape
    proj = h @ w_in
    q, k, v, gate = jnp.split(
        proj, [SWA_WIDTH, SWA_WIDTH + SWA_KV_WIDTH, SWA_WIDTH + 2 * SWA_KV_WIDTH], axis=-1)
    q = rope(q.reshape(B, S, SWA_Q_HEADS, SWA_HEAD_DIM), positions, SWA_ROPE_DIM)
    k = rope(k.reshape(B, S, SWA_KV_HEADS, SWA_HEAD_DIM), positions, SWA_ROPE_DIM)
    v = v.reshape(B, S, SWA_KV_HEADS, SWA_HEAD_DIM)
    y = swa_sink_attention(q, k, v, sinks)
    return y, gate


def setup_inputs(seed: int = 0) -> dict:
    key = jax.random.key(seed)
    ks = jax.random.split(key, 16)
    f32 = jnp.float32

    def nrm(k, shape, fan_in):
        return jax.random.normal(k, shape, f32) * fan_in ** -0.5

    def gain(k, shape):
        return 1.0 + 0.05 * jax.random.normal(k, shape, f32)

    x = jax.random.normal(ks[0], (BATCH, SEQ, D_MODEL), f32)
    offset = jax.random.randint(ks[1], (BATCH, 1), 0, POS_OFFSET_MAX, dtype=jnp.int32)
    positions = offset + jnp.arange(SEQ, dtype=jnp.int32)[None, :]
    return {
        "x": x,
        "positions": positions,
        "layer_norm": gain(ks[2], (DEPTH, D_MODEL)),
        "mla_w_in": nrm(ks[3], (N_MLA, D_MODEL, MLA_IN), D_MODEL),
        "mla_q_norm": gain(ks[4], (N_MLA, MLA_Q_RANK)),
        "mla_w_uq": nrm(ks[5], (N_MLA, MLA_Q_RANK, MLA_HEADS * (MLA_NOPE + MLA_ROPE)), MLA_Q_RANK),
        "mla_kv_norm": gain(ks[6], (N_MLA, MLA_KV_RANK)),
        "mla_w_ukv": nrm(ks[7], (N_MLA, MLA_KV_RANK, MLA_HEADS * (MLA_NOPE + MLA_V)), MLA_KV_RANK),
        "mla_w_out": nrm(ks[8], (N_MLA, MLA_WIDTH, D_MODEL), MLA_WIDTH),
        "swa_w_in": nrm(ks[9], (N_SWA, D_MODEL, SWA_IN), D_MODEL),
        "swa_sinks": jax.random.normal(ks[10], (N_SWA, SWA_Q_HEADS), f32),
        "swa_w_out": nrm(ks[11], (N_SWA, SWA_WIDTH, D_MODEL), SWA_WIDTH),
        "final_norm": gain(ks[12], (D_MODEL,)),
    }


def reference(x, positions, layer_norm, mla_w_in, mla_q_norm, mla_w_uq, mla_kv_norm,
              mla_w_ukv, mla_w_out, swa_w_in, swa_sinks, swa_w_out, final_norm):
    for i in range(DEPTH):
        h = rmsnorm(x, layer_norm[i])
        j = i // N_MIXERS
        if i % N_MIXERS == 0:
            y, gate = mla_mixer(h, positions, mla_w_in[j], mla_q_norm[j], mla_w_uq[j],
                                mla_kv_norm[j], mla_w_ukv[j])
            w_out = mla_w_out[j]
        else:
            y, gate = swa_mixer(h, positions, swa_w_in[j], swa_sinks[j])
            w_out = swa_w_out[j]
        x = x + (y * jax.nn.silu(gate)) @ w_out
    return rmsnorm(x, final_norm)
```

```python
import functools
import math

import jax
import jax.numpy as jnp
from jax import lax
from jax.experimental import pallas as pl
from jax.experimental.pallas import tpu as pltpu

D_MODEL = 2048
DEPTH = 4
N_MIXERS = 2
ROPE_THETA = 500000.0
NORM_EPS = 1e-6
BLOCK = 128
NEG_INF = -1e30

MLA_HEADS = 16
MLA_Q_RANK = 512
MLA_KV_RANK = 512
MLA_NOPE = 128
MLA_ROPE = 64
MLA_V = 128
MLA_WIDTH = MLA_HEADS * MLA_V

SWA_Q_HEADS = 32
SWA_KV_HEADS = 4
SWA_GROUP = SWA_Q_HEADS // SWA_KV_HEADS
SWA_HEAD_DIM = 64
SWA_ROPE_DIM = SWA_HEAD_DIM // 4
SWA_WIDTH = SWA_Q_HEADS * SWA_HEAD_DIM
SWA_KV_WIDTH = SWA_KV_HEADS * SWA_HEAD_DIM

LANES = 128
VMEM_LIMIT_BYTES = 56 * 1024 * 1024

PROJ_ROWS = 256
OUT_ROWS = 512
FLASH_BLOCK = 512

BF16 = jnp.bfloat16
F32 = jnp.float32


def _dot(a, b):
    return jnp.dot(a, b, preferred_element_type=F32)


def _dot_nt(a, b):
    return lax.dot_general(a, b, (((1,), (1,)), ((), ())), preferred_element_type=F32)


def _rmsnorm(xf, g):
    y = xf * lax.rsqrt(jnp.mean(xf * xf, axis=-1, keepdims=True) + NORM_EPS)
    return y * g


def _tile_lanes(t, width):
    reps = width // t.shape[1]
    return t if reps == 1 else jnp.concatenate([t] * reps, axis=1)


def _lane_index(shape):
    return lax.broadcasted_iota(jnp.int32, shape, 1)


def _rope_half64(x, cos_t, sin_t):
    w = x.shape[1]
    fwd = pltpu.roll(x, w - MLA_ROPE // 2, 1)
    bwd = pltpu.roll(x, MLA_ROPE // 2, 1)
    first_half = (_lane_index(x.shape) % MLA_ROPE) < (MLA_ROPE // 2)
    swapped = jnp.where(first_half, fwd, bwd)
    return x * _tile_lanes(cos_t, w) + swapped * _tile_lanes(sin_t, w)


def _rope_partial(x, c_t, s1_t, s2_t):
    w = x.shape[1]
    half = SWA_ROPE_DIM // 2
    fwd = pltpu.roll(x, w - half, 1)
    bwd = pltpu.roll(x, half, 1)
    return (x * _tile_lanes(c_t, w) + fwd * _tile_lanes(s1_t, w)
            + bwd * _tile_lanes(s2_t, w))


def _rope_table_kernel(pos_ref, const_ref, mcos_ref, msin_ref, sc_ref, s1_ref, s2_ref):
    pos = pos_ref[...].astype(F32)
    ang_m = pos * const_ref[0:1, :]
    mcos_ref[...] = jnp.cos(ang_m)
    msin_ref[...] = jnp.sin(ang_m) * const_ref[1:2, :]
    ang_s = pos * const_ref[2:3, :]
    sin_s = jnp.sin(ang_s)
    sc_ref[...] = jnp.cos(ang_s)
    s1_ref[...] = sin_s * const_ref[3:4, :]
    s2_ref[...] = sin_s * const_ref[4:5, :]


def _rope_constants():
    lane = jnp.arange(LANES)
    f_mla = ROPE_THETA ** (-jnp.arange(0, MLA_ROPE, 2, dtype=F32) / MLA_ROPE)
    f_swa = ROPE_THETA ** (-jnp.arange(0, SWA_ROPE_DIM, 2, dtype=F32) / SWA_ROPE_DIM)
    half_m = MLA_ROPE // 2
    half_s = SWA_ROPE_DIM // 2
    d_m = lane % MLA_ROPE
    d_s = lane % SWA_HEAD_DIM
    rows = [
        f_mla[d_m % half_m],
        jnp.where(d_m < half_m, -1.0, 1.0),
        jnp.where(d_s < SWA_ROPE_DIM, f_swa[d_s % half_s], 0.0),
        jnp.where(d_s < half_s, -1.0, 0.0),
        jnp.where((d_s >= half_s) & (d_s < SWA_ROPE_DIM), 1.0, 0.0),
    ]
    rows += [jnp.zeros((LANES,), F32)] * 3
    return jnp.stack([r.astype(F32) for r in rows])


def _rope_tables(positions):
    tokens = positions.size
    rows = 1024
    tab = jax.ShapeDtypeStruct((tokens, LANES), F32)
    spec = pl.BlockSpec((rows, LANES), lambda i: (i, 0))
    return pl.pallas_call(
        _rope_table_kernel,
        grid=(tokens // rows,),
        in_specs=[pl.BlockSpec((rows, 1), lambda i: (i, 0)),
                  pl.BlockSpec((8, LANES), lambda i: (0, 0))],
        out_specs=[spec] * 5,
        out_shape=[tab] * 5,
        name="rope_tables",
    )(positions.reshape(tokens, 1), _rope_constants())


def _mla_in_kernel(x_ref, ln_ref, wq_ref, wkv_ref, wg_ref, wkr_ref, qnorm_ref, kvnorm_ref,
                   wuqn_ref, wuqr_ref, wukn_ref, wuv_ref, cos_ref, sin_ref,
                   qn_ref, qr_ref, kn_ref, v_ref, krp_ref, g_ref):
    scale = (MLA_NOPE + MLA_ROPE) ** -0.5
    h = _rmsnorm(x_ref[...], ln_ref[...]).astype(BF16)
    cos_t = cos_ref[...]
    sin_t = sin_ref[...]

    cq = _rmsnorm(_dot(h, wq_ref[...]), qnorm_ref[...]).astype(BF16)
    qn_ref[...] = (_dot(cq, wuqn_ref[...]) * scale).astype(BF16)
    qr = _rope_half64(_dot(cq, wuqr_ref[...]), cos_t, sin_t)
    qr_ref[...] = (qr * scale).astype(BF16)

    ckv = _rmsnorm(_dot(h, wkv_ref[...]), kvnorm_ref[...]).astype(BF16)
    kn_ref[...] = _dot(ckv, wukn_ref[...]).astype(BF16)
    v_ref[...] = _dot(ckv, wuv_ref[...]).astype(BF16)

    kr = _rope_half64(_dot(h, wkr_ref[...]), cos_t, sin_t)
    low = _lane_index(kr.shape) < MLA_ROPE
    krp_ref[:, :LANES] = jnp.where(low, kr, 0.0).astype(BF16)
    krp_ref[:, LANES:] = jnp.where(low, 0.0, kr).astype(BF16)

    gate = _dot(h, wg_ref[...])
    g_ref[...] = (gate * jax.nn.sigmoid(gate)).astype(BF16)


def _const_spec(shape):
    return pl.BlockSpec(shape, lambda i: (0,) * len(shape), pipeline_mode=pl.Buffered(1))


def _mla_in(x, ln, w, cos_t, sin_t):
    tokens = x.shape[0]
    rows = PROJ_ROWS
    row_spec = lambda width: pl.BlockSpec((rows, width), lambda i: (i, 0))
    consts = [ln, w["wq"], w["wkv"], w["wg"], w["wkr"], w["qnorm"], w["kvnorm"],
              w["wuqn"], w["wuqr"], w["wukn"], w["wuv"]]
    out_widths = [MLA_HEADS * MLA_NOPE, MLA_HEADS * MLA_ROPE, MLA_HEADS * MLA_NOPE,
                  MLA_WIDTH, 2 * LANES, MLA_WIDTH]
    return pl.pallas_call(
        _mla_in_kernel,
        grid=(tokens // rows,),
        in_specs=[row_spec(D_MODEL)] + [_const_spec(c.shape) for c in consts]
                 + [row_spec(LANES), row_spec(LANES)],
        out_specs=[row_spec(wd) for wd in out_widths],
        out_shape=[jax.ShapeDtypeStruct((tokens, wd), BF16) for wd in out_widths],
        compiler_params=pltpu.CompilerParams(
            dimension_semantics=("arbitrary",), vmem_limit_bytes=VMEM_LIMIT_BYTES),
        name="mla_in",
    )(x, *consts, cos_t, sin_t)


def _mla_flash_kernel(qn_ref, qr_ref, kn_ref, kr_ref, v_ref, g_ref, o_ref,
                      kcat_sc, m_sc, l_sc, acc_sc, *, seq):
    blk = FLASH_BLOCK
    kcat_sc[:, :LANES] = kn_ref[...]
    kcat_sc[:, LANES:] = kr_ref[...]
    tri = (lax.broadcasted_iota(jnp.int32, (blk, blk), 1)
           <= lax.broadcasted_iota(jnp.int32, (blk, blk), 0))

    for qi in range(seq // blk):
        rows = slice(qi * blk, (qi + 1) * blk)
        q = jnp.concatenate([qn_ref[rows, :], qr_ref[rows, :]], axis=1)
        m_sc[...] = jnp.full(m_sc.shape, NEG_INF, F32)
        l_sc[...] = jnp.zeros(l_sc.shape, F32)
        acc_sc[...] = jnp.zeros(acc_sc.shape, F32)

        def step(kstart, masked, q=q):
            s = _dot_nt(q, kcat_sc[pl.ds(kstart, blk), :])
            if masked:
                s = jnp.where(tri, s, NEG_INF)
            m_prev = m_sc[...]
            m_new = jnp.maximum(m_prev, jnp.max(s, axis=1, keepdims=True))
            alpha = jnp.exp(m_prev - m_new)
            p = jnp.exp(s - m_new)
            l_sc[...] = alpha * l_sc[...] + jnp.sum(p, axis=1, keepdims=True)
            acc_sc[...] = alpha * acc_sc[...] + _dot(p.astype(BF16), v_ref[pl.ds(kstart, blk), :])
            m_sc[...] = m_new

        if qi > 0:
            def body(j, carry):
                step(pl.multiple_of(j * blk, blk), False)
                return carry
            lax.fori_loop(0, qi, body, 0)
        step(qi * blk, True)

        y = acc_sc[...] * (1.0 / l_sc[...])
        o_ref[rows, :] = (y * g_ref[rows, :].astype(F32)).astype(BF16)


def _mla_flash(qn, qr, kn, krp, v, g, batch, seq):
    tokens = batch * seq
    head_spec = pl.BlockSpec((seq, LANES), lambda b, h: (b, h))
    return pl.pallas_call(
        functools.partial(_mla_flash_kernel, seq=seq),
        grid=(batch, MLA_HEADS),
        in_specs=[head_spec,
                  pl.BlockSpec((seq, LANES), lambda b, h: (b, h // 2)),
                  head_spec,
                  pl.BlockSpec((seq, LANES), lambda b, h: (b, h % 2)),
                  head_spec, head_spec],
        out_specs=head_spec,
        out_shape=jax.ShapeDtypeStruct((tokens, MLA_WIDTH), BF16),
        scratch_shapes=[pltpu.VMEM((seq, 2 * LANES), BF16),
                        pltpu.VMEM((FLASH_BLOCK, 1), F32),
                        pltpu.VMEM((FLASH_BLOCK, 1), F32),
                        pltpu.VMEM((FLASH_BLOCK, MLA_V), F32)],
        compiler_params=pltpu.CompilerParams(
            dimension_semantics=("arbitrary", "arbitrary"), vmem_limit_bytes=VMEM_LIMIT_BYTES),
        name="mla_flash",
    )(qn, qr, kn, krp, v, g)


def _dup_heads(t):
    parts = []
    for hh in range(t.shape[1] // SWA_HEAD_DIM):
        piece = t[:, hh * SWA_HEAD_DIM:(hh + 1) * SWA_HEAD_DIM]
        parts += [piece, piece]
    return jnp.concatenate(parts, axis=1)


def _swa_in_kernel(x_ref, ln_ref, wq_ref, wk_ref, wv_ref, wg_ref, c_ref, s1_ref, s2_ref,
                   q_ref, kd_ref, vd_ref, g_ref):
    scale = SWA_HEAD_DIM ** -0.5
    h = _rmsnorm(x_ref[...], ln_ref[...]).astype(BF16)
    c_t, s1_t, s2_t = c_ref[...], s1_ref[...], s2_ref[...]
    q = _rope_partial(_dot(h, wq_ref[...]), c_t, s1_t, s2_t)
    q_ref[...] = (q * scale).astype(BF16)
    k = _rope_partial(_dot(h, wk_ref[...]), c_t, s1_t, s2_t)
    kd_ref[...] = _dup_heads(k).astype(BF16)
    vd_ref[...] = _dup_heads(_dot(h, wv_ref[...])).astype(BF16)
    gate = _dot(h, wg_ref[...])
    g_ref[...] = (gate * jax.nn.sigmoid(gate)).astype(BF16)


def _swa_in(x, ln, w, c_t, s1_t, s2_t):
    tokens = x.shape[0]
    rows = PROJ_ROWS
    row_spec = lambda width: pl.BlockSpec((rows, width), lambda i: (i, 0))
    consts = [ln, w["wq"], w["wk"], w["wv"], w["wg"]]
    out_widths = [SWA_WIDTH, SWA_KV_HEADS * LANES, SWA_KV_HEADS * LANES, SWA_WIDTH]
    return pl.pallas_call(
        _swa_in_kernel,
        grid=(tokens // rows,),
        in_specs=[row_spec(D_MODEL)] + [_const_spec(c.shape) for c in consts]
                 + [row_spec(LANES)] * 3,
        out_specs=[row_spec(wd) for wd in out_widths],
        out_shape=[jax.ShapeDtypeStruct((tokens, wd), BF16) for wd in out_widths],
        compiler_params=pltpu.CompilerParams(
            dimension_semantics=("arbitrary",), vmem_limit_bytes=VMEM_LIMIT_BYTES),
        name="swa_in",
    )(x, *consts, c_t, s1_t, s2_t)


def _swa_attn_kernel(sink_ref, q_ref, kc_ref, kp_ref, vc_ref, vp_ref, g_ref, o_ref):
    has_prev = pl.program_id(1) > 0
    tri = (lax.broadcasted_iota(jnp.int32, (BLOCK, BLOCK), 1)
           <= lax.broadcasted_iota(jnp.int32, (BLOCK, BLOCK), 0))
    low = _lane_index((BLOCK, LANES)) < SWA_HEAD_DIM
    zero = jnp.zeros((BLOCK, LANES), BF16)

    for kvh in range(SWA_KV_HEADS):
        cols = slice(kvh * LANES, (kvh + 1) * LANES)
        kc, kp, vc, vp = kc_ref[:, cols], kp_ref[:, cols], vc_ref[:, cols], vp_ref[:, cols]
        k_lo = jnp.concatenate([jnp.where(low, kp, zero), jnp.where(low, kc, zero)], axis=0)
        k_hi = jnp.concatenate([jnp.where(low, zero, kp), jnp.where(low, zero, kc)], axis=0)
        v_lo = jnp.concatenate([jnp.where(low, vp, zero), jnp.where(low, vc, zero)], axis=0)
        v_hi = jnp.concatenate([jnp.where(low, zero, vp), jnp.where(low, zero, vc)], axis=0)

        for pair in range(SWA_GROUP // 2):
            head = kvh * SWA_GROUP + 2 * pair
            pcols = slice(head * SWA_HEAD_DIM, (head + 2) * SWA_HEAD_DIM)
            qp = q_ref[:, pcols]
            out = None
            for sub, (k_sel, v_sel) in enumerate(((k_lo, v_lo), (k_hi, v_hi))):
                s = _dot_nt(qp, k_sel)
                s_prev = jnp.where(has_prev, s[:, :BLOCK], NEG_INF)
                s = jnp.where(tri, s[:, BLOCK:], s_prev)
                sink = sink_ref[head + sub]
                m = jnp.maximum(jnp.max(s, axis=1, keepdims=True), sink)
                e = jnp.exp(s - m)
                denom = jnp.sum(e, axis=1, keepdims=True) + jnp.exp(sink - m)
                p = (e * (1.0 / denom)).astype(BF16)
                p_cat = jnp.concatenate([jnp.where(tri, jnp.zeros_like(p), p),
                                         jnp.where(tri, p, jnp.zeros_like(p))], axis=1)
                o = _dot(p_cat, v_sel)
                out = o if out is None else out + o
            o_ref[:, pcols] = (out * g_ref[:, pcols].astype(F32)).astype(BF16)


def _swa_attn(sinks, q, kd, vd, g, batch, seq):
    tokens = batch * seq
    nb = seq // BLOCK
    cur = lambda b, i: (b * nb + i, 0)
    prev = lambda b, i: (b * nb + jnp.maximum(i - 1, 0), 0)
    kv_width = SWA_KV_HEADS * LANES
    return pl.pallas_call(
        _swa_attn_kernel,
        grid=(batch, nb),
        in_specs=[pl.BlockSpec(memory_space=pltpu.SMEM),
                  pl.BlockSpec((BLOCK, SWA_WIDTH), cur),
                  pl.BlockSpec((BLOCK, kv_width), cur),
                  pl.BlockSpec((BLOCK, kv_width), prev),
                  pl.BlockSpec((BLOCK, kv_width), cur),
                  pl.BlockSpec((BLOCK, kv_width), prev),
                  pl.BlockSpec((BLOCK, SWA_WIDTH), cur)],
        out_specs=pl.BlockSpec((BLOCK, SWA_WIDTH), cur),
        out_shape=jax.ShapeDtypeStruct((tokens, SWA_WIDTH), BF16),
        compiler_params=pltpu.CompilerParams(
            dimension_semantics=("arbitrary", "arbitrary"), vmem_limit_bytes=VMEM_LIMIT_BYTES),
        name="swa_attn",
    )(sinks, q, kd, kd, vd, vd, g)


def _out_kernel(x_ref, y_ref, w_ref, fn_ref, o_ref, *, final):
    x_new = x_ref[...] + _dot(y_ref[...], w_ref[...])
    o_ref[...] = _rmsnorm(x_new, fn_ref[...]) if final else x_new


def _out_proj(x, y, w_out, final_norm, final):
    tokens = x.shape[0]
    rows = OUT_ROWS
    row_spec = lambda width: pl.BlockSpec((rows, width), lambda i: (i, 0))
    return pl.pallas_call(
        functools.partial(_out_kernel, final=final),
        grid=(tokens // rows,),
        in_specs=[row_spec(D_MODEL), row_spec(y.shape[1]), _const_spec(w_out.shape),
                  _const_spec(final_norm.shape)],
        out_specs=row_spec(D_MODEL),
        out_shape=jax.ShapeDtypeStruct((tokens, D_MODEL), F32),
        compiler_params=pltpu.CompilerParams(
            dimension_semantics=("arbitrary",), vmem_limit_bytes=VMEM_LIMIT_BYTES),
        name="out_proj",
    )(x, y, w_out, final_norm)


def _mla_weights(w_in, q_norm, w_uq, kv_norm, w_ukv):
    o1 = MLA_Q_RANK
    o2 = o1 + MLA_KV_RANK
    o3 = o2 + MLA_ROPE
    w_kr = w_in[:, o2:o3]
    uq = w_uq.reshape(MLA_Q_RANK, MLA_HEADS, MLA_NOPE + MLA_ROPE)
    ukv = w_ukv.reshape(MLA_KV_RANK, MLA_HEADS, MLA_NOPE + MLA_V)
    return {
        "wq": w_in[:, :o1].astype(BF16),
        "wkv": w_in[:, o1:o2].astype(BF16),
        "wkr": jnp.concatenate([w_kr, w_kr], axis=1).astype(BF16),
        "wg": w_in[:, o3:].astype(BF16),
        "qnorm": q_norm.reshape(1, -1),
        "kvnorm": kv_norm.reshape(1, -1),
        "wuqn": uq[:, :, :MLA_NOPE].reshape(MLA_Q_RANK, -1).astype(BF16),
        "wuqr": uq[:, :, MLA_NOPE:].reshape(MLA_Q_RANK, -1).astype(BF16),
        "wukn": ukv[:, :, :MLA_NOPE].reshape(MLA_KV_RANK, -1).astype(BF16),
        "wuv": ukv[:, :, MLA_NOPE:].reshape(MLA_KV_RANK, -1).astype(BF16),
    }


def _swa_weights(w_in):
    o1 = SWA_WIDTH
    o2 = o1 + SWA_KV_WIDTH
    o3 = o2 + SWA_KV_WIDTH
    return {
        "wq": w_in[:, :o1].astype(BF16),
        "wk": w_in[:, o1:o2].astype(BF16),
        "wv": w_in[:, o2:o3].astype(BF16),
        "wg": w_in[:, o3:].astype(BF16),
    }


def kernel(x, positions, layer_norm, mla_w_in, mla_q_norm, mla_w_uq, mla_kv_norm, mla_w_ukv,
           mla_w_out, swa_w_in, swa_sinks, swa_w_out, final_norm):
    batch, seq, d_model = x.shape
    tokens = batch * seq
    assert d_model == D_MODEL and seq % FLASH_BLOCK == 0 and tokens % OUT_ROWS == 0
    mcos, msin, sc, s1, s2 = _rope_tables(positions)
    xt = x.reshape(tokens, d_model)
    fn = final_norm.reshape(1, -1)
    for i in range(DEPTH):
        j = i // N_MIXERS
        ln = layer_norm[i].reshape(1, -1)
        final = i == DEPTH - 1
        if i % N_MIXERS == 0:
            w = _mla_weights(mla_w_in[j], mla_q_norm[j], mla_w_uq[j], mla_kv_norm[j], mla_w_ukv[j])
            qn, qr, kn, v, krp, g = _mla_in(xt, ln, w, mcos, msin)
            y = _mla_flash(qn, qr, kn, krp, v, g, batch, seq)
            w_out = mla_w_out[j]
        else:
            w = _swa_weights(swa_w_in[j])
            q, kd, vd, g = _swa_in(xt, ln, w, sc, s1, s2)
            y = _swa_attn(swa_sinks[j], q, kd, vd, g, batch, seq)
            w_out = swa_w_out[j]
        xt = _out_proj(xt, y, w_out.astype(BF16), fn, final)
    return xt.reshape(batch, seq, d_model)
```

```python
import functools
import math

import jax
import jax.numpy as jnp
from jax import lax
from jax.experimental import pallas as pl
from jax.experimental.pallas import tpu as pltpu

D_MODEL = 2048
DEPTH = 4
N_MIXERS = 2
ROPE_THETA = 500000.0
NORM_EPS = 1e-6
BLOCK = 128
NEG_INF = -1e30

MLA_HEADS = 16
MLA_Q_RANK = 512
MLA_KV_RANK = 512
MLA_NOPE = 128
MLA_ROPE = 64
MLA_V = 128
MLA_WIDTH = MLA_HEADS * MLA_V

SWA_Q_HEADS = 32
SWA_KV_HEADS = 4
SWA_GROUP = SWA_Q_HEADS // SWA_KV_HEADS
SWA_HEAD_DIM = 64
SWA_ROPE_DIM = SWA_HEAD_DIM // 4
SWA_WIDTH = SWA_Q_HEADS * SWA_HEAD_DIM
SWA_KV_WIDTH = SWA_KV_HEADS * SWA_HEAD_DIM

LANES = 128
VMEM_LIMIT_BYTES = 56 * 1024 * 1024

PROJ_ROWS = 256
OUT_ROWS = 512
FLASH_BLOCK = 512

BF16 = jnp.bfloat16
F32 = jnp.float32


def _dot(a, b):
    return jnp.dot(a, b, preferred_element_type=F32)


def _dot_nt(a, b):
    return lax.dot_general(a, b, (((1,), (1,)), ((), ())), preferred_element_type=F32)


def _rmsnorm(xf, g):
    y = xf * lax.rsqrt(jnp.mean(xf * xf, axis=-1, keepdims=True) + NORM_EPS)
    return y * g


def _tile_lanes(t, width):
    reps = width // t.shape[1]
    return t if reps == 1 else jnp.concatenate([t] * reps, axis=1)


def _lane_index(shape):
    return lax.broadcasted_iota(jnp.int32, shape, 1)


def _rope_half64(x, cos_t, sin_t):
    w = x.shape[1]
    fwd = pltpu.roll(x, w - MLA_ROPE // 2, 1)
    bwd = pltpu.roll(x, MLA_ROPE // 2, 1)
    first_half = (_lane_index(x.shape) % MLA_ROPE) < (MLA_ROPE // 2)
    swapped = jnp.where(first_half, fwd, bwd)
    return x * _tile_lanes(cos_t, w) + swapped * _tile_lanes(sin_t, w)


def _rope_partial(x, c_t, s1_t, s2_t):
    w = x.shape[1]
    half = SWA_ROPE_DIM // 2
    fwd = pltpu.roll(x, w - half, 1)
    bwd = pltpu.roll(x, half, 1)
    return (x * _tile_lanes(c_t, w) + fwd * _tile_lanes(s1_t, w)
            + bwd * _tile_lanes(s2_t, w))


def _rope_table_kernel(pos_ref, const_ref, mcos_ref, msin_ref, sc_ref, s1_ref, s2_ref):
    pos = pos_ref[...].astype(F32)
    ang_m = pos * const_ref[0:1, :]
    mcos_ref[...] = jnp.cos(ang_m)
    msin_ref[...] = jnp.sin(ang_m) * const_ref[1:2, :]
    ang_s = pos * const_ref[2:3, :]
    sin_s = jnp.sin(ang_s)
    sc_ref[...] = jnp.cos(ang_s)
    s1_ref[...] = sin_s * const_ref[3:4, :]
    s2_ref[...] = sin_s * const_ref[4:5, :]


def _rope_constants():
    lane = jnp.arange(LANES)
    f_mla = ROPE_THETA ** (-jnp.arange(0, MLA_ROPE, 2, dtype=F32) / MLA_ROPE)
    f_swa = ROPE_THETA ** (-jnp.arange(0, SWA_ROPE_DIM, 2, dtype=F32) / SWA_ROPE_DIM)
    half_m = MLA_ROPE // 2
    half_s = SWA_ROPE_DIM // 2
    d_m = lane % MLA_ROPE
    d_s = lane % SWA_HEAD_DIM
    rows = [
        f_mla[d_m % half_m],
        jnp.where(d_m < half_m, -1.0, 1.0),
        jnp.where(d_s < SWA_ROPE_DIM, f_swa[d_s % half_s], 0.0),
        jnp.where(d_s < half_s, -1.0, 0.0),
        jnp.where((d_s >= half_s) & (d_s < SWA_ROPE_DIM), 1.0, 0.0),
    ]
    rows += [jnp.zeros((LANES,), F32)] * 3
    return jnp.stack([r.astype(F32) for r in rows])


def _rope_tables(positions):
    tokens = positions.size
    rows = 1024
    tab = jax.ShapeDtypeStruct((tokens, LANES), F32)
    spec = pl.BlockSpec((rows, LANES), lambda i: (i, 0))
    return pl.pallas_call(
        _rope_table_kernel,
        grid=(tokens // rows,),
        in_specs=[pl.BlockSpec((rows, 1), lambda i: (i, 0)),
                  pl.BlockSpec((8, LANES), lambda i: (0, 0))],
        out_specs=[spec] * 5,
        out_shape=[tab] * 5,
        name="rope_tables",
    )(positions.reshape(tokens, 1), _rope_constants())


def _mla_in_kernel(x_ref, ln_ref, wq_ref, wkv_ref, wg_ref, wkr_ref, qnorm_ref, kvnorm_ref,
                   wuqn_ref, wuqr_ref, wukn_ref, wuv_ref, cos_ref, sin_ref,
                   qn_ref, qr_ref, kn_ref, vt_ref, krp_ref, g_ref):
    scale = (MLA_NOPE + MLA_ROPE) ** -0.5 * math.log2(math.e)
    h = _rmsnorm(x_ref[...], ln_ref[...]).astype(BF16)
    cos_t = cos_ref[...]
    sin_t = sin_ref[...]

    cq = _rmsnorm(_dot(h, wq_ref[...]), qnorm_ref[...]).astype(BF16)
    qn_ref[...] = (_dot(cq, wuqn_ref[...]) * scale).astype(BF16)
    qr = _rope_half64(_dot(cq, wuqr_ref[...]), cos_t, sin_t)
    qr_ref[...] = (qr * scale).astype(BF16)

    ckv = _rmsnorm(_dot(h, wkv_ref[...]), kvnorm_ref[...]).astype(BF16)
    kn_ref[...] = _dot(ckv, wukn_ref[...]).astype(BF16)
    vt_ref[0] = _dot_nt(wuv_ref[...], ckv).astype(BF16)

    kr = _rope_half64(_dot(h, wkr_ref[...]), cos_t, sin_t)
    low = _lane_index(kr.shape) < MLA_ROPE
    krp_ref[:, :LANES] = jnp.where(low, kr, 0.0).astype(BF16)
    krp_ref[:, LANES:] = jnp.where(low, 0.0, kr).astype(BF16)

    gate = _dot(h, wg_ref[...])
    g_ref[...] = (gate * jax.nn.sigmoid(gate)).astype(BF16)


def _const_spec(shape):
    return pl.BlockSpec(shape, lambda i: (0,) * len(shape), pipeline_mode=pl.Buffered(1))


def _mla_in(x, ln, w, cos_t, sin_t):
    tokens = x.shape[0]
    rows = PROJ_ROWS
    row_spec = lambda width: pl.BlockSpec((rows, width), lambda i: (i, 0))
    consts = [ln, w["wq"], w["wkv"], w["wg"], w["wkr"], w["qnorm"], w["kvnorm"],
              w["wuqn"], w["wuqr"], w["wukn"], w["wuv"]]
    out_widths = [MLA_HEADS * MLA_NOPE, MLA_HEADS * MLA_ROPE, MLA_HEADS * MLA_NOPE,
                  None, 2 * LANES, MLA_WIDTH]
    per_blk = FLASH_BLOCK // rows
    vt_spec = pl.BlockSpec((1, MLA_WIDTH, rows), lambda i: (i // per_blk, 0, i % per_blk))
    vt_shape = jax.ShapeDtypeStruct((tokens // FLASH_BLOCK, MLA_WIDTH, FLASH_BLOCK), BF16)
    return pl.pallas_call(
        _mla_in_kernel,
        grid=(tokens // rows,),
        in_specs=[row_spec(D_MODEL)] + [_const_spec(c.shape) for c in consts]
                 + [row_spec(LANES), row_spec(LANES)],
        out_specs=[vt_spec if wd is None else row_spec(wd) for wd in out_widths],
        out_shape=[vt_shape if wd is None else jax.ShapeDtypeStruct((tokens, wd), BF16)
                   for wd in out_widths],
        compiler_params=pltpu.CompilerParams(
            dimension_semantics=("arbitrary",), vmem_limit_bytes=VMEM_LIMIT_BYTES),
        name="mla_in",
    )(x, *consts, cos_t, sin_t)


def _mla_flash_kernel(qn_ref, qr_ref, kn_ref, kr_ref, vt_ref, g_ref, o_ref,
                      kcat_sc, m_sc, l_sc, acc_sc, *, seq):
    blk = FLASH_BLOCK
    pair = (0, 1)
    hcols = [slice(hh * LANES, (hh + 1) * LANES) for hh in pair]
    for hh in pair:
        kcat_sc[hh, :, :LANES] = kn_ref[:, hcols[hh]]
        kcat_sc[hh, :, LANES:] = kr_ref[:, hcols[hh]]
    causal_t = (lax.broadcasted_iota(jnp.int32, (blk, blk), 0)
                <= lax.broadcasted_iota(jnp.int32, (blk, blk), 1))

    for qi in range(seq // blk):
        rows = slice(qi * blk, (qi + 1) * blk)
        qs = [jnp.concatenate([qn_ref[rows, hcols[hh]], qr_ref[rows, :]], axis=1) for hh in pair]
        m_sc[...] = jnp.full(m_sc.shape, NEG_INF, F32)
        l_sc[...] = jnp.zeros(l_sc.shape, F32)
        acc_sc[...] = jnp.zeros(acc_sc.shape, F32)

        def step(kb, masked, qs=qs):
            kstart = pl.multiple_of(kb * blk, blk)
            for hh in pair:
                st = _dot_nt(kcat_sc[hh, pl.ds(kstart, blk), :], qs[hh])
                if masked:
                    st = jnp.where(causal_t, st, NEG_INF)
                m_prev = m_sc[hh]
                m_new = jnp.maximum(m_prev, jnp.max(st, axis=0, keepdims=True))
                alpha = jnp.exp2(m_prev - m_new)
                p = jnp.exp2(st - m_new)
                l_sc[hh] = alpha * l_sc[hh] + jnp.sum(p, axis=0, keepdims=True)
                acc_sc[hh] = alpha * acc_sc[hh] + _dot(vt_ref[kb, hcols[hh], :], p.astype(BF16))
                m_sc[hh] = m_new

        if qi > 0:
            def body(j, carry):
                step(j, False)
                return carry
            lax.fori_loop(0, qi, body, 0)
        step(qi, True)

        for hh in pair:
            y = (acc_sc[hh] * (1.0 / l_sc[hh])).T
            o_ref[rows, hcols[hh]] = (y * g_ref[rows, hcols[hh]].astype(F32)).astype(BF16)


def _mla_flash(qn, qr, kn, krp, vt, g, batch, seq):
    tokens = batch * seq
    nblk = seq // FLASH_BLOCK
    pair_spec = pl.BlockSpec((seq, 2 * LANES), lambda b, p: (b, p))
    return pl.pallas_call(
        functools.partial(_mla_flash_kernel, seq=seq),
        grid=(batch, MLA_HEADS // 2),
        in_specs=[pair_spec,
                  pl.BlockSpec((seq, LANES), lambda b, p: (b, p)),
                  pair_spec,
                  pl.BlockSpec((seq, 2 * LANES), lambda b, p: (b, 0)),
                  pl.BlockSpec((nblk, 2 * LANES, FLASH_BLOCK), lambda b, p: (b, p, 0)),
                  pair_spec],
        out_specs=pair_spec,
        out_shape=jax.ShapeDtypeStruct((tokens, MLA_WIDTH), BF16),
        scratch_shapes=[pltpu.VMEM((2, seq, 2 * LANES), BF16),
                        pltpu.VMEM((2, 1, FLASH_BLOCK), F32),
                        pltpu.VMEM((2, 1, FLASH_BLOCK), F32),
                        pltpu.VMEM((2, MLA_V, FLASH_BLOCK), F32)],
        compiler_params=pltpu.CompilerParams(
            dimension_semantics=("arbitrary", "arbitrary"), vmem_limit_bytes=VMEM_LIMIT_BYTES),
        name="mla_flash",
    )(qn, qr, kn, krp, vt, g)


def _dup_heads(t):
    parts = []
    for hh in range(t.shape[1] // SWA_HEAD_DIM):
        piece = t[:, hh * SWA_HEAD_DIM:(hh + 1) * SWA_HEAD_DIM]
        parts += [piece, piece]
    return jnp.concatenate(parts, axis=1)


def _swa_in_kernel(x_ref, ln_ref, wq_ref, wk_ref, wv_ref, wg_ref, c_ref, s1_ref, s2_ref,
                   q_ref, kd_ref, vd_ref, g_ref):
    scale = SWA_HEAD_DIM ** -0.5
    h = _rmsnorm(x_ref[...], ln_ref[...]).astype(BF16)
    c_t, s1_t, s2_t = c_ref[...], s1_ref[...], s2_ref[...]
    q = _rope_partial(_dot(h, wq_ref[...]), c_t, s1_t, s2_t)
    q_ref[...] = (q * scale).astype(BF16)
    k = _rope_partial(_dot(h, wk_ref[...]), c_t, s1_t, s2_t)
    kd_ref[...] = _dup_heads(k).astype(BF16)
    vd_ref[...] = _dup_heads(_dot(h, wv_ref[...])).astype(BF16)
    gate = _dot(h, wg_ref[...])
    g_ref[...] = (gate * jax.nn.sigmoid(gate)).astype(BF16)


def _swa_in(x, ln, w, c_t, s1_t, s2_t):
    tokens = x.shape[0]
    rows = PROJ_ROWS
    row_spec = lambda width: pl.BlockSpec((rows, width), lambda i: (i, 0))
    consts = [ln, w["wq"], w["wk"], w["wv"], w["wg"]]
    out_widths = [SWA_WIDTH, SWA_KV_HEADS * LANES, SWA_KV_HEADS * LANES, SWA_WIDTH]
    return pl.pallas_call(
        _swa_in_kernel,
        grid=(tokens // rows,),
        in_specs=[row_spec(D_MODEL)] + [_const_spec(c.shape) for c in consts]
                 + [row_spec(LANES)] * 3,
        out_specs=[row_spec(wd) for wd in out_widths],
        out_shape=[jax.ShapeDtypeStruct((tokens, wd), BF16) for wd in out_widths],
        compiler_params=pltpu.CompilerParams(
            dimension_semantics=("arbitrary",), vmem_limit_bytes=VMEM_LIMIT_BYTES),
        name="swa_in",
    )(x, *consts, c_t, s1_t, s2_t)


def _swa_attn_kernel(sink_ref, q_ref, kc_ref, kp_ref, vc_ref, vp_ref, g_ref, o_ref):
    has_prev = pl.program_id(1) > 0
    tri = (lax.broadcasted_iota(jnp.int32, (BLOCK, BLOCK), 1)
           <= lax.broadcasted_iota(jnp.int32, (BLOCK, BLOCK), 0))
    low = _lane_index((BLOCK, LANES)) < SWA_HEAD_DIM
    zero = jnp.zeros((BLOCK, LANES), BF16)

    for kvh in range(SWA_KV_HEADS):
        cols = slice(kvh * LANES, (kvh + 1) * LANES)
        kc, kp, vc, vp = kc_ref[:, cols], kp_ref[:, cols], vc_ref[:, cols], vp_ref[:, cols]
        k_lo = jnp.concatenate([jnp.where(low, kp, zero), jnp.where(low, kc, zero)], axis=0)
        k_hi = jnp.concatenate([jnp.where(low, zero, kp), jnp.where(low, zero, kc)], axis=0)
        v_lo = jnp.concatenate([jnp.where(low, vp, zero), jnp.where(low, vc, zero)], axis=0)
        v_hi = jnp.concatenate([jnp.where(low, zero, vp), jnp.where(low, zero, vc)], axis=0)

        for pair in range(SWA_GROUP // 2):
            head = kvh * SWA_GROUP + 2 * pair
            pcols = slice(head * SWA_HEAD_DIM, (head + 2) * SWA_HEAD_DIM)
            qp = q_ref[:, pcols]
            out = None
            for sub, (k_sel, v_sel) in enumerate(((k_lo, v_lo), (k_hi, v_hi))):
                s = _dot_nt(qp, k_sel)
                s_prev = jnp.where(has_prev, s[:, :BLOCK], NEG_INF)
                s = jnp.where(tri, s[:, BLOCK:], s_prev)
                sink = sink_ref[head + sub]
                m = jnp.maximum(jnp.max(s, axis=1, keepdims=True), sink)
                e = jnp.exp(s - m)
                denom = jnp.sum(e, axis=1, keepdims=True) + jnp.exp(sink - m)
                p = (e * (1.0 / denom)).astype(BF16)
                p_cat = jnp.concatenate([jnp.where(tri, jnp.zeros_like(p), p),
                                         jnp.where(tri, p, jnp.zeros_like(p))], axis=1)
                o = _dot(p_cat, v_sel)
                out = o if out is None else out + o
            o_ref[:, pcols] = (out * g_ref[:, pcols].astype(F32)).astype(BF16)


def _swa_attn(sinks, q, kd, vd, g, batch, seq):
    tokens = batch * seq
    nb = seq // BLOCK
    cur = lambda b, i: (b * nb + i, 0)
    prev = lambda b, i: (b * nb + jnp.maximum(i - 1, 0), 0)
    kv_width = SWA_KV_HEADS * LANES
    return pl.pallas_call(
        _swa_attn_kernel,
        grid=(batch, nb),
        in_specs=[pl.BlockSpec(memory_space=pltpu.SMEM),
                  pl.BlockSpec((BLOCK, SWA_WIDTH), cur),
                  pl.BlockSpec((BLOCK, kv_width), cur),
                  pl.BlockSpec((BLOCK, kv_width), prev),
                  pl.BlockSpec((BLOCK, kv_width), cur),
                  pl.BlockSpec((BLOCK, kv_width), prev),
                  pl.BlockSpec((BLOCK, SWA_WIDTH), cur)],
        out_specs=pl.BlockSpec((BLOCK, SWA_WIDTH), cur),
        out_shape=jax.ShapeDtypeStruct((tokens, SWA_WIDTH), BF16),
        compiler_params=pltpu.CompilerParams(
            dimension_semantics=("arbitrary", "arbitrary"), vmem_limit_bytes=VMEM_LIMIT_BYTES),
        name="swa_attn",
    )(sinks, q, kd, kd, vd, vd, g)


def _out_kernel(x_ref, y_ref, w_ref, fn_ref, o_ref, *, final):
    x_new = x_ref[...] + _dot(y_ref[...], w_ref[...])
    o_ref[...] = _rmsnorm(x_new, fn_ref[...]) if final else x_new


def _out_proj(x, y, w_out, final_norm, final):
    tokens = x.shape[0]
    rows = OUT_ROWS
    row_spec = lambda width: pl.BlockSpec((rows, width), lambda i: (i, 0))
    return pl.pallas_call(
        functools.partial(_out_kernel, final=final),
        grid=(tokens // rows,),
        in_specs=[row_spec(D_MODEL), row_spec(y.shape[1]), _const_spec(w_out.shape),
                  _const_spec(final_norm.shape)],
        out_specs=row_spec(D_MODEL),
        out_shape=jax.ShapeDtypeStruct((tokens, D_MODEL), F32),
        compiler_params=pltpu.CompilerParams(
            dimension_semantics=("arbitrary",), vmem_limit_bytes=VMEM_LIMIT_BYTES),
        name="out_proj",
    )(x, y, w_out, final_norm)


def _mla_weights(w_in, q_norm, w_uq, kv_norm, w_ukv):
    o1 = MLA_Q_RANK
    o2 = o1 + MLA_KV_RANK
    o3 = o2 + MLA_ROPE
    w_kr = w_in[:, o2:o3]
    uq = w_uq.reshape(MLA_Q_RANK, MLA_HEADS, MLA_NOPE + MLA_ROPE)
    ukv = w_ukv.reshape(MLA_KV_RANK, MLA_HEADS, MLA_NOPE + MLA_V)
    return {
        "wq": w_in[:, :o1].astype(BF16),
        "wkv": w_in[:, o1:o2].astype(BF16),
        "wkr": jnp.concatenate([w_kr, w_kr], axis=1).astype(BF16),
        "wg": w_in[:, o3:].astype(BF16),
        "qnorm": q_norm.reshape(1, -1),
        "kvnorm": kv_norm.reshape(1, -1),
        "wuqn": uq[:, :, :MLA_NOPE].reshape(MLA_Q_RANK, -1).astype(BF16),
        "wuqr": uq[:, :, MLA_NOPE:].reshape(MLA_Q_RANK, -1).astype(BF16),
        "wukn": ukv[:, :, :MLA_NOPE].reshape(MLA_KV_RANK, -1).astype(BF16),
        "wuv": ukv[:, :, MLA_NOPE:].reshape(MLA_KV_RANK, -1).T.astype(BF16),
    }


def _swa_weights(w_in):
    o1 = SWA_WIDTH
    o2 = o1 + SWA_KV_WIDTH
    o3 = o2 + SWA_KV_WIDTH
    return {
        "wq": w_in[:, :o1].astype(BF16),
        "wk": w_in[:, o1:o2].astype(BF16),
        "wv": w_in[:, o2:o3].astype(BF16),
        "wg": w_in[:, o3:].astype(BF16),
    }


def kernel(x, positions, layer_norm, mla_w_in, mla_q_norm, mla_w_uq, mla_kv_norm, mla_w_ukv,
           mla_w_out, swa_w_in, swa_sinks, swa_w_out, final_norm):
    batch, seq, d_model = x.shape
    tokens = batch * seq
    assert d_model == D_MODEL and seq % FLASH_BLOCK == 0 and tokens % OUT_ROWS == 0
    mcos, msin, sc, s1, s2 = _rope_tables(positions)
    xt = x.reshape(tokens, d_model)
    fn = final_norm.reshape(1, -1)
    for i in range(DEPTH):
        j = i // N_MIXERS
        ln = layer_norm[i].reshape(1, -1)
        final = i == DEPTH - 1
        if i % N_MIXERS == 0:
            w = _mla_weights(mla_w_in[j], mla_q_norm[j], mla_w_uq[j], mla_kv_norm[j], mla_w_ukv[j])
            qn, qr, kn, vt, krp, g = _mla_in(xt, ln, w, mcos, msin)
            y = _mla_flash(qn, qr, kn, krp, vt, g, batch, seq)
            w_out = mla_w_out[j]
        else:
            w = _swa_weights(swa_w_in[j])
            q, kd, vd, g = _swa_in(xt, ln, w, sc, s1, s2)
            y = _swa_attn(swa_sinks[j], q, kd, vd, g, batch, seq)
            w_out = swa_w_out[j]
        xt = _out_proj(xt, y, w_out.astype(BF16), fn, final)
    return xt.reshape(batch, seq, d_model)
```

```python
import functools
import math

import jax
import jax.numpy as jnp
from jax import lax
from jax.experimental import pallas as pl
from jax.experimental.pallas import tpu as pltpu

D_MODEL = 2048
DEPTH = 4
N_MIXERS = 2
ROPE_THETA = 500000.0
NORM_EPS = 1e-6
BLOCK = 128
NEG_INF = -1e30

MLA_HEADS = 16
MLA_Q_RANK = 512
MLA_KV_RANK = 512
MLA_NOPE = 128
MLA_ROPE = 64
MLA_V = 128
MLA_WIDTH = MLA_HEADS * MLA_V

SWA_Q_HEADS = 32
SWA_KV_HEADS = 4
SWA_GROUP = SWA_Q_HEADS // SWA_KV_HEADS
SWA_HEAD_DIM = 64
SWA_ROPE_DIM = SWA_HEAD_DIM // 4
SWA_WIDTH = SWA_Q_HEADS * SWA_HEAD_DIM
SWA_KV_WIDTH = SWA_KV_HEADS * SWA_HEAD_DIM

LANES = 128
VMEM_LIMIT_BYTES = 56 * 1024 * 1024

PROJ_ROWS = 256
OUT_ROWS = 512
FLASH_BLOCK = 512

BF16 = jnp.bfloat16
F32 = jnp.float32


def _dot(a, b):
    return jnp.dot(a, b, preferred_element_type=F32)


def _dot_nt(a, b):
    return lax.dot_general(a, b, (((1,), (1,)), ((), ())), preferred_element_type=F32)


def _rmsnorm(xf, g):
    y = xf * lax.rsqrt(jnp.mean(xf * xf, axis=-1, keepdims=True) + NORM_EPS)
    return y * g


def _tile_lanes(t, width):
    reps = width // t.shape[1]
    return t if reps == 1 else jnp.concatenate([t] * reps, axis=1)


def _lane_index(shape):
    return lax.broadcasted_iota(jnp.int32, shape, 1)


def _rope_half64(x, cos_t, sin_t):
    w = x.shape[1]
    fwd = pltpu.roll(x, w - MLA_ROPE // 2, 1)
    bwd = pltpu.roll(x, MLA_ROPE // 2, 1)
    first_half = (_lane_index(x.shape) % MLA_ROPE) < (MLA_ROPE // 2)
    swapped = jnp.where(first_half, fwd, bwd)
    return x * _tile_lanes(cos_t, w) + swapped * _tile_lanes(sin_t, w)


def _rope_partial(x, c_t, s1_t, s2_t):
    w = x.shape[1]
    half = SWA_ROPE_DIM // 2
    fwd = pltpu.roll(x, w - half, 1)
    bwd = pltpu.roll(x, half, 1)
    return (x * _tile_lanes(c_t, w) + fwd * _tile_lanes(s1_t, w)
            + bwd * _tile_lanes(s2_t, w))


def _rope_table_kernel(pos_ref, const_ref, mcos_ref, msin_ref, sc_ref, s1_ref, s2_ref):
    pos = pos_ref[...].astype(F32)
    ang_m = pos * const_ref[0:1, :]
    mcos_ref[...] = jnp.cos(ang_m)
    msin_ref[...] = jnp.sin(ang_m) * const_ref[1:2, :]
    ang_s = pos * const_ref[2:3, :]
    sin_s = jnp.sin(ang_s)
    sc_ref[...] = jnp.cos(ang_s)
    s1_ref[...] = sin_s * const_ref[3:4, :]
    s2_ref[...] = sin_s * const_ref[4:5, :]


def _rope_constants():
    lane = jnp.arange(LANES)
    f_mla = ROPE_THETA ** (-jnp.arange(0, MLA_ROPE, 2, dtype=F32) / MLA_ROPE)
    f_swa = ROPE_THETA ** (-jnp.arange(0, SWA_ROPE_DIM, 2, dtype=F32) / SWA_ROPE_DIM)
    half_m = MLA_ROPE // 2
    half_s = SWA_ROPE_DIM // 2
    d_m = lane % MLA_ROPE
    d_s = lane % SWA_HEAD_DIM
    rows = [
        f_mla[d_m % half_m],
        jnp.where(d_m < half_m, -1.0, 1.0),
        jnp.where(d_s < SWA_ROPE_DIM, f_swa[d_s % half_s], 0.0),
        jnp.where(d_s < half_s, -1.0, 0.0),
        jnp.where((d_s >= half_s) & (d_s < SWA_ROPE_DIM), 1.0, 0.0),
    ]
    rows += [jnp.zeros((LANES,), F32)] * 3
    return jnp.stack([r.astype(F32) for r in rows])


def _rope_tables(positions):
    tokens = positions.size
    rows = 1024
    tab = jax.ShapeDtypeStruct((tokens, LANES), F32)
    spec = pl.BlockSpec((rows, LANES), lambda i: (i, 0))
    return pl.pallas_call(
        _rope_table_kernel,
        grid=(tokens // rows,),
        in_specs=[pl.BlockSpec((rows, 1), lambda i: (i, 0)),
                  pl.BlockSpec((8, LANES), lambda i: (0, 0))],
        out_specs=[spec] * 5,
        out_shape=[tab] * 5,
        name="rope_tables",
    )(positions.reshape(tokens, 1), _rope_constants())


def _mla_in_kernel(x_ref, ln_ref, wq_ref, wkv_ref, wg_ref, wkr_ref, qnorm_ref, kvnorm_ref,
                   wuqn_ref, wuqr_ref, wukn_ref, wuv_ref, cos_ref, sin_ref,
                   qn_ref, qr_ref, kn_ref, vt_ref, krp_ref, g_ref):
    scale = (MLA_NOPE + MLA_ROPE) ** -0.5 * math.log2(math.e)
    h = _rmsnorm(x_ref[...], ln_ref[...]).astype(BF16)
    cos_t = cos_ref[...]
    sin_t = sin_ref[...]

    cq = _rmsnorm(_dot(h, wq_ref[...]), qnorm_ref[...]).astype(BF16)
    qn_ref[...] = (_dot(cq, wuqn_ref[...]) * scale).astype(BF16)
    qr = _rope_half64(_dot(cq, wuqr_ref[...]), cos_t, sin_t)
    qr_ref[...] = (qr * scale).astype(BF16)

    ckv = _rmsnorm(_dot(h, wkv_ref[...]), kvnorm_ref[...]).astype(BF16)
    kn_ref[...] = _dot(ckv, wukn_ref[...]).astype(BF16)
    vt_ref[0] = _dot_nt(wuv_ref[...], ckv).astype(BF16)

    kr = _rope_half64(_dot(h, wkr_ref[...]), cos_t, sin_t)
    low = _lane_index(kr.shape) < MLA_ROPE
    krp_ref[:, :LANES] = jnp.where(low, kr, 0.0).astype(BF16)
    krp_ref[:, LANES:] = jnp.where(low, 0.0, kr).astype(BF16)

    gate = _dot(h, wg_ref[...])
    g_ref[...] = (gate * jax.nn.sigmoid(gate)).astype(BF16)


def _const_spec(shape):
    return pl.BlockSpec(shape, lambda i: (0,) * len(shape), pipeline_mode=pl.Buffered(1))


def _mla_in(x, ln, w, cos_t, sin_t):
    tokens = x.shape[0]
    rows = PROJ_ROWS
    row_spec = lambda width: pl.BlockSpec((rows, width), lambda i: (i, 0))
    consts = [ln, w["wq"], w["wkv"], w["wg"], w["wkr"], w["qnorm"], w["kvnorm"],
              w["wuqn"], w["wuqr"], w["wukn"], w["wuv"]]
    out_widths = [MLA_HEADS * MLA_NOPE, MLA_HEADS * MLA_ROPE, MLA_HEADS * MLA_NOPE,
                  None, 2 * LANES, MLA_WIDTH]
    per_blk = FLASH_BLOCK // rows
    vt_spec = pl.BlockSpec((1, MLA_WIDTH, rows), lambda i: (i // per_blk, 0, i % per_blk))
    vt_shape = jax.ShapeDtypeStruct((tokens // FLASH_BLOCK, MLA_WIDTH, FLASH_BLOCK), BF16)
    return pl.pallas_call(
        _mla_in_kernel,
        grid=(tokens // rows,),
        in_specs=[row_spec(D_MODEL)] + [_const_spec(c.shape) for c in consts]
                 + [row_spec(LANES), row_spec(LANES)],
        out_specs=[vt_spec if wd is None else row_spec(wd) for wd in out_widths],
        out_shape=[vt_shape if wd is None else jax.ShapeDtypeStruct((tokens, wd), BF16)
                   for wd in out_widths],
        compiler_params=pltpu.CompilerParams(
            dimension_semantics=("arbitrary",), vmem_limit_bytes=VMEM_LIMIT_BYTES),
        name="mla_in",
    )(x, *consts, cos_t, sin_t)


def _mla_flash_kernel(qn_ref, qr_ref, kn_ref, kr_ref, vt_ref, g_ref, o_ref,
                      kcat_sc, m_sc, l_sc, acc_sc, *, seq):
    blk = FLASH_BLOCK
    pair = (0, 1)
    hcols = [slice(hh * LANES, (hh + 1) * LANES) for hh in pair]
    for hh in pair:
        kcat_sc[hh, :, :LANES] = kn_ref[:, hcols[hh]]
        kcat_sc[hh, :, LANES:] = kr_ref[:, hcols[hh]]
    causal_t = (lax.broadcasted_iota(jnp.int32, (blk, blk), 0)
                <= lax.broadcasted_iota(jnp.int32, (blk, blk), 1))

    for qi in range(seq // blk):
        rows = slice(qi * blk, (qi + 1) * blk)
        qs = [jnp.concatenate([qn_ref[rows, hcols[hh]], qr_ref[rows, :]], axis=1) for hh in pair]
        m_sc[...] = jnp.full(m_sc.shape, NEG_INF, F32)
        l_sc[...] = jnp.zeros(l_sc.shape, F32)
        acc_sc[...] = jnp.zeros(acc_sc.shape, F32)

        def step(kb, masked, qs=qs):
            kstart = pl.multiple_of(kb * blk, blk)
            for hh in pair:
                st = _dot_nt(kcat_sc[hh, pl.ds(kstart, blk), :], qs[hh])
                if masked:
                    st = jnp.where(causal_t, st, NEG_INF)
                m_prev = m_sc[hh]
                m_new = jnp.maximum(m_prev, jnp.max(st, axis=0, keepdims=True))
                alpha = jnp.exp2(m_prev - m_new)
                p = jnp.exp2(st - m_new)
                l_sc[hh] = alpha * l_sc[hh] + jnp.sum(p, axis=0, keepdims=True)
                acc_sc[hh] = alpha * acc_sc[hh] + _dot(vt_ref[kb, hcols[hh], :], p.astype(BF16))
                m_sc[hh] = m_new

        if qi > 0:
            def body(j, carry):
                step(j, False)
                return carry
            lax.fori_loop(0, qi, body, 0)
        step(qi, True)

        for hh in pair:
            y = (acc_sc[hh] * (1.0 / l_sc[hh])).T
            o_ref[rows, hcols[hh]] = (y * g_ref[rows, hcols[hh]].astype(F32)).astype(BF16)


def _mla_flash(qn, qr, kn, krp, vt, g, batch, seq):
    tokens = batch * seq
    nblk = seq // FLASH_BLOCK
    pair_spec = pl.BlockSpec((seq, 2 * LANES), lambda b, p: (b, p))
    return pl.pallas_call(
        functools.partial(_mla_flash_kernel, seq=seq),
        grid=(batch, MLA_HEADS // 2),
        in_specs=[pair_spec,
                  pl.BlockSpec((seq, LANES), lambda b, p: (b, p)),
                  pair_spec,
                  pl.BlockSpec((seq, 2 * LANES), lambda b, p: (b, 0)),
                  pl.BlockSpec((nblk, 2 * LANES, FLASH_BLOCK), lambda b, p: (b, p, 0)),
                  pair_spec],
        out_specs=pair_spec,
        out_shape=jax.ShapeDtypeStruct((tokens, MLA_WIDTH), BF16),
        scratch_shapes=[pltpu.VMEM((2, seq, 2 * LANES), BF16),
                        pltpu.VMEM((2, 1, FLASH_BLOCK), F32),
                        pltpu.VMEM((2, 1, FLASH_BLOCK), F32),
                        pltpu.VMEM((2, MLA_V, FLASH_BLOCK), F32)],
        compiler_params=pltpu.CompilerParams(
            dimension_semantics=("arbitrary", "arbitrary"), vmem_limit_bytes=VMEM_LIMIT_BYTES),
        name="mla_flash",
    )(qn, qr, kn, krp, vt, g)


def _dup_heads(t):
    parts = []
    for hh in range(t.shape[1] // SWA_HEAD_DIM):
        piece = t[:, hh * SWA_HEAD_DIM:(hh + 1) * SWA_HEAD_DIM]
        parts += [piece, piece]
    return jnp.concatenate(parts, axis=1)


def _swa_in_kernel(x_ref, ln_ref, wq_ref, wk_ref, wv_ref, wg_ref, c_ref, s1_ref, s2_ref,
                   q_ref, kd_ref, vt_ref, g_ref):
    scale = SWA_HEAD_DIM ** -0.5 * math.log2(math.e)
    h = _rmsnorm(x_ref[...], ln_ref[...]).astype(BF16)
    c_t, s1_t, s2_t = c_ref[...], s1_ref[...], s2_ref[...]
    q = _rope_partial(_dot(h, wq_ref[...]), c_t, s1_t, s2_t)
    q_ref[...] = (q * scale).astype(BF16)
    k = _rope_partial(_dot(h, wk_ref[...]), c_t, s1_t, s2_t)
    kd_ref[...] = _dup_heads(k).astype(BF16)
    vt_ref[...] = _dot_nt(wv_ref[...], h).astype(BF16)
    gate = _dot(h, wg_ref[...])
    g_ref[...] = (gate * jax.nn.sigmoid(gate)).astype(BF16)


def _swa_in(x, ln, w, c_t, s1_t, s2_t):
    tokens = x.shape[0]
    rows = PROJ_ROWS
    row_spec = lambda width: pl.BlockSpec((rows, width), lambda i: (i, 0))
    consts = [ln, w["wq"], w["wk"], w["wv"], w["wg"]]
    out_widths = [SWA_WIDTH, SWA_KV_HEADS * LANES, None, SWA_WIDTH]
    vt_spec = pl.BlockSpec((SWA_KV_WIDTH, rows), lambda i: (0, i))
    vt_shape = jax.ShapeDtypeStruct((SWA_KV_WIDTH, tokens), BF16)
    return pl.pallas_call(
        _swa_in_kernel,
        grid=(tokens // rows,),
        in_specs=[row_spec(D_MODEL)] + [_const_spec(c.shape) for c in consts]
                 + [row_spec(LANES)] * 3,
        out_specs=[vt_spec if wd is None else row_spec(wd) for wd in out_widths],
        out_shape=[vt_shape if wd is None else jax.ShapeDtypeStruct((tokens, wd), BF16)
                   for wd in out_widths],
        compiler_params=pltpu.CompilerParams(
            dimension_semantics=("arbitrary",), vmem_limit_bytes=VMEM_LIMIT_BYTES),
        name="swa_in",
    )(x, *consts, c_t, s1_t, s2_t)


def _swa_attn_kernel(sink_ref, q_ref, kc_ref, kp_ref, vtc_ref, vtp_ref, g_ref, o_ref):
    has_prev = pl.program_id(1) > 0
    pairs = SWA_GROUP // 2
    key_i = lax.broadcasted_iota(jnp.int32, (BLOCK, pairs * BLOCK), 0)
    qry_i = lax.broadcasted_iota(jnp.int32, (BLOCK, pairs * BLOCK), 1) % BLOCK
    in_cur = key_i <= qry_i
    low = _lane_index((BLOCK, LANES)) < SWA_HEAD_DIM
    zero = jnp.zeros((BLOCK, LANES), BF16)

    for kvh in range(SWA_KV_HEADS):
        cols = slice(kvh * LANES, (kvh + 1) * LANES)
        kc, kp = kc_ref[:, cols], kp_ref[:, cols]
        k_halves = (jnp.concatenate([jnp.where(low, kp, zero), jnp.where(low, kc, zero)], axis=0),
                    jnp.concatenate([jnp.where(low, zero, kp), jnp.where(low, zero, kc)], axis=0))
        vrows = slice(kvh * SWA_HEAD_DIM, (kvh + 1) * SWA_HEAD_DIM)
        vt = jnp.concatenate([vtp_ref[vrows, :], vtc_ref[vrows, :]], axis=1)
        head0 = kvh * SWA_GROUP
        q_rows = jnp.concatenate(
            [q_ref[:, (head0 + 2 * pp) * SWA_HEAD_DIM:(head0 + 2 * pp + 2) * SWA_HEAD_DIM]
             for pp in range(pairs)], axis=0)
        outs = []
        for sub in range(2):
            st = _dot_nt(k_halves[sub], q_rows)
            s_prev = jnp.where(has_prev, st[:BLOCK], NEG_INF)
            s = jnp.where(in_cur, st[BLOCK:], s_prev)
            sink = sink_ref[sub * SWA_KV_HEADS + kvh:sub * SWA_KV_HEADS + kvh + 1, :]
            m = jnp.maximum(jnp.max(s, axis=0, keepdims=True), sink)
            e = jnp.exp2(s - m)
            denom = jnp.sum(e, axis=0, keepdims=True) + jnp.exp2(sink - m)
            p = e.astype(BF16)
            pt = jnp.concatenate([jnp.where(in_cur, jnp.zeros_like(p), p),
                                  jnp.where(in_cur, p, jnp.zeros_like(p))], axis=0)
            outs.append(_dot(vt, pt) * (1.0 / denom))
        for pp in range(pairs):
            qc = slice(pp * BLOCK, (pp + 1) * BLOCK)
            y = jnp.concatenate([outs[0][:, qc], outs[1][:, qc]], axis=0).T
            pcols = slice((head0 + 2 * pp) * SWA_HEAD_DIM, (head0 + 2 * pp + 2) * SWA_HEAD_DIM)
            o_ref[:, pcols] = (y * g_ref[:, pcols].astype(F32)).astype(BF16)


def _swa_sink_rows(sinks):
    s2 = (sinks * math.log2(math.e)).reshape(SWA_KV_HEADS, SWA_GROUP // 2, 2)
    rows = jnp.transpose(s2, (2, 0, 1)).reshape(2 * SWA_KV_HEADS, SWA_GROUP // 2, 1)
    return jnp.broadcast_to(rows, (2 * SWA_KV_HEADS, SWA_GROUP // 2, BLOCK)).reshape(
        2 * SWA_KV_HEADS, -1)


def _swa_attn(sinks, q, kd, vt, g, batch, seq):
    tokens = batch * seq
    nb = seq // BLOCK
    cur = lambda b, i: (b * nb + i, 0)
    prev = lambda b, i: (b * nb + jnp.maximum(i - 1, 0), 0)
    cur_t = lambda b, i: (0, b * nb + i)
    prev_t = lambda b, i: (0, b * nb + jnp.maximum(i - 1, 0))
    kv_width = SWA_KV_HEADS * LANES
    sink_rows = _swa_sink_rows(sinks)
    return pl.pallas_call(
        _swa_attn_kernel,
        grid=(batch, nb),
        in_specs=[pl.BlockSpec(sink_rows.shape, lambda b, i: (0, 0)),
                  pl.BlockSpec((BLOCK, SWA_WIDTH), cur),
                  pl.BlockSpec((BLOCK, kv_width), cur),
                  pl.BlockSpec((BLOCK, kv_width), prev),
                  pl.BlockSpec((SWA_KV_WIDTH, BLOCK), cur_t),
                  pl.BlockSpec((SWA_KV_WIDTH, BLOCK), prev_t),
                  pl.BlockSpec((BLOCK, SWA_WIDTH), cur)],
        out_specs=pl.BlockSpec((BLOCK, SWA_WIDTH), cur),
        out_shape=jax.ShapeDtypeStruct((tokens, SWA_WIDTH), BF16),
        compiler_params=pltpu.CompilerParams(
            dimension_semantics=("arbitrary", "arbitrary"), vmem_limit_bytes=VMEM_LIMIT_BYTES),
        name="swa_attn",
    )(sink_rows, q, kd, kd, vt, vt, g)


def _out_kernel(x_ref, y_ref, w_ref, fn_ref, o_ref, *, final):
    x_new = x_ref[...] + _dot(y_ref[...], w_ref[...])
    o_ref[...] = _rmsnorm(x_new, fn_ref[...]) if final else x_new


def _out_proj(x, y, w_out, final_norm, final):
    tokens = x.shape[0]
    rows = OUT_ROWS
    row_spec = lambda width: pl.BlockSpec((rows, width), lambda i: (i, 0))
    return pl.pallas_call(
        functools.partial(_out_kernel, final=final),
        grid=(tokens // rows,),
        in_specs=[row_spec(D_MODEL), row_spec(y.shape[1]), _const_spec(w_out.shape),
                  _const_spec(final_norm.shape)],
        out_specs=row_spec(D_MODEL),
        out_shape=jax.ShapeDtypeStruct((tokens, D_MODEL), F32),
        compiler_params=pltpu.CompilerParams(
            dimension_semantics=("arbitrary",), vmem_limit_bytes=VMEM_LIMIT_BYTES),
        name="out_proj",
    )(x, y, w_out, final_norm)


def _mla_weights(w_in, q_norm, w_uq, kv_norm, w_ukv):
    o1 = MLA_Q_RANK
    o2 = o1 + MLA_KV_RANK
    o3 = o2 + MLA_ROPE
    w_kr = w_in[:, o2:o3]
    uq = w_uq.reshape(MLA_Q_RANK, MLA_HEADS, MLA_NOPE + MLA_ROPE)
    ukv = w_ukv.reshape(MLA_KV_RANK, MLA_HEADS, MLA_NOPE + MLA_V)
    return {
        "wq": w_in[:, :o1].astype(BF16),
        "wkv": w_in[:, o1:o2].astype(BF16),
        "wkr": jnp.concatenate([w_kr, w_kr], axis=1).astype(BF16),
        "wg": w_in[:, o3:].astype(BF16),
        "qnorm": q_norm.reshape(1, -1),
        "kvnorm": kv_norm.reshape(1, -1),
        "wuqn": uq[:, :, :MLA_NOPE].reshape(MLA_Q_RANK, -1).astype(BF16),
        "wuqr": uq[:, :, MLA_NOPE:].reshape(MLA_Q_RANK, -1).astype(BF16),
        "wukn": ukv[:, :, :MLA_NOPE].reshape(MLA_KV_RANK, -1).astype(BF16),
        "wuv": ukv[:, :, MLA_NOPE:].reshape(MLA_KV_RANK, -1).T.astype(BF16),
    }


def _swa_weights(w_in):
    o1 = SWA_WIDTH
    o2 = o1 + SWA_KV_WIDTH
    o3 = o2 + SWA_KV_WIDTH
    return {
        "wq": w_in[:, :o1].astype(BF16),
        "wk": w_in[:, o1:o2].astype(BF16),
        "wv": w_in[:, o2:o3].T.astype(BF16),
        "wg": w_in[:, o3:].astype(BF16),
    }


def kernel(x, positions, layer_norm, mla_w_in, mla_q_norm, mla_w_uq, mla_kv_norm, mla_w_ukv,
           mla_w_out, swa_w_in, swa_sinks, swa_w_out, final_norm):
    batch, seq, d_model = x.shape
    tokens = batch * seq
    assert d_model == D_MODEL and seq % FLASH_BLOCK == 0 and tokens % OUT_ROWS == 0
    mcos, msin, sc, s1, s2 = _rope_tables(positions)
    xt = x.reshape(tokens, d_model)
    fn = final_norm.reshape(1, -1)
    for i in range(DEPTH):
        j = i // N_MIXERS
        ln = layer_norm[i].reshape(1, -1)
        final = i == DEPTH - 1
        if i % N_MIXERS == 0:
            w = _mla_weights(mla_w_in[j], mla_q_norm[j], mla_w_uq[j], mla_kv_norm[j], mla_w_ukv[j])
            qn, qr, kn, vt, krp, g = _mla_in(xt, ln, w, mcos, msin)
            y = _mla_flash(qn, qr, kn, krp, vt, g, batch, seq)
            w_out = mla_w_out[j]
        else:
            w = _swa_weights(swa_w_in[j])
            q, kd, vt, g = _swa_in(xt, ln, w, sc, s1, s2)
            y = _swa_attn(swa_sinks[j], q, kd, vt, g, batch, seq)
            w_out = swa_w_out[j]
        xt = _out_proj(xt, y, w_out.astype(BF16), fn, final)
    return xt.reshape(batch, seq, d_model)
```

```python
import functools
import math

import jax
import jax.numpy as jnp
from jax import lax
from jax.experimental import pallas as pl
from jax.experimental.pallas import tpu as pltpu

D_MODEL = 2048
DEPTH = 4
N_MIXERS = 2
ROPE_THETA = 500000.0
NORM_EPS = 1e-6
BLOCK = 128
NEG_INF = -1e30

MLA_HEADS = 16
MLA_Q_RANK = 512
MLA_KV_RANK = 512
MLA_NOPE = 128
MLA_ROPE = 64
MLA_V = 128
MLA_WIDTH = MLA_HEADS * MLA_V

SWA_Q_HEADS = 32
SWA_KV_HEADS = 4
SWA_GROUP = SWA_Q_HEADS // SWA_KV_HEADS
SWA_HEAD_DIM = 64
SWA_ROPE_DIM = SWA_HEAD_DIM // 4
SWA_WIDTH = SWA_Q_HEADS * SWA_HEAD_DIM
SWA_KV_WIDTH = SWA_KV_HEADS * SWA_HEAD_DIM

LANES = 128
VMEM_LIMIT_BYTES = 56 * 1024 * 1024

PROJ_ROWS = 256
OUT_ROWS = 512
FLASH_BLOCK = 512

BF16 = jnp.bfloat16
F32 = jnp.float32


def _dot(a, b):
    return jnp.dot(a, b, preferred_element_type=F32)


def _dot_nt(a, b):
    return lax.dot_general(a, b, (((1,), (1,)), ((), ())), preferred_element_type=F32)


def _rmsnorm(xf, g):
    y = xf * lax.rsqrt(jnp.mean(xf * xf, axis=-1, keepdims=True) + NORM_EPS)
    return y * g


def _tile_lanes(t, width):
    reps = width // t.shape[1]
    return t if reps == 1 else jnp.concatenate([t] * reps, axis=1)


def _lane_index(shape):
    return lax.broadcasted_iota(jnp.int32, shape, 1)


def _rope_half64(x, cos_t, sin_t):
    w = x.shape[1]
    fwd = pltpu.roll(x, w - MLA_ROPE // 2, 1)
    bwd = pltpu.roll(x, MLA_ROPE // 2, 1)
    first_half = (_lane_index(x.shape) % MLA_ROPE) < (MLA_ROPE // 2)
    swapped = jnp.where(first_half, fwd, bwd)
    return x * _tile_lanes(cos_t, w) + swapped * _tile_lanes(sin_t, w)


def _rope_partial(x, c_t, s1_t, s2_t):
    w = x.shape[1]
    half = SWA_ROPE_DIM // 2
    fwd = pltpu.roll(x, w - half, 1)
    bwd = pltpu.roll(x, half, 1)
    return (x * _tile_lanes(c_t, w) + fwd * _tile_lanes(s1_t, w)
            + bwd * _tile_lanes(s2_t, w))


def _rope_table_kernel(pos_ref, const_ref, mcos_ref, msin_ref, sc_ref, s1_ref, s2_ref):
    pos = pos_ref[...].astype(F32)
    ang_m = pos * const_ref[0:1, :]
    mcos_ref[...] = jnp.cos(ang_m)
    msin_ref[...] = jnp.sin(ang_m) * const_ref[1:2, :]
    ang_s = pos * const_ref[2:3, :]
    sin_s = jnp.sin(ang_s)
    sc_ref[...] = jnp.cos(ang_s)
    s1_ref[...] = sin_s * const_ref[3:4, :]
    s2_ref[...] = sin_s * const_ref[4:5, :]


def _rope_constants():
    lane = jnp.arange(LANES)
    f_mla = ROPE_THETA ** (-jnp.arange(0, MLA_ROPE, 2, dtype=F32) / MLA_ROPE)
    f_swa = ROPE_THETA ** (-jnp.arange(0, SWA_ROPE_DIM, 2, dtype=F32) / SWA_ROPE_DIM)
    half_m = MLA_ROPE // 2
    half_s = SWA_ROPE_DIM // 2
    d_m = lane % MLA_ROPE
    d_s = lane % SWA_HEAD_DIM
    rows = [
        f_mla[d_m % half_m],
        jnp.where(d_m < half_m, -1.0, 1.0),
        jnp.where(d_s < SWA_ROPE_DIM, f_swa[d_s % half_s], 0.0),
        jnp.where(d_s < half_s, -1.0, 0.0),
        jnp.where((d_s >= half_s) & (d_s < SWA_ROPE_DIM), 1.0, 0.0),
    ]
    rows += [jnp.zeros((LANES,), F32)] * 3
    return jnp.stack([r.astype(F32) for r in rows])


def _rope_tables(positions):
    tokens = positions.size
    rows = 1024
    tab = jax.ShapeDtypeStruct((tokens, LANES), F32)
    spec = pl.BlockSpec((rows, LANES), lambda i: (i, 0))
    return pl.pallas_call(
        _rope_table_kernel,
        grid=(tokens // rows,),
        in_specs=[pl.BlockSpec((rows, 1), lambda i: (i, 0)),
                  pl.BlockSpec((8, LANES), lambda i: (0, 0))],
        out_specs=[spec] * 5,
        out_shape=[tab] * 5,
        name="rope_tables",
    )(positions.reshape(tokens, 1), _rope_constants())


def _mla_in_kernel(x_ref, ln_ref, wq_ref, wkv_ref, wg_ref, wkr_ref, qnorm_ref, kvnorm_ref,
                   wuqn_ref, wuqr_ref, wukn_ref, wuv_ref, cos_ref, sin_ref,
                   qn_ref, qr_ref, kn_ref, vt_ref, krp_ref, g_ref):
    scale = (MLA_NOPE + MLA_ROPE) ** -0.5 * math.log2(math.e)
    h = _rmsnorm(x_ref[...], ln_ref[...]).astype(BF16)
    cos_t = cos_ref[...]
    sin_t = sin_ref[...]

    cq = _rmsnorm(_dot(h, wq_ref[...]), qnorm_ref[...]).astype(BF16)
    qn_ref[...] = (_dot(cq, wuqn_ref[...]) * scale).astype(BF16)
    qr = _rope_half64(_dot(cq, wuqr_ref[...]), cos_t, sin_t)
    qr_ref[...] = (qr * scale).astype(BF16)

    ckv = _rmsnorm(_dot(h, wkv_ref[...]), kvnorm_ref[...]).astype(BF16)
    kn_ref[...] = _dot(ckv, wukn_ref[...]).astype(BF16)
    vt_ref[0] = _dot_nt(wuv_ref[...], ckv).astype(BF16)

    kr = _rope_half64(_dot(h, wkr_ref[...]), cos_t, sin_t)
    low = _lane_index(kr.shape) < MLA_ROPE
    krp_ref[:, :LANES] = jnp.where(low, kr, 0.0).astype(BF16)
    krp_ref[:, LANES:] = jnp.where(low, 0.0, kr).astype(BF16)

    gate = _dot(h, wg_ref[...])
    g_ref[...] = (gate * jax.nn.sigmoid(gate)).astype(BF16)


def _const_spec(shape):
    return pl.BlockSpec(shape, lambda i: (0,) * len(shape), pipeline_mode=pl.Buffered(1))


def _mla_in(x, ln, w, cos_t, sin_t):
    tokens = x.shape[0]
    rows = PROJ_ROWS
    row_spec = lambda width: pl.BlockSpec((rows, width), lambda i: (i, 0))
    consts = [ln, w["wq"], w["wkv"], w["wg"], w["wkr"], w["qnorm"], w["kvnorm"],
              w["wuqn"], w["wuqr"], w["wukn"], w["wuv"]]
    out_widths = [MLA_HEADS * MLA_NOPE, MLA_HEADS * MLA_ROPE, MLA_HEADS * MLA_NOPE,
                  None, 2 * LANES, MLA_WIDTH]
    per_blk = FLASH_BLOCK // rows
    vt_spec = pl.BlockSpec((1, MLA_WIDTH, rows), lambda i: (i // per_blk, 0, i % per_blk))
    vt_shape = jax.ShapeDtypeStruct((tokens // FLASH_BLOCK, MLA_WIDTH, FLASH_BLOCK), BF16)
    return pl.pallas_call(
        _mla_in_kernel,
        grid=(tokens // rows,),
        in_specs=[row_spec(D_MODEL)] + [_const_spec(c.shape) for c in consts]
                 + [row_spec(LANES), row_spec(LANES)],
        out_specs=[vt_spec if wd is None else row_spec(wd) for wd in out_widths],
        out_shape=[vt_shape if wd is None else jax.ShapeDtypeStruct((tokens, wd), BF16)
                   for wd in out_widths],
        compiler_params=pltpu.CompilerParams(
            dimension_semantics=("arbitrary",), vmem_limit_bytes=VMEM_LIMIT_BYTES),
        name="mla_in",
    )(x, *consts, cos_t, sin_t)


def _mla_flash_kernel(qn_ref, qr_ref, kn_ref, kr_ref, vt_ref, g_ref, o_ref,
                      kcat_sc, s_sc, m_sc, l_sc, acc_sc, *, seq):
    blk = FLASH_BLOCK
    pair = (0, 1)
    hcols = [slice(hh * LANES, (hh + 1) * LANES) for hh in pair]
    for hh in pair:
        kcat_sc[hh, :, :LANES] = kn_ref[:, hcols[hh]]
        kcat_sc[hh, :, LANES:] = kr_ref[:, hcols[hh]]
    causal_t = (lax.broadcasted_iota(jnp.int32, (blk, blk), 0)
                <= lax.broadcasted_iota(jnp.int32, (blk, blk), 1))

    for qi in range(seq // blk):
        rows = slice(qi * blk, (qi + 1) * blk)
        qs = [jnp.concatenate([qn_ref[rows, hcols[hh]], qr_ref[rows, :]], axis=1) for hh in pair]
        m_sc[...] = jnp.full(m_sc.shape, NEG_INF, F32)
        l_sc[...] = jnp.zeros(l_sc.shape, F32)
        acc_sc[...] = jnp.zeros(acc_sc.shape, F32)

        def scores(hh, kb, slot, masked, qs=qs):
            kstart = pl.multiple_of(kb * blk, blk)
            st = _dot_nt(kcat_sc[hh, pl.ds(kstart, blk), :], qs[hh])
            s_sc[hh, slot] = jnp.where(causal_t, st, NEG_INF) if masked else st

        def accumulate(hh, kb, slot):
            st = s_sc[hh, slot]
            m_prev = m_sc[hh]
            m_new = jnp.maximum(m_prev, jnp.max(st, axis=0, keepdims=True))
            alpha = jnp.exp2(m_prev - m_new)
            p = jnp.exp2(st - m_new)
            l_sc[hh] = alpha * l_sc[hh] + jnp.sum(p, axis=0, keepdims=True)
            acc_sc[hh] = alpha * acc_sc[hh] + _dot(vt_ref[kb, hcols[hh], :], p.astype(BF16))
            m_sc[hh] = m_new

        visits = [qi] + list(range(qi))
        for hh in pair:
            scores(hh, visits[0], 0, True)
        for i, kb in enumerate(visits):
            for hh in pair:
                if i + 1 < len(visits):
                    scores(hh, visits[i + 1], (i + 1) % 2, False)
                accumulate(hh, kb, i % 2)

        for hh in pair:
            y = (acc_sc[hh] * (1.0 / l_sc[hh])).T
            o_ref[rows, hcols[hh]] = (y * g_ref[rows, hcols[hh]].astype(F32)).astype(BF16)


def _mla_flash(qn, qr, kn, krp, vt, g, batch, seq):
    tokens = batch * seq
    nblk = seq // FLASH_BLOCK
    pair_spec = pl.BlockSpec((seq, 2 * LANES), lambda b, p: (b, p))
    return pl.pallas_call(
        functools.partial(_mla_flash_kernel, seq=seq),
        grid=(batch, MLA_HEADS // 2),
        in_specs=[pair_spec,
                  pl.BlockSpec((seq, LANES), lambda b, p: (b, p)),
                  pair_spec,
                  pl.BlockSpec((seq, 2 * LANES), lambda b, p: (b, 0)),
                  pl.BlockSpec((nblk, 2 * LANES, FLASH_BLOCK), lambda b, p: (b, p, 0)),
                  pair_spec],
        out_specs=pair_spec,
        out_shape=jax.ShapeDtypeStruct((tokens, MLA_WIDTH), BF16),
        scratch_shapes=[pltpu.VMEM((2, seq, 2 * LANES), BF16),
                        pltpu.VMEM((2, 2, FLASH_BLOCK, FLASH_BLOCK), F32),
                        pltpu.VMEM((2, 1, FLASH_BLOCK), F32),
                        pltpu.VMEM((2, 1, FLASH_BLOCK), F32),
                        pltpu.VMEM((2, MLA_V, FLASH_BLOCK), F32)],
        compiler_params=pltpu.CompilerParams(
            dimension_semantics=("arbitrary", "arbitrary"), vmem_limit_bytes=VMEM_LIMIT_BYTES),
        name="mla_flash",
    )(qn, qr, kn, krp, vt, g)


def _dup_heads(t):
    parts = []
    for hh in range(t.shape[1] // SWA_HEAD_DIM):
        piece = t[:, hh * SWA_HEAD_DIM:(hh + 1) * SWA_HEAD_DIM]
        parts += [piece, piece]
    return jnp.concatenate(parts, axis=1)


def _swa_in_kernel(x_ref, ln_ref, wq_ref, wk_ref, wv_ref, wg_ref, c_ref, s1_ref, s2_ref,
                   q_ref, kd_ref, vt_ref, g_ref):
    scale = SWA_HEAD_DIM ** -0.5 * math.log2(math.e)
    h = _rmsnorm(x_ref[...], ln_ref[...]).astype(BF16)
    c_t, s1_t, s2_t = c_ref[...], s1_ref[...], s2_ref[...]
    q = _rope_partial(_dot(h, wq_ref[...]), c_t, s1_t, s2_t)
    q_ref[...] = (q * scale).astype(BF16)
    k = _rope_partial(_dot(h, wk_ref[...]), c_t, s1_t, s2_t)
    kd_ref[...] = _dup_heads(k).astype(BF16)
    vt_ref[...] = _dot_nt(wv_ref[...], h).astype(BF16)
    gate = _dot(h, wg_ref[...])
    g_ref[...] = (gate * jax.nn.sigmoid(gate)).astype(BF16)


def _swa_in(x, ln, w, c_t, s1_t, s2_t):
    tokens = x.shape[0]
    rows = PROJ_ROWS
    row_spec = lambda width: pl.BlockSpec((rows, width), lambda i: (i, 0))
    consts = [ln, w["wq"], w["wk"], w["wv"], w["wg"]]
    out_widths = [SWA_WIDTH, SWA_KV_HEADS * LANES, None, SWA_WIDTH]
    vt_spec = pl.BlockSpec((SWA_KV_WIDTH, rows), lambda i: (0, i))
    vt_shape = jax.ShapeDtypeStruct((SWA_KV_WIDTH, tokens), BF16)
    return pl.pallas_call(
        _swa_in_kernel,
        grid=(tokens // rows,),
        in_specs=[row_spec(D_MODEL)] + [_const_spec(c.shape) for c in consts]
                 + [row_spec(LANES)] * 3,
        out_specs=[vt_spec if wd is None else row_spec(wd) for wd in out_widths],
        out_shape=[vt_shape if wd is None else jax.ShapeDtypeStruct((tokens, wd), BF16)
                   for wd in out_widths],
        compiler_params=pltpu.CompilerParams(
            dimension_semantics=("arbitrary",), vmem_limit_bytes=VMEM_LIMIT_BYTES),
        name="swa_in",
    )(x, *consts, c_t, s1_t, s2_t)


def _swa_attn_kernel(sink_ref, q_ref, kc_ref, kp_ref, vtc_ref, vtp_ref, g_ref, o_ref):
    has_prev = pl.program_id(1) > 0
    pairs = SWA_GROUP // 2
    key_i = lax.broadcasted_iota(jnp.int32, (BLOCK, pairs * BLOCK), 0)
    qry_i = lax.broadcasted_iota(jnp.int32, (BLOCK, pairs * BLOCK), 1) % BLOCK
    in_cur = key_i <= qry_i
    low = _lane_index((BLOCK, LANES)) < SWA_HEAD_DIM
    zero = jnp.zeros((BLOCK, LANES), BF16)

    for kvh in range(SWA_KV_HEADS):
        cols = slice(kvh * LANES, (kvh + 1) * LANES)
        kc, kp = kc_ref[:, cols], kp_ref[:, cols]
        k_halves = (jnp.concatenate([jnp.where(low, kp, zero), jnp.where(low, kc, zero)], axis=0),
                    jnp.concatenate([jnp.where(low, zero, kp), jnp.where(low, zero, kc)], axis=0))
        vrows = slice(kvh * SWA_HEAD_DIM, (kvh + 1) * SWA_HEAD_DIM)
        vt = jnp.concatenate([vtp_ref[vrows, :], vtc_ref[vrows, :]], axis=1)
        head0 = kvh * SWA_GROUP
        q_rows = jnp.concatenate(
            [q_ref[:, (head0 + 2 * pp) * SWA_HEAD_DIM:(head0 + 2 * pp + 2) * SWA_HEAD_DIM]
             for pp in range(pairs)], axis=0)
        outs = []
        for sub in range(2):
            st = _dot_nt(k_halves[sub], q_rows)
            s_prev = jnp.where(has_prev, st[:BLOCK], NEG_INF)
            s = jnp.where(in_cur, st[BLOCK:], s_prev)
            sink = sink_ref[sub * SWA_KV_HEADS + kvh:sub * SWA_KV_HEADS + kvh + 1, :]
            m = jnp.maximum(jnp.max(s, axis=0, keepdims=True), sink)
            e = jnp.exp2(s - m)
            denom = jnp.sum(e, axis=0, keepdims=True) + jnp.exp2(sink - m)
            p = e.astype(BF16)
            pt = jnp.concatenate([jnp.where(in_cur, jnp.zeros_like(p), p),
                                  jnp.where(in_cur, p, jnp.zeros_like(p))], axis=0)
            outs.append(_dot(vt, pt) * (1.0 / denom))
        for pp in range(pairs):
            qc = slice(pp * BLOCK, (pp + 1) * BLOCK)
            y = jnp.concatenate([outs[0][:, qc], outs[1][:, qc]], axis=0).T
            pcols = slice((head0 + 2 * pp) * SWA_HEAD_DIM, (head0 + 2 * pp + 2) * SWA_HEAD_DIM)
            o_ref[:, pcols] = (y * g_ref[:, pcols].astype(F32)).astype(BF16)


def _swa_sink_rows(sinks):
    s2 = (sinks * math.log2(math.e)).reshape(SWA_KV_HEADS, SWA_GROUP // 2, 2)
    rows = jnp.transpose(s2, (2, 0, 1)).reshape(2 * SWA_KV_HEADS, SWA_GROUP // 2, 1)
    return jnp.broadcast_to(rows, (2 * SWA_KV_HEADS, SWA_GROUP // 2, BLOCK)).reshape(
        2 * SWA_KV_HEADS, -1)


def _swa_attn(sinks, q, kd, vt, g, batch, seq):
    tokens = batch * seq
    nb = seq // BLOCK
    cur = lambda b, i: (b * nb + i, 0)
    prev = lambda b, i: (b * nb + jnp.maximum(i - 1, 0), 0)
    cur_t = lambda b, i: (0, b * nb + i)
    prev_t = lambda b, i: (0, b * nb + jnp.maximum(i - 1, 0))
    kv_width = SWA_KV_HEADS * LANES
    sink_rows = _swa_sink_rows(sinks)
    return pl.pallas_call(
        _swa_attn_kernel,
        grid=(batch, nb),
        in_specs=[pl.BlockSpec(sink_rows.shape, lambda b, i: (0, 0)),
                  pl.BlockSpec((BLOCK, SWA_WIDTH), cur),
                  pl.BlockSpec((BLOCK, kv_width), cur),
                  pl.BlockSpec((BLOCK, kv_width), prev),
                  pl.BlockSpec((SWA_KV_WIDTH, BLOCK), cur_t),
                  pl.BlockSpec((SWA_KV_WIDTH, BLOCK), prev_t),
                  pl.BlockSpec((BLOCK, SWA_WIDTH), cur)],
        out_specs=pl.BlockSpec((BLOCK, SWA_WIDTH), cur),
        out_shape=jax.ShapeDtypeStruct((tokens, SWA_WIDTH), BF16),
        compiler_params=pltpu.CompilerParams(
            dimension_semantics=("arbitrary", "arbitrary"), vmem_limit_bytes=VMEM_LIMIT_BYTES),
        name="swa_attn",
    )(sink_rows, q, kd, kd, vt, vt, g)


def _out_kernel(x_ref, y_ref, w_ref, fn_ref, o_ref, *, final):
    x_new = x_ref[...] + _dot(y_ref[...], w_ref[...])
    o_ref[...] = _rmsnorm(x_new, fn_ref[...]) if final else x_new


def _out_proj(x, y, w_out, final_norm, final):
    tokens = x.shape[0]
    rows = OUT_ROWS
    row_spec = lambda width: pl.BlockSpec((rows, width), lambda i: (i, 0))
    return pl.pallas_call(
        functools.partial(_out_kernel, final=final),
        grid=(tokens // rows,),
        in_specs=[row_spec(D_MODEL), row_spec(y.shape[1]), _const_spec(w_out.shape),
                  _const_spec(final_norm.shape)],
        out_specs=row_spec(D_MODEL),
        out_shape=jax.ShapeDtypeStruct((tokens, D_MODEL), F32),
        compiler_params=pltpu.CompilerParams(
            dimension_semantics=("arbitrary",), vmem_limit_bytes=VMEM_LIMIT_BYTES),
        name="out_proj",
    )(x, y, w_out, final_norm)


def _mla_weights(w_in, q_norm, w_uq, kv_norm, w_ukv):
    o1 = MLA_Q_RANK
    o2 = o1 + MLA_KV_RANK
    o3 = o2 + MLA_ROPE
    w_kr = w_in[:, o2:o3]
    uq = w_uq.reshape(MLA_Q_RANK, MLA_HEADS, MLA_NOPE + MLA_ROPE)
    ukv = w_ukv.reshape(MLA_KV_RANK, MLA_HEADS, MLA_NOPE + MLA_V)
    return {
        "wq": w_in[:, :o1].astype(BF16),
        "wkv": w_in[:, o1:o2].astype(BF16),
        "wkr": jnp.concatenate([w_kr, w_kr], axis=1).astype(BF16),
        "wg": w_in[:, o3:].astype(BF16),
        "qnorm": q_norm.reshape(1, -1),
        "kvnorm": kv_norm.reshape(1, -1),
        "wuqn": uq[:, :, :MLA_NOPE].reshape(MLA_Q_RANK, -1).astype(BF16),
        "wuqr": uq[:, :, MLA_NOPE:].reshape(MLA_Q_RANK, -1).astype(BF16),
        "wukn": ukv[:, :, :MLA_NOPE].reshape(MLA_KV_RANK, -1).astype(BF16),
        "wuv": ukv[:, :, MLA_NOPE:].reshape(MLA_KV_RANK, -1).T.astype(BF16),
    }


def _swa_weights(w_in):
    o1 = SWA_WIDTH
    o2 = o1 + SWA_KV_WIDTH
    o3 = o2 + SWA_KV_WIDTH
    return {
        "wq": w_in[:, :o1].astype(BF16),
        "wk": w_in[:, o1:o2].astype(BF16),
        "wv": w_in[:, o2:o3].T.astype(BF16),
        "wg": w_in[:, o3:].astype(BF16),
    }


def kernel(x, positions, layer_norm, mla_w_in, mla_q_norm, mla_w_uq, mla_kv_norm, mla_w_ukv,
           mla_w_out, swa_w_in, swa_sinks, swa_w_out, final_norm):
    batch, seq, d_model = x.shape
    tokens = batch * seq
    assert d_model == D_MODEL and seq % FLASH_BLOCK == 0 and tokens % OUT_ROWS == 0
    assert FLASH_BLOCK % PROJ_ROWS == 0
    mcos, msin, sc, s1, s2 = _rope_tables(positions)
    xt = x.reshape(tokens, d_model)
    fn = final_norm.reshape(1, -1)
    for i in range(DEPTH):
        j = i // N_MIXERS
        ln = layer_norm[i].reshape(1, -1)
        final = i == DEPTH - 1
        if i % N_MIXERS == 0:
            w = _mla_weights(mla_w_in[j], mla_q_norm[j], mla_w_uq[j], mla_kv_norm[j], mla_w_ukv[j])
            qn, qr, kn, vt, krp, g = _mla_in(xt, ln, w, mcos, msin)
            y = _mla_flash(qn, qr, kn, krp, vt, g, batch, seq)
            w_out = mla_w_out[j]
        else:
            w = _swa_weights(swa_w_in[j])
            q, kd, vt, g = _swa_in(xt, ln, w, sc, s1, s2)
            y = _swa_attn(swa_sinks[j], q, kd, vt, g, batch, seq)
            w_out = swa_w_out[j]
        xt = _out_proj(xt, y, w_out.astype(BF16), fn, final)
    return xt.reshape(batch, seq, d_model)
```

```python
import functools
import math

import jax
import jax.numpy as jnp
from jax import lax
from jax.experimental import pallas as pl
from jax.experimental.pallas import tpu as pltpu

D_MODEL = 2048
DEPTH = 4
N_MIXERS = 2
ROPE_THETA = 500000.0
NORM_EPS = 1e-6
BLOCK = 128
NEG_INF = -1e30

MLA_HEADS = 16
MLA_Q_RANK = 512
MLA_KV_RANK = 512
MLA_NOPE = 128
MLA_ROPE = 64
MLA_V = 128
MLA_WIDTH = MLA_HEADS * MLA_V

SWA_Q_HEADS = 32
SWA_KV_HEADS = 4
SWA_GROUP = SWA_Q_HEADS // SWA_KV_HEADS
SWA_HEAD_DIM = 64
SWA_ROPE_DIM = SWA_HEAD_DIM // 4
SWA_WIDTH = SWA_Q_HEADS * SWA_HEAD_DIM
SWA_KV_WIDTH = SWA_KV_HEADS * SWA_HEAD_DIM

LANES = 128
VMEM_LIMIT_BYTES = 56 * 1024 * 1024

PROJ_ROWS = 256
OUT_ROWS = 512
FLASH_BLOCK = 512

BF16 = jnp.bfloat16
F32 = jnp.float32


def _dot(a, b):
    return jnp.dot(a, b, preferred_element_type=F32)


def _dot_nt(a, b):
    return lax.dot_general(a, b, (((1,), (1,)), ((), ())), preferred_element_type=F32)


def _rmsnorm(xf, g):
    y = xf * lax.rsqrt(jnp.mean(xf * xf, axis=-1, keepdims=True) + NORM_EPS)
    return y * g


def _tile_lanes(t, width):
    reps = width // t.shape[1]
    return t if reps == 1 else jnp.concatenate([t] * reps, axis=1)


def _lane_index(shape):
    return lax.broadcasted_iota(jnp.int32, shape, 1)


def _rope_half64(x, cos_t, sin_t):
    w = x.shape[1]
    fwd = pltpu.roll(x, w - MLA_ROPE // 2, 1)
    bwd = pltpu.roll(x, MLA_ROPE // 2, 1)
    first_half = (_lane_index(x.shape) % MLA_ROPE) < (MLA_ROPE // 2)
    swapped = jnp.where(first_half, fwd, bwd)
    return x * _tile_lanes(cos_t, w) + swapped * _tile_lanes(sin_t, w)


def _rope_partial(x, c_t, s1_t, s2_t):
    w = x.shape[1]
    half = SWA_ROPE_DIM // 2
    fwd = pltpu.roll(x, w - half, 1)
    bwd = pltpu.roll(x, half, 1)
    return (x * _tile_lanes(c_t, w) + fwd * _tile_lanes(s1_t, w)
            + bwd * _tile_lanes(s2_t, w))


def _rope_table_kernel(pos_ref, const_ref, mcos_ref, msin_ref, sc_ref, s1_ref, s2_ref):
    pos = pos_ref[...].astype(F32)
    ang_m = pos * const_ref[0:1, :]
    mcos_ref[...] = jnp.cos(ang_m)
    msin_ref[...] = jnp.sin(ang_m) * const_ref[1:2, :]
    ang_s = pos * const_ref[2:3, :]
    sin_s = jnp.sin(ang_s)
    sc_ref[...] = jnp.cos(ang_s)
    s1_ref[...] = sin_s * const_ref[3:4, :]
    s2_ref[...] = sin_s * const_ref[4:5, :]


def _rope_constants():
    lane = jnp.arange(LANES)
    f_mla = ROPE_THETA ** (-jnp.arange(0, MLA_ROPE, 2, dtype=F32) / MLA_ROPE)
    f_swa = ROPE_THETA ** (-jnp.arange(0, SWA_ROPE_DIM, 2, dtype=F32) / SWA_ROPE_DIM)
    half_m = MLA_ROPE // 2
    half_s = SWA_ROPE_DIM // 2
    d_m = lane % MLA_ROPE
    d_s = lane % SWA_HEAD_DIM
    rows = [
        f_mla[d_m % half_m],
        jnp.where(d_m < half_m, -1.0, 1.0),
        jnp.where(d_s < SWA_ROPE_DIM, f_swa[d_s % half_s], 0.0),
        jnp.where(d_s < half_s, -1.0, 0.0),
        jnp.where((d_s >= half_s) & (d_s < SWA_ROPE_DIM), 1.0, 0.0),
    ]
    rows += [jnp.zeros((LANES,), F32)] * 3
    return jnp.stack([r.astype(F32) for r in rows])


def _rope_tables(positions):
    tokens = positions.size
    rows = 1024
    tab = jax.ShapeDtypeStruct((tokens, LANES), F32)
    spec = pl.BlockSpec((rows, LANES), lambda i: (i, 0))
    return pl.pallas_call(
        _rope_table_kernel,
        grid=(tokens // rows,),
        in_specs=[pl.BlockSpec((rows, 1), lambda i: (i, 0)),
                  pl.BlockSpec((8, LANES), lambda i: (0, 0))],
        out_specs=[spec] * 5,
        out_shape=[tab] * 5,
        name="rope_tables",
    )(positions.reshape(tokens, 1), _rope_constants())


def _mla_in_kernel(x_ref, ln_ref, wq_ref, wkv_ref, wg_ref, wkr_ref, qnorm_ref, kvnorm_ref,
                   wuqn_ref, wuqr_ref, wukn_ref, wuv_ref, cos_ref, sin_ref,
                   qn_ref, qr_ref, kn_ref, vt_ref, krp_ref, g_ref):
    scale = (MLA_NOPE + MLA_ROPE) ** -0.5 * math.log2(math.e)
    h = _rmsnorm(x_ref[...], ln_ref[...]).astype(BF16)
    cos_t = cos_ref[...]
    sin_t = sin_ref[...]

    cq = _rmsnorm(_dot(h, wq_ref[...]), qnorm_ref[...]).astype(BF16)
    _store_groups(qn_ref, (_dot(cq, wuqn_ref[...]) * scale).astype(BF16))
    qr = _rope_half64(_dot(cq, wuqr_ref[...]), cos_t, sin_t)
    _store_groups(qr_ref, (qr * scale).astype(BF16))

    ckv = _rmsnorm(_dot(h, wkv_ref[...]), kvnorm_ref[...]).astype(BF16)
    _store_groups(kn_ref, _dot(ckv, wukn_ref[...]).astype(BF16))
    vt_ref[0] = _dot_nt(wuv_ref[...], ckv).astype(BF16)

    kr = _rope_half64(_dot(h, wkr_ref[...]), cos_t, sin_t)
    low = _lane_index(kr.shape) < MLA_ROPE
    krp_ref[:, :LANES] = jnp.where(low, kr, 0.0).astype(BF16)
    krp_ref[:, LANES:] = jnp.where(low, 0.0, kr).astype(BF16)

    gate = _dot(h, wg_ref[...])
    _store_groups(g_ref, (gate * jax.nn.sigmoid(gate)).astype(BF16))


def _store_groups(ref, val):
    groups, _, width = ref.shape
    for gg in range(groups):
        ref[gg] = val[:, gg * width:(gg + 1) * width]


def _load_groups(ref):
    return jnp.concatenate([ref[gg] for gg in range(ref.shape[0])], axis=1)


def _const_spec(shape):
    return pl.BlockSpec(shape, lambda i: (0,) * len(shape), pipeline_mode=pl.Buffered(1))


def _mla_in(x, ln, w, cos_t, sin_t):
    tokens = x.shape[0]
    rows = PROJ_ROWS
    row_spec = lambda width: pl.BlockSpec((rows, width), lambda i: (i, 0))
    consts = [ln, w["wq"], w["wkv"], w["wg"], w["wkr"], w["qnorm"], w["kvnorm"],
              w["wuqn"], w["wuqr"], w["wukn"], w["wuv"]]
    pairs = MLA_HEADS // 2
    per_blk = FLASH_BLOCK // rows

    def pair_major(width):
        return (pl.BlockSpec((pairs, rows, width), lambda i: (0, i, 0)),
                jax.ShapeDtypeStruct((pairs, tokens, width), BF16))

    outs = [pair_major(2 * MLA_NOPE),
            pair_major(2 * MLA_ROPE),
            pair_major(2 * MLA_NOPE),
            (pl.BlockSpec((1, MLA_WIDTH, rows), lambda i: (i // per_blk, 0, i % per_blk)),
             jax.ShapeDtypeStruct((tokens // FLASH_BLOCK, MLA_WIDTH, FLASH_BLOCK), BF16)),
            (row_spec(2 * LANES), jax.ShapeDtypeStruct((tokens, 2 * LANES), BF16)),
            pair_major(2 * MLA_V)]
    return pl.pallas_call(
        _mla_in_kernel,
        grid=(tokens // rows,),
        in_specs=[row_spec(D_MODEL)] + [_const_spec(c.shape) for c in consts]
                 + [row_spec(LANES), row_spec(LANES)],
        out_specs=[o[0] for o in outs],
        out_shape=[o[1] for o in outs],
        compiler_params=pltpu.CompilerParams(
            dimension_semantics=("arbitrary",), vmem_limit_bytes=VMEM_LIMIT_BYTES),
        name="mla_in",
    )(x, *consts, cos_t, sin_t)


def _mla_flash_kernel(qn_ref, qr_ref, kn_ref, kr_ref, vt_ref, g_ref, o_ref,
                      kcat_sc, s_sc, m_sc, l_sc, acc_sc, *, seq):
    blk = FLASH_BLOCK
    pair = (0, 1)
    hcols = [slice(hh * LANES, (hh + 1) * LANES) for hh in pair]
    for hh in pair:
        kcat_sc[hh, :, :LANES] = kn_ref[:, hcols[hh]]
        kcat_sc[hh, :, LANES:] = kr_ref[:, hcols[hh]]
    causal_t = (lax.broadcasted_iota(jnp.int32, (blk, blk), 0)
                <= lax.broadcasted_iota(jnp.int32, (blk, blk), 1))

    for qi in range(seq // blk):
        rows = slice(qi * blk, (qi + 1) * blk)
        qs = [jnp.concatenate([qn_ref[rows, hcols[hh]], qr_ref[rows, :]], axis=1) for hh in pair]
        m_sc[...] = jnp.full(m_sc.shape, NEG_INF, F32)
        l_sc[...] = jnp.zeros(l_sc.shape, F32)
        acc_sc[...] = jnp.zeros(acc_sc.shape, F32)

        def scores(hh, kb, slot, masked, qs=qs):
            kstart = pl.multiple_of(kb * blk, blk)
            st = _dot_nt(kcat_sc[hh, pl.ds(kstart, blk), :], qs[hh])
            s_sc[hh, slot] = jnp.where(causal_t, st, NEG_INF) if masked else st

        def accumulate(hh, kb, slot):
            st = s_sc[hh, slot]
            m_prev = m_sc[hh]
            m_new = jnp.maximum(m_prev, jnp.max(st, axis=0, keepdims=True))
            alpha = jnp.exp2(m_prev - m_new)
            p = jnp.exp2(st - m_new)
            l_sc[hh] = alpha * l_sc[hh] + jnp.sum(p, axis=0, keepdims=True)
            acc_sc[hh] = alpha * acc_sc[hh] + _dot(vt_ref[kb, hcols[hh], :], p.astype(BF16))
            m_sc[hh] = m_new

        visits = [qi] + list(range(qi))
        for hh in pair:
            scores(hh, visits[0], 0, True)
        for i, kb in enumerate(visits):
            for hh in pair:
                if i + 1 < len(visits):
                    scores(hh, visits[i + 1], (i + 1) % 2, False)
                accumulate(hh, kb, i % 2)

        for hh in pair:
            y = (acc_sc[hh] * (1.0 / l_sc[hh])).T
            o_ref[rows, hcols[hh]] = (y * g_ref[rows, hcols[hh]].astype(F32)).astype(BF16)


def _mla_flash(qn, qr, kn, krp, vt, g, batch, seq):
    tokens = batch * seq
    nblk = seq // FLASH_BLOCK
    pair_spec = pl.BlockSpec((None, seq, 2 * LANES), lambda b, p: (p, b, 0))
    return pl.pallas_call(
        functools.partial(_mla_flash_kernel, seq=seq),
        grid=(batch, MLA_HEADS // 2),
        in_specs=[pair_spec,
                  pl.BlockSpec((None, seq, LANES), lambda b, p: (p, b, 0)),
                  pair_spec,
                  pl.BlockSpec((seq, 2 * LANES), lambda b, p: (b, 0)),
                  pl.BlockSpec((nblk, 2 * LANES, FLASH_BLOCK), lambda b, p: (b, p, 0)),
                  pair_spec],
        out_specs=pair_spec,
        out_shape=jax.ShapeDtypeStruct((MLA_HEADS // 2, tokens, 2 * MLA_V), BF16),
        scratch_shapes=[pltpu.VMEM((2, seq, 2 * LANES), BF16),
                        pltpu.VMEM((2, 2, FLASH_BLOCK, FLASH_BLOCK), F32),
                        pltpu.VMEM((2, 1, FLASH_BLOCK), F32),
                        pltpu.VMEM((2, 1, FLASH_BLOCK), F32),
                        pltpu.VMEM((2, MLA_V, FLASH_BLOCK), F32)],
        compiler_params=pltpu.CompilerParams(
            dimension_semantics=("arbitrary", "arbitrary"), vmem_limit_bytes=VMEM_LIMIT_BYTES),
        name="mla_flash",
    )(qn, qr, kn, krp, vt, g)


def _dup_heads(t):
    parts = []
    for hh in range(t.shape[1] // SWA_HEAD_DIM):
        piece = t[:, hh * SWA_HEAD_DIM:(hh + 1) * SWA_HEAD_DIM]
        parts += [piece, piece]
    return jnp.concatenate(parts, axis=1)


def _swa_in_kernel(x_ref, ln_ref, wq_ref, wk_ref, wv_ref, wg_ref, c_ref, s1_ref, s2_ref,
                   q_ref, kd_ref, vt_ref, g_ref):
    scale = SWA_HEAD_DIM ** -0.5 * math.log2(math.e)
    h = _rmsnorm(x_ref[...], ln_ref[...]).astype(BF16)
    c_t, s1_t, s2_t = c_ref[...], s1_ref[...], s2_ref[...]
    q = _rope_partial(_dot(h, wq_ref[...]), c_t, s1_t, s2_t)
    q_ref[...] = (q * scale).astype(BF16)
    k = _rope_partial(_dot(h, wk_ref[...]), c_t, s1_t, s2_t)
    kd_ref[...] = _dup_heads(k).astype(BF16)
    vt = _dot_nt(wv_ref[...], h).astype(BF16)
    for cc in range(vt_ref.shape[0]):
        vt_ref[cc] = vt[:, cc * BLOCK:(cc + 1) * BLOCK]
    gate = _dot(h, wg_ref[...])
    g_ref[...] = (gate * jax.nn.sigmoid(gate)).astype(BF16)


def _swa_in(x, ln, w, c_t, s1_t, s2_t):
    tokens = x.shape[0]
    rows = PROJ_ROWS
    row_spec = lambda width: pl.BlockSpec((rows, width), lambda i: (i, 0))
    consts = [ln, w["wq"], w["wk"], w["wv"], w["wg"]]
    out_widths = [SWA_WIDTH, SWA_KV_HEADS * LANES, None, SWA_WIDTH]
    vt_spec = pl.BlockSpec((rows // BLOCK, SWA_KV_WIDTH, BLOCK), lambda i: (i, 0, 0))
    vt_shape = jax.ShapeDtypeStruct((tokens // BLOCK, SWA_KV_WIDTH, BLOCK), BF16)
    return pl.pallas_call(
        _swa_in_kernel,
        grid=(tokens // rows,),
        in_specs=[row_spec(D_MODEL)] + [_const_spec(c.shape) for c in consts]
                 + [row_spec(LANES)] * 3,
        out_specs=[vt_spec if wd is None else row_spec(wd) for wd in out_widths],
        out_shape=[vt_shape if wd is None else jax.ShapeDtypeStruct((tokens, wd), BF16)
                   for wd in out_widths],
        compiler_params=pltpu.CompilerParams(
            dimension_semantics=("arbitrary",), vmem_limit_bytes=VMEM_LIMIT_BYTES),
        name="swa_in",
    )(x, *consts, c_t, s1_t, s2_t)


def _swa_attn_kernel(sink_ref, q_ref, kc_ref, kp_ref, vtc_ref, vtp_ref, g_ref, o_ref):
    has_prev = pl.program_id(1) > 0
    pairs = SWA_GROUP // 2
    key_i = lax.broadcasted_iota(jnp.int32, (BLOCK, pairs * BLOCK), 0)
    qry_i = lax.broadcasted_iota(jnp.int32, (BLOCK, pairs * BLOCK), 1) % BLOCK
    in_cur = key_i <= qry_i
    low = _lane_index((BLOCK, LANES)) < SWA_HEAD_DIM
    zero = jnp.zeros((BLOCK, LANES), BF16)

    for kvh in range(SWA_KV_HEADS):
        cols = slice(kvh * LANES, (kvh + 1) * LANES)
        kc, kp = kc_ref[:, cols], kp_ref[:, cols]
        k_halves = (jnp.concatenate([jnp.where(low, kp, zero), jnp.where(low, kc, zero)], axis=0),
                    jnp.concatenate([jnp.where(low, zero, kp), jnp.where(low, zero, kc)], axis=0))
        vrows = slice(kvh * SWA_HEAD_DIM, (kvh + 1) * SWA_HEAD_DIM)
        vt = jnp.concatenate([vtp_ref[vrows, :], vtc_ref[vrows, :]], axis=1)
        head0 = kvh * SWA_GROUP
        q_rows = jnp.concatenate(
            [q_ref[:, (head0 + 2 * pp) * SWA_HEAD_DIM:(head0 + 2 * pp + 2) * SWA_HEAD_DIM]
             for pp in range(pairs)], axis=0)
        outs = []
        for sub in range(2):
            st = _dot_nt(k_halves[sub], q_rows)
            s_prev = jnp.where(has_prev, st[:BLOCK], NEG_INF)
            s = jnp.where(in_cur, st[BLOCK:], s_prev)
            sink = sink_ref[sub * SWA_KV_HEADS + kvh:sub * SWA_KV_HEADS + kvh + 1, :]
            m = jnp.maximum(jnp.max(s, axis=0, keepdims=True), sink)
            e = jnp.exp2(s - m)
            denom = jnp.sum(e, axis=0, keepdims=True) + jnp.exp2(sink - m)
            p = e.astype(BF16)
            pt = jnp.concatenate([jnp.where(in_cur, jnp.zeros_like(p), p),
                                  jnp.where(in_cur, p, jnp.zeros_like(p))], axis=0)
            outs.append(_dot(vt, pt) * (1.0 / denom))
        for pp in range(pairs):
            qc = slice(pp * BLOCK, (pp + 1) * BLOCK)
            y = jnp.concatenate([outs[0][:, qc], outs[1][:, qc]], axis=0).T
            pcols = slice((head0 + 2 * pp) * SWA_HEAD_DIM, (head0 + 2 * pp + 2) * SWA_HEAD_DIM)
            o_ref[:, pcols] = (y * g_ref[:, pcols].astype(F32)).astype(BF16)


def _swa_sink_rows(sinks):
    s2 = (sinks * math.log2(math.e)).reshape(SWA_KV_HEADS, SWA_GROUP // 2, 2)
    rows = jnp.transpose(s2, (2, 0, 1)).reshape(2 * SWA_KV_HEADS, SWA_GROUP // 2, 1)
    return jnp.broadcast_to(rows, (2 * SWA_KV_HEADS, SWA_GROUP // 2, BLOCK)).reshape(
        2 * SWA_KV_HEADS, -1)


def _swa_attn(sinks, q, kd, vt, g, batch, seq):
    tokens = batch * seq
    nb = seq // BLOCK
    cur = lambda b, i: (b * nb + i, 0)
    prev = lambda b, i: (b * nb + jnp.maximum(i - 1, 0), 0)
    cur_t = lambda b, i: (b * nb + i, 0, 0)
    prev_t = lambda b, i: (b * nb + jnp.maximum(i - 1, 0), 0, 0)
    kv_width = SWA_KV_HEADS * LANES
    sink_rows = _swa_sink_rows(sinks)
    return pl.pallas_call(
        _swa_attn_kernel,
        grid=(batch, nb),
        in_specs=[pl.BlockSpec(sink_rows.shape, lambda b, i: (0, 0)),
                  pl.BlockSpec((BLOCK, SWA_WIDTH), cur),
                  pl.BlockSpec((BLOCK, kv_width), cur),
                  pl.BlockSpec((BLOCK, kv_width), prev),
                  pl.BlockSpec((None, SWA_KV_WIDTH, BLOCK), cur_t),
                  pl.BlockSpec((None, SWA_KV_WIDTH, BLOCK), prev_t),
                  pl.BlockSpec((BLOCK, SWA_WIDTH), cur)],
        out_specs=pl.BlockSpec((BLOCK, SWA_WIDTH), cur),
        out_shape=jax.ShapeDtypeStruct((tokens, SWA_WIDTH), BF16),
        compiler_params=pltpu.CompilerParams(
            dimension_semantics=("arbitrary", "arbitrary"), vmem_limit_bytes=VMEM_LIMIT_BYTES),
        name="swa_attn",
    )(sink_rows, q, kd, kd, vt, vt, g)


def _out_kernel(x_ref, y_ref, w_ref, fn_ref, o_ref, *, final):
    y = _load_groups(y_ref) if len(y_ref.shape) == 3 else y_ref[...]
    x_new = x_ref[...] + _dot(y, w_ref[...])
    o_ref[...] = _rmsnorm(x_new, fn_ref[...]) if final else x_new


def _out_proj(x, y, w_out, final_norm, final):
    tokens = x.shape[0]
    rows = OUT_ROWS
    row_spec = lambda width: pl.BlockSpec((rows, width), lambda i: (i, 0))
    if y.ndim == 3:
        y_spec = pl.BlockSpec((y.shape[0], rows, y.shape[2]), lambda i: (0, i, 0))
    else:
        y_spec = row_spec(y.shape[1])
    return pl.pallas_call(
        functools.partial(_out_kernel, final=final),
        grid=(tokens // rows,),
        in_specs=[row_spec(D_MODEL), y_spec, _const_spec(w_out.shape),
                  _const_spec(final_norm.shape)],
        out_specs=row_spec(D_MODEL),
        out_shape=jax.ShapeDtypeStruct((tokens, D_MODEL), F32),
        compiler_params=pltpu.CompilerParams(
            dimension_semantics=("arbitrary",), vmem_limit_bytes=VMEM_LIMIT_BYTES),
        name="out_proj",
    )(x, y, w_out, final_norm)


def _mla_weights(w_in, q_norm, w_uq, kv_norm, w_ukv):
    o1 = MLA_Q_RANK
    o2 = o1 + MLA_KV_RANK
    o3 = o2 + MLA_ROPE
    w_kr = w_in[:, o2:o3]
    uq = w_uq.reshape(MLA_Q_RANK, MLA_HEADS, MLA_NOPE + MLA_ROPE)
    ukv = w_ukv.reshape(MLA_KV_RANK, MLA_HEADS, MLA_NOPE + MLA_V)
    return {
        "wq": w_in[:, :o1].astype(BF16),
        "wkv": w_in[:, o1:o2].astype(BF16),
        "wkr": jnp.concatenate([w_kr, w_kr], axis=1).astype(BF16),
        "wg": w_in[:, o3:].astype(BF16),
        "qnorm": q_norm.reshape(1, -1),
        "kvnorm": kv_norm.reshape(1, -1),
        "wuqn": uq[:, :, :MLA_NOPE].reshape(MLA_Q_RANK, -1).astype(BF16),
        "wuqr": uq[:, :, MLA_NOPE:].reshape(MLA_Q_RANK, -1).astype(BF16),
        "wukn": ukv[:, :, :MLA_NOPE].reshape(MLA_KV_RANK, -1).astype(BF16),
        "wuv": ukv[:, :, MLA_NOPE:].reshape(MLA_KV_RANK, -1).T.astype(BF16),
    }


def _swa_weights(w_in):
    o1 = SWA_WIDTH
    o2 = o1 + SWA_KV_WIDTH
    o3 = o2 + SWA_KV_WIDTH
    return {
        "wq": w_in[:, :o1].astype(BF16),
        "wk": w_in[:, o1:o2].astype(BF16),
        "wv": w_in[:, o2:o3].T.astype(BF16),
        "wg": w_in[:, o3:].astype(BF16),
    }


def kernel(x, positions, layer_norm, mla_w_in, mla_q_norm, mla_w_uq, mla_kv_norm, mla_w_ukv,
           mla_w_out, swa_w_in, swa_sinks, swa_w_out, final_norm):
    batch, seq, d_model = x.shape
    tokens = batch * seq
    assert d_model == D_MODEL and seq % FLASH_BLOCK == 0 and tokens % OUT_ROWS == 0
    assert FLASH_BLOCK % PROJ_ROWS == 0
    mcos, msin, sc, s1, s2 = _rope_tables(positions)
    xt = x.reshape(tokens, d_model)
    fn = final_norm.reshape(1, -1)
    for i in range(DEPTH):
        j = i // N_MIXERS
        ln = layer_norm[i].reshape(1, -1)
        final = i == DEPTH - 1
        if i % N_MIXERS == 0:
            w = _mla_weights(mla_w_in[j], mla_q_norm[j], mla_w_uq[j], mla_kv_norm[j], mla_w_ukv[j])
            qn, qr, kn, vt, krp, g = _mla_in(xt, ln, w, mcos, msin)
            y = _mla_flash(qn, qr, kn, krp, vt, g, batch, seq)
            w_out = mla_w_out[j]
        else:
            w = _swa_weights(swa_w_in[j])
            q, kd, vt, g = _swa_in(xt, ln, w, sc, s1, s2)
            y = _swa_attn(swa_sinks[j], q, kd, vt, g, batch, seq)
            w_out = swa_w_out[j]
        xt = _out_proj(xt, y, w_out.astype(BF16), fn, final)
    return xt.reshape(batch, seq, d_model)
```

```python
import functools
import math

import jax
import jax.numpy as jnp
from jax import lax
from jax.experimental import pallas as pl
from jax.experimental.pallas import tpu as pltpu

D_MODEL = 2048
DEPTH = 4
N_MIXERS = 2
ROPE_THETA = 500000.0
NORM_EPS = 1e-6
BLOCK = 128
NEG_INF = -1e30

MLA_HEADS = 16
MLA_Q_RANK = 512
MLA_KV_RANK = 512
MLA_NOPE = 128
MLA_ROPE = 64
MLA_V = 128
MLA_WIDTH = MLA_HEADS * MLA_V

SWA_Q_HEADS = 32
SWA_KV_HEADS = 4
SWA_GROUP = SWA_Q_HEADS // SWA_KV_HEADS
SWA_HEAD_DIM = 64
SWA_ROPE_DIM = SWA_HEAD_DIM // 4
SWA_WIDTH = SWA_Q_HEADS * SWA_HEAD_DIM
SWA_KV_WIDTH = SWA_KV_HEADS * SWA_HEAD_DIM

LANES = 128
VMEM_LIMIT_BYTES = 56 * 1024 * 1024

PROJ_ROWS = 256
OUT_ROWS = 512
FLASH_BLOCK = 512

BF16 = jnp.bfloat16
F32 = jnp.float32


def _dot(a, b):
    return jnp.dot(a, b, preferred_element_type=F32)


def _dot_nt(a, b):
    return lax.dot_general(a, b, (((1,), (1,)), ((), ())), preferred_element_type=F32)


def _rmsnorm(xf, g):
    y = xf * lax.rsqrt(jnp.mean(xf * xf, axis=-1, keepdims=True) + NORM_EPS)
    return y * g


def _tile_lanes(t, width):
    reps = width // t.shape[1]
    return t if reps == 1 else jnp.concatenate([t] * reps, axis=1)


def _lane_index(shape):
    return lax.broadcasted_iota(jnp.int32, shape, 1)


def _rope_half64(x, cos_t, sin_t):
    w = x.shape[1]
    fwd = pltpu.roll(x, w - MLA_ROPE // 2, 1)
    bwd = pltpu.roll(x, MLA_ROPE // 2, 1)
    first_half = (_lane_index(x.shape) % MLA_ROPE) < (MLA_ROPE // 2)
    swapped = jnp.where(first_half, fwd, bwd)
    return x * _tile_lanes(cos_t, w) + swapped * _tile_lanes(sin_t, w)


def _rope_partial(x, c_t, s1_t, s2_t):
    w = x.shape[1]
    half = SWA_ROPE_DIM // 2
    fwd = pltpu.roll(x, w - half, 1)
    bwd = pltpu.roll(x, half, 1)
    return (x * _tile_lanes(c_t, w) + fwd * _tile_lanes(s1_t, w)
            + bwd * _tile_lanes(s2_t, w))


def _rope_table_kernel(pos_ref, const_ref, mcos_ref, msin_ref, sc_ref, s1_ref, s2_ref):
    pos = pos_ref[...].astype(F32)
    ang_m = pos * const_ref[0:1, :]
    mcos_ref[...] = jnp.cos(ang_m)
    msin_ref[...] = jnp.sin(ang_m) * const_ref[1:2, :]
    ang_s = pos * const_ref[2:3, :]
    sin_s = jnp.sin(ang_s)
    sc_ref[...] = jnp.cos(ang_s)
    s1_ref[...] = sin_s * const_ref[3:4, :]
    s2_ref[...] = sin_s * const_ref[4:5, :]


def _rope_constants():
    lane = jnp.arange(LANES)
    f_mla = ROPE_THETA ** (-jnp.arange(0, MLA_ROPE, 2, dtype=F32) / MLA_ROPE)
    f_swa = ROPE_THETA ** (-jnp.arange(0, SWA_ROPE_DIM, 2, dtype=F32) / SWA_ROPE_DIM)
    half_m = MLA_ROPE // 2
    half_s = SWA_ROPE_DIM // 2
    d_m = lane % MLA_ROPE
    d_s = lane % SWA_HEAD_DIM
    rows = [
        f_mla[d_m % half_m],
        jnp.where(d_m < half_m, -1.0, 1.0),
        jnp.where(d_s < SWA_ROPE_DIM, f_swa[d_s % half_s], 0.0),
        jnp.where(d_s < half_s, -1.0, 0.0),
        jnp.where((d_s >= half_s) & (d_s < SWA_ROPE_DIM), 1.0, 0.0),
    ]
    rows += [jnp.zeros((LANES,), F32)] * 3
    return jnp.stack([r.astype(F32) for r in rows])


def _rope_tables(positions):
    tokens = positions.size
    rows = 1024
    tab = jax.ShapeDtypeStruct((tokens, LANES), F32)
    spec = pl.BlockSpec((rows, LANES), lambda i: (i, 0))
    return pl.pallas_call(
        _rope_table_kernel,
        grid=(tokens // rows,),
        in_specs=[pl.BlockSpec((rows, 1), lambda i: (i, 0)),
                  pl.BlockSpec((8, LANES), lambda i: (0, 0))],
        out_specs=[spec] * 5,
        out_shape=[tab] * 5,
        name="rope_tables",
    )(positions.reshape(tokens, 1), _rope_constants())


def _mla_in_kernel(x_ref, ln_ref, win_ref, qnorm_ref, kvnorm_ref, wuq_ref, wukv_ref,
                   cos_ref, sin_ref, qn_ref, qr_ref, kn_ref, vt_ref, krp_ref, g_ref):
    scale = (MLA_NOPE + MLA_ROPE) ** -0.5 * math.log2(math.e)
    h = _rmsnorm(x_ref[...], ln_ref[...]).astype(BF16)
    cos_t = cos_ref[...]
    sin_t = sin_ref[...]
    rank2 = MLA_Q_RANK + MLA_KV_RANK
    heads = range(MLA_HEADS)

    c = _dot(h, win_ref[:, :rank2])
    cq = _rmsnorm(c[:, :MLA_Q_RANK], qnorm_ref[...]).astype(BF16)
    ckv = _rmsnorm(c[:, MLA_Q_RANK:], kvnorm_ref[...]).astype(BF16)

    q = _dot(cq, wuq_ref[...])
    qd = MLA_NOPE + MLA_ROPE
    qn = jnp.concatenate([q[:, hh * qd:hh * qd + MLA_NOPE] for hh in heads], axis=1)
    qr = jnp.concatenate([q[:, hh * qd + MLA_NOPE:(hh + 1) * qd] for hh in heads], axis=1)
    _store_groups(qn_ref, (qn * scale).astype(BF16))
    _store_groups(qr_ref, (_rope_half64(qr, cos_t, sin_t) * scale).astype(BF16))

    kv = _dot(ckv, wukv_ref[...])
    kd = MLA_NOPE + MLA_V
    kn = jnp.concatenate([kv[:, hh * kd:hh * kd + MLA_NOPE] for hh in heads], axis=1)
    v = jnp.concatenate([kv[:, hh * kd + MLA_NOPE:(hh + 1) * kd] for hh in heads], axis=1)
    _store_groups(kn_ref, kn.astype(BF16))
    vt_ref[0] = v.T.astype(BF16)

    tail = _dot(h, win_ref[:, rank2:])
    kr = tail[:, :MLA_ROPE]
    kr = _rope_half64(jnp.concatenate([kr, kr], axis=1), cos_t, sin_t)
    low = _lane_index(kr.shape) < MLA_ROPE
    krp_ref[:, :LANES] = jnp.where(low, kr, 0.0).astype(BF16)
    krp_ref[:, LANES:] = jnp.where(low, 0.0, kr).astype(BF16)

    gate = tail[:, MLA_ROPE:]
    _store_groups(g_ref, (gate * jax.nn.sigmoid(gate)).astype(BF16))


def _store_groups(ref, val):
    groups, _, width = ref.shape
    for gg in range(groups):
        ref[gg] = val[:, gg * width:(gg + 1) * width]


def _load_groups(ref):
    return jnp.concatenate([ref[gg] for gg in range(ref.shape[0])], axis=1)


def _const_spec(shape):
    return pl.BlockSpec(shape, lambda i: (0,) * len(shape), pipeline_mode=pl.Buffered(1))


def _layer_spec(stacked, layer):
    tail = (0,) * (stacked.ndim - 1)
    return pl.BlockSpec((None,) + stacked.shape[1:], lambda i: (layer,) + tail,
                        pipeline_mode=pl.Buffered(1))


def _mla_in(x, layer_norm, layer, w, j, cos_t, sin_t):
    tokens = x.shape[0]
    rows = PROJ_ROWS
    row_spec = lambda width: pl.BlockSpec((rows, width), lambda i: (i, 0))
    consts = [(layer_norm, layer), (w["w_in"], j), (w["q_norm"], j), (w["kv_norm"], j),
              (w["w_uq"], j), (w["w_ukv"], j)]
    pairs = MLA_HEADS // 2
    per_blk = FLASH_BLOCK // rows

    def pair_major(width):
        return (pl.BlockSpec((pairs, rows, width), lambda i: (0, i, 0)),
                jax.ShapeDtypeStruct((pairs, tokens, width), BF16))

    outs = [pair_major(2 * MLA_NOPE),
            pair_major(2 * MLA_ROPE),
            pair_major(2 * MLA_NOPE),
            (pl.BlockSpec((1, MLA_WIDTH, rows), lambda i: (i // per_blk, 0, i % per_blk)),
             jax.ShapeDtypeStruct((tokens // FLASH_BLOCK, MLA_WIDTH, FLASH_BLOCK), BF16)),
            (row_spec(2 * LANES), jax.ShapeDtypeStruct((tokens, 2 * LANES), BF16)),
            pair_major(2 * MLA_V)]
    return pl.pallas_call(
        _mla_in_kernel,
        grid=(tokens // rows,),
        in_specs=[row_spec(D_MODEL)] + [_layer_spec(c, ll) for c, ll in consts]
                 + [row_spec(LANES), row_spec(LANES)],
        out_specs=[o[0] for o in outs],
        out_shape=[o[1] for o in outs],
        compiler_params=pltpu.CompilerParams(
            dimension_semantics=("arbitrary",), vmem_limit_bytes=VMEM_LIMIT_BYTES),
        name="mla_in",
    )(x, *[c for c, _ in consts], cos_t, sin_t)


def _mla_flash_kernel(qn_ref, qr_ref, kn_ref, kr_ref, vt_ref, g_ref, o_ref,
                      kcat_sc, s_sc, m_sc, l_sc, acc_sc, *, seq):
    blk = FLASH_BLOCK
    pair = (0, 1)
    hcols = [slice(hh * LANES, (hh + 1) * LANES) for hh in pair]
    for hh in pair:
        kcat_sc[hh, :, :LANES] = kn_ref[:, hcols[hh]]
        kcat_sc[hh, :, LANES:] = kr_ref[:, hcols[hh]]
    causal_t = (lax.broadcasted_iota(jnp.int32, (blk, blk), 0)
                <= lax.broadcasted_iota(jnp.int32, (blk, blk), 1))

    for qi in range(seq // blk):
        rows = slice(qi * blk, (qi + 1) * blk)
        qs = [jnp.concatenate([qn_ref[rows, hcols[hh]], qr_ref[rows, :]], axis=1) for hh in pair]
        m_sc[...] = jnp.full(m_sc.shape, NEG_INF, F32)
        l_sc[...] = jnp.zeros(l_sc.shape, F32)
        acc_sc[...] = jnp.zeros(acc_sc.shape, F32)

        def scores(hh, kb, slot, masked, qs=qs):
            kstart = pl.multiple_of(kb * blk, blk)
            st = _dot_nt(kcat_sc[hh, pl.ds(kstart, blk), :], qs[hh])
            s_sc[hh, slot] = jnp.where(causal_t, st, NEG_INF) if masked else st

        def accumulate(hh, kb, slot):
            st = s_sc[hh, slot]
            m_prev = m_sc[hh]
            m_new = jnp.maximum(m_prev, jnp.max(st, axis=0, keepdims=True))
            alpha = jnp.exp2(m_prev - m_new)
            p = jnp.exp2(st - m_new)
            l_sc[hh] = alpha * l_sc[hh] + jnp.sum(p, axis=0, keepdims=True)
            acc_sc[hh] = alpha * acc_sc[hh] + _dot(vt_ref[kb, hcols[hh], :], p.astype(BF16))
            m_sc[hh] = m_new

        visits = [qi] + list(range(qi))
        for hh in pair:
            scores(hh, visits[0], 0, True)
        for i, kb in enumerate(visits):
            for hh in pair:
                if i + 1 < len(visits):
                    scores(hh, visits[i + 1], (i + 1) % 2, False)
                accumulate(hh, kb, i % 2)

        for hh in pair:
            y = (acc_sc[hh] * (1.0 / l_sc[hh])).T
            o_ref[rows, hcols[hh]] = (y * g_ref[rows, hcols[hh]].astype(F32)).astype(BF16)


def _mla_flash(qn, qr, kn, krp, vt, g, batch, seq):
    tokens = batch * seq
    nblk = seq // FLASH_BLOCK
    pair_spec = pl.BlockSpec((None, seq, 2 * LANES), lambda b, p: (p, b, 0))
    return pl.pallas_call(
        functools.partial(_mla_flash_kernel, seq=seq),
        grid=(batch, MLA_HEADS // 2),
        in_specs=[pair_spec,
                  pl.BlockSpec((None, seq, LANES), lambda b, p: (p, b, 0)),
                  pair_spec,
                  pl.BlockSpec((seq, 2 * LANES), lambda b, p: (b, 0)),
                  pl.BlockSpec((nblk, 2 * LANES, FLASH_BLOCK), lambda b, p: (b, p, 0)),
                  pair_spec],
        out_specs=pair_spec,
        out_shape=jax.ShapeDtypeStruct((MLA_HEADS // 2, tokens, 2 * MLA_V), BF16),
        scratch_shapes=[pltpu.VMEM((2, seq, 2 * LANES), BF16),
                        pltpu.VMEM((2, 2, FLASH_BLOCK, FLASH_BLOCK), F32),
                        pltpu.VMEM((2, 1, FLASH_BLOCK), F32),
                        pltpu.VMEM((2, 1, FLASH_BLOCK), F32),
                        pltpu.VMEM((2, MLA_V, FLASH_BLOCK), F32)],
        compiler_params=pltpu.CompilerParams(
            dimension_semantics=("arbitrary", "arbitrary"), vmem_limit_bytes=VMEM_LIMIT_BYTES),
        name="mla_flash",
    )(qn, qr, kn, krp, vt, g)


def _dup_heads(t):
    parts = []
    for hh in range(t.shape[1] // SWA_HEAD_DIM):
        piece = t[:, hh * SWA_HEAD_DIM:(hh + 1) * SWA_HEAD_DIM]
        parts += [piece, piece]
    return jnp.concatenate(parts, axis=1)


def _swa_in_kernel(x_ref, ln_ref, win_ref, c_ref, s1_ref, s2_ref,
                   q_ref, kd_ref, vt_ref, g_ref):
    scale = SWA_HEAD_DIM ** -0.5 * math.log2(math.e)
    o1 = SWA_WIDTH
    o2 = o1 + SWA_KV_WIDTH
    o3 = o2 + SWA_KV_WIDTH
    h = _rmsnorm(x_ref[...], ln_ref[...]).astype(BF16)
    c_t, s1_t, s2_t = c_ref[...], s1_ref[...], s2_ref[...]
    q = _rope_partial(_dot(h, win_ref[:, :o1]), c_t, s1_t, s2_t)
    q_ref[...] = (q * scale).astype(BF16)
    kv = _dot(h, win_ref[:, o1:o3])
    k = _rope_partial(kv[:, :SWA_KV_WIDTH], c_t, s1_t, s2_t)
    kd_ref[...] = _dup_heads(k).astype(BF16)
    vt = kv[:, SWA_KV_WIDTH:].T.astype(BF16)
    for cc in range(vt_ref.shape[0]):
        vt_ref[cc] = vt[:, cc * BLOCK:(cc + 1) * BLOCK]
    gate = _dot(h, win_ref[:, o3:])
    g_ref[...] = (gate * jax.nn.sigmoid(gate)).astype(BF16)


def _swa_in(x, layer_norm, layer, w_in, j, c_t, s1_t, s2_t):
    tokens = x.shape[0]
    rows = PROJ_ROWS
    row_spec = lambda width: pl.BlockSpec((rows, width), lambda i: (i, 0))
    consts = [(layer_norm, layer), (w_in, j)]
    out_widths = [SWA_WIDTH, SWA_KV_HEADS * LANES, None, SWA_WIDTH]
    vt_spec = pl.BlockSpec((rows // BLOCK, SWA_KV_WIDTH, BLOCK), lambda i: (i, 0, 0))
    vt_shape = jax.ShapeDtypeStruct((tokens // BLOCK, SWA_KV_WIDTH, BLOCK), BF16)
    return pl.pallas_call(
        _swa_in_kernel,
        grid=(tokens // rows,),
        in_specs=[row_spec(D_MODEL)] + [_layer_spec(c, ll) for c, ll in consts]
                 + [row_spec(LANES)] * 3,
        out_specs=[vt_spec if wd is None else row_spec(wd) for wd in out_widths],
        out_shape=[vt_shape if wd is None else jax.ShapeDtypeStruct((tokens, wd), BF16)
                   for wd in out_widths],
        compiler_params=pltpu.CompilerParams(
            dimension_semantics=("arbitrary",), vmem_limit_bytes=VMEM_LIMIT_BYTES),
        name="swa_in",
    )(x, *[c for c, _ in consts], c_t, s1_t, s2_t)


def _swa_attn_kernel(sink_ref, q_ref, kc_ref, kp_ref, vtc_ref, vtp_ref, g_ref, o_ref):
    has_prev = pl.program_id(1) > 0
    pairs = SWA_GROUP // 2
    key_i = lax.broadcasted_iota(jnp.int32, (BLOCK, pairs * BLOCK), 0)
    qry_i = lax.broadcasted_iota(jnp.int32, (BLOCK, pairs * BLOCK), 1) % BLOCK
    in_cur = key_i <= qry_i
    low = _lane_index((BLOCK, LANES)) < SWA_HEAD_DIM
    zero = jnp.zeros((BLOCK, LANES), BF16)

    for kvh in range(SWA_KV_HEADS):
        cols = slice(kvh * LANES, (kvh + 1) * LANES)
        kc, kp = kc_ref[:, cols], kp_ref[:, cols]
        k_halves = (jnp.concatenate([jnp.where(low, kp, zero), jnp.where(low, kc, zero)], axis=0),
                    jnp.concatenate([jnp.where(low, zero, kp), jnp.where(low, zero, kc)], axis=0))
        vrows = slice(kvh * SWA_HEAD_DIM, (kvh + 1) * SWA_HEAD_DIM)
        vt = jnp.concatenate([vtp_ref[vrows, :], vtc_ref[vrows, :]], axis=1)
        head0 = kvh * SWA_GROUP
        q_rows = jnp.concatenate(
            [q_ref[:, (head0 + 2 * pp) * SWA_HEAD_DIM:(head0 + 2 * pp + 2) * SWA_HEAD_DIM]
             for pp in range(pairs)], axis=0)
        outs = []
        for sub in range(2):
            st = _dot_nt(k_halves[sub], q_rows)
            s_prev = jnp.where(has_prev, st[:BLOCK], NEG_INF)
            s = jnp.where(in_cur, st[BLOCK:], s_prev)
            sink = sink_ref[sub * SWA_KV_HEADS + kvh:sub * SWA_KV_HEADS + kvh + 1, :]
            m = jnp.maximum(jnp.max(s, axis=0, keepdims=True), sink)
            e = jnp.exp2(s - m)
            denom = jnp.sum(e, axis=0, keepdims=True) + jnp.exp2(sink - m)
            p = e.astype(BF16)
            pt = jnp.concatenate([jnp.where(in_cur, jnp.zeros_like(p), p),
                                  jnp.where(in_cur, p, jnp.zeros_like(p))], axis=0)
            outs.append(_dot(vt, pt) * (1.0 / denom))
        for pp in range(pairs):
            qc = slice(pp * BLOCK, (pp + 1) * BLOCK)
            y = jnp.concatenate([outs[0][:, qc], outs[1][:, qc]], axis=0).T
            pcols = slice((head0 + 2 * pp) * SWA_HEAD_DIM, (head0 + 2 * pp + 2) * SWA_HEAD_DIM)
            o_ref[:, pcols] = (y * g_ref[:, pcols].astype(F32)).astype(BF16)


def _swa_sink_rows(sinks):
    s2 = (sinks * math.log2(math.e)).reshape(SWA_KV_HEADS, SWA_GROUP // 2, 2)
    rows = jnp.transpose(s2, (2, 0, 1)).reshape(2 * SWA_KV_HEADS, SWA_GROUP // 2, 1)
    return jnp.broadcast_to(rows, (2 * SWA_KV_HEADS, SWA_GROUP // 2, BLOCK)).reshape(
        2 * SWA_KV_HEADS, -1)


def _swa_attn(sinks, q, kd, vt, g, batch, seq):
    tokens = batch * seq
    nb = seq // BLOCK
    cur = lambda b, i: (b * nb + i, 0)
    prev = lambda b, i: (b * nb + jnp.maximum(i - 1, 0), 0)
    cur_t = lambda b, i: (b * nb + i, 0, 0)
    prev_t = lambda b, i: (b * nb + jnp.maximum(i - 1, 0), 0, 0)
    kv_width = SWA_KV_HEADS * LANES
    sink_rows = _swa_sink_rows(sinks)
    return pl.pallas_call(
        _swa_attn_kernel,
        grid=(batch, nb),
        in_specs=[pl.BlockSpec(sink_rows.shape, lambda b, i: (0, 0)),
                  pl.BlockSpec((BLOCK, SWA_WIDTH), cur),
                  pl.BlockSpec((BLOCK, kv_width), cur),
                  pl.BlockSpec((BLOCK, kv_width), prev),
                  pl.BlockSpec((None, SWA_KV_WIDTH, BLOCK), cur_t),
                  pl.BlockSpec((None, SWA_KV_WIDTH, BLOCK), prev_t),
                  pl.BlockSpec((BLOCK, SWA_WIDTH), cur)],
        out_specs=pl.BlockSpec((BLOCK, SWA_WIDTH), cur),
        out_shape=jax.ShapeDtypeStruct((tokens, SWA_WIDTH), BF16),
        compiler_params=pltpu.CompilerParams(
            dimension_semantics=("arbitrary", "arbitrary"), vmem_limit_bytes=VMEM_LIMIT_BYTES),
        name="swa_attn",
    )(sink_rows, q, kd, kd, vt, vt, g)


def _out_kernel(x_ref, y_ref, w_ref, fn_ref, o_ref, w_sc, *, final):
    @pl.when(pl.program_id(0) == 0)
    def _():
        w_sc[...] = w_ref[...].astype(BF16)

    y = _load_groups(y_ref) if len(y_ref.shape) == 3 else y_ref[...]
    x_new = x_ref[...] + _dot(y, w_sc[...])
    o_ref[...] = _rmsnorm(x_new, fn_ref[...]) if final else x_new


def _out_proj(x, y, w_out, j, final_norm, final):
    tokens = x.shape[0]
    rows = OUT_ROWS
    row_spec = lambda width: pl.BlockSpec((rows, width), lambda i: (i, 0))
    if y.ndim == 3:
        y_spec = pl.BlockSpec((y.shape[0], rows, y.shape[2]), lambda i: (0, i, 0))
    else:
        y_spec = row_spec(y.shape[1])
    return pl.pallas_call(
        functools.partial(_out_kernel, final=final),
        grid=(tokens // rows,),
        in_specs=[row_spec(D_MODEL), y_spec, _layer_spec(w_out, j),
                  _const_spec(final_norm.shape)],
        out_specs=row_spec(D_MODEL),
        out_shape=jax.ShapeDtypeStruct((tokens, D_MODEL), F32),
        scratch_shapes=[pltpu.VMEM(w_out.shape[1:], BF16)],
        compiler_params=pltpu.CompilerParams(
            dimension_semantics=("arbitrary",), vmem_limit_bytes=VMEM_LIMIT_BYTES),
        name="out_proj",
    )(x, y, w_out, final_norm)


def kernel(x, positions, layer_norm, mla_w_in, mla_q_norm, mla_w_uq, mla_kv_norm, mla_w_ukv,
           mla_w_out, swa_w_in, swa_sinks, swa_w_out, final_norm):
    batch, seq, d_model = x.shape
    tokens = batch * seq
    assert d_model == D_MODEL and seq % FLASH_BLOCK == 0 and tokens % OUT_ROWS == 0
    assert FLASH_BLOCK % PROJ_ROWS == 0
    mcos, msin, sc, s1, s2 = _rope_tables(positions)
    xt = x.reshape(tokens, d_model)
    fn = final_norm.reshape(1, -1)
    ln_stack = layer_norm[:, None, :]
    mla_w = {"w_in": mla_w_in.astype(BF16), "w_uq": mla_w_uq.astype(BF16),
             "w_ukv": mla_w_ukv.astype(BF16), "q_norm": mla_q_norm[:, None, :],
             "kv_norm": mla_kv_norm[:, None, :]}
    swa_w = swa_w_in.astype(BF16)
    for i in range(DEPTH):
        j = i // N_MIXERS
        final = i == DEPTH - 1
        if i % N_MIXERS == 0:
            qn, qr, kn, vt, krp, g = _mla_in(xt, ln_stack, i, mla_w, j, mcos, msin)
            y = _mla_flash(qn, qr, kn, krp, vt, g, batch, seq)
            w_out = mla_w_out
        else:
            q, kd, vt, g = _swa_in(xt, ln_stack, i, swa_w, j, sc, s1, s2)
            y = _swa_attn(swa_sinks[j], q, kd, vt, g, batch, seq)
            w_out = swa_w_out
        xt = _out_proj(xt, y, w_out, j, fn, final)
    return xt.reshape(batch, seq, d_model)
```

```python
import functools
import math

import jax
import jax.numpy as jnp
from jax import lax
from jax.experimental import pallas as pl
from jax.experimental.pallas import tpu as pltpu

D_MODEL = 2048
DEPTH = 4
N_MIXERS = 2
ROPE_THETA = 500000.0
NORM_EPS = 1e-6
BLOCK = 128
NEG_INF = -1e30

MLA_HEADS = 16
MLA_Q_RANK = 512
MLA_KV_RANK = 512
MLA_NOPE = 128
MLA_ROPE = 64
MLA_V = 128
MLA_WIDTH = MLA_HEADS * MLA_V

SWA_Q_HEADS = 32
SWA_KV_HEADS = 4
SWA_GROUP = SWA_Q_HEADS // SWA_KV_HEADS
SWA_HEAD_DIM = 64
SWA_ROPE_DIM = SWA_HEAD_DIM // 4
SWA_WIDTH = SWA_Q_HEADS * SWA_HEAD_DIM
SWA_KV_WIDTH = SWA_KV_HEADS * SWA_HEAD_DIM

LANES = 128
VMEM_LIMIT_BYTES = 56 * 1024 * 1024

MLA_PROJ_ROWS = 256
SWA_PROJ_ROWS = 512
OUT_ROWS = 512
FLASH_BLOCK = 512

BF16 = jnp.bfloat16
F32 = jnp.float32


def _dot(a, b):
    return jnp.dot(a, b, preferred_element_type=F32)


def _dot_nt(a, b):
    return lax.dot_general(a, b, (((1,), (1,)), ((), ())), preferred_element_type=F32)


def _rmsnorm(xf, g):
    y = xf * lax.rsqrt(jnp.mean(xf * xf, axis=-1, keepdims=True) + NORM_EPS)
    return y * g


def _tile_lanes(t, width):
    reps = width // t.shape[1]
    return t if reps == 1 else jnp.concatenate([t] * reps, axis=1)


def _lane_index(shape):
    return lax.broadcasted_iota(jnp.int32, shape, 1)


def _rope_half64(x, cos_t, sin_t):
    w = x.shape[1]
    fwd = pltpu.roll(x, w - MLA_ROPE // 2, 1)
    bwd = pltpu.roll(x, MLA_ROPE // 2, 1)
    first_half = (_lane_index(x.shape) % MLA_ROPE) < (MLA_ROPE // 2)
    swapped = jnp.where(first_half, fwd, bwd)
    return x * _tile_lanes(cos_t, w) + swapped * _tile_lanes(sin_t, w)


def _rope_partial(x, c_t, s1_t, s2_t):
    w = x.shape[1]
    half = SWA_ROPE_DIM // 2
    fwd = pltpu.roll(x, w - half, 1)
    bwd = pltpu.roll(x, half, 1)
    return (x * _tile_lanes(c_t, w) + fwd * _tile_lanes(s1_t, w)
            + bwd * _tile_lanes(s2_t, w))


def _rope_table_kernel(pos_ref, const_ref, mcos_ref, msin_ref, sc_ref, s1_ref, s2_ref):
    pos = pos_ref[...].astype(F32)
    ang_m = pos * const_ref[0:1, :]
    mcos_ref[...] = jnp.cos(ang_m)
    msin_ref[...] = jnp.sin(ang_m) * const_ref[1:2, :]
    ang_s = pos * const_ref[2:3, :]
    sin_s = jnp.sin(ang_s)
    sc_ref[...] = jnp.cos(ang_s)
    s1_ref[...] = sin_s * const_ref[3:4, :]
    s2_ref[...] = sin_s * const_ref[4:5, :]


def _rope_constants():
    lane = jnp.arange(LANES)
    f_mla = ROPE_THETA ** (-jnp.arange(0, MLA_ROPE, 2, dtype=F32) / MLA_ROPE)
    f_swa = ROPE_THETA ** (-jnp.arange(0, SWA_ROPE_DIM, 2, dtype=F32) / SWA_ROPE_DIM)
    half_m = MLA_ROPE // 2
    half_s = SWA_ROPE_DIM // 2
    d_m = lane % MLA_ROPE
    d_s = lane % SWA_HEAD_DIM
    rows = [
        f_mla[d_m % half_m],
        jnp.where(d_m < half_m, -1.0, 1.0),
        jnp.where(d_s < SWA_ROPE_DIM, f_swa[d_s % half_s], 0.0),
        jnp.where(d_s < half_s, -1.0, 0.0),
        jnp.where((d_s >= half_s) & (d_s < SWA_ROPE_DIM), 1.0, 0.0),
    ]
    rows += [jnp.zeros((LANES,), F32)] * 3
    return jnp.stack([r.astype(F32) for r in rows])


def _rope_tables(positions):
    tokens = positions.size
    rows = 1024
    tab = jax.ShapeDtypeStruct((tokens, LANES), F32)
    spec = pl.BlockSpec((rows, LANES), lambda i: (i, 0))
    return pl.pallas_call(
        _rope_table_kernel,
        grid=(tokens // rows,),
        in_specs=[pl.BlockSpec((rows, 1), lambda i: (i, 0)),
                  pl.BlockSpec((8, LANES), lambda i: (0, 0))],
        out_specs=[spec] * 5,
        out_shape=[tab] * 5,
        name="rope_tables",
    )(positions.reshape(tokens, 1), _rope_constants())


def _mla_in_kernel(x_ref, ln_ref, win_ref, qnorm_ref, kvnorm_ref, wuq_ref, wukv_ref,
                   cos_ref, sin_ref, qn_ref, qr_ref, kn_ref, vt_ref, krp_ref, g_ref):
    scale = (MLA_NOPE + MLA_ROPE) ** -0.5 * math.log2(math.e)
    h = _rmsnorm(x_ref[...], ln_ref[...]).astype(BF16)
    cos_t = cos_ref[...]
    sin_t = sin_ref[...]
    rank2 = MLA_Q_RANK + MLA_KV_RANK
    heads = range(MLA_HEADS)

    c = _dot(h, win_ref[:, :rank2])
    cq = _rmsnorm(c[:, :MLA_Q_RANK], qnorm_ref[...]).astype(BF16)
    ckv = _rmsnorm(c[:, MLA_Q_RANK:], kvnorm_ref[...]).astype(BF16)

    q = _dot(cq, wuq_ref[...])
    qd = MLA_NOPE + MLA_ROPE
    qn = jnp.concatenate([q[:, hh * qd:hh * qd + MLA_NOPE] for hh in heads], axis=1)
    qr = jnp.concatenate([q[:, hh * qd + MLA_NOPE:(hh + 1) * qd] for hh in heads], axis=1)
    _store_groups(qn_ref, (qn * scale).astype(BF16))
    _store_groups(qr_ref, (_rope_half64(qr, cos_t, sin_t) * scale).astype(BF16))

    kv = _dot(ckv, wukv_ref[...])
    kd = MLA_NOPE + MLA_V
    kn = jnp.concatenate([kv[:, hh * kd:hh * kd + MLA_NOPE] for hh in heads], axis=1)
    v = jnp.concatenate([kv[:, hh * kd + MLA_NOPE:(hh + 1) * kd] for hh in heads], axis=1)
    _store_groups(kn_ref, kn.astype(BF16))
    vt_ref[0] = v.T.astype(BF16)

    tail = _dot(h, win_ref[:, rank2:])
    kr = tail[:, :MLA_ROPE]
    kr = _rope_half64(jnp.concatenate([kr, kr], axis=1), cos_t, sin_t)
    low = _lane_index(kr.shape) < MLA_ROPE
    krp_ref[:, :LANES] = jnp.where(low, kr, 0.0).astype(BF16)
    krp_ref[:, LANES:] = jnp.where(low, 0.0, kr).astype(BF16)

    gate = tail[:, MLA_ROPE:]
    _store_groups(g_ref, (gate * jax.nn.sigmoid(gate)).astype(BF16))


def _store_groups(ref, val):
    groups, _, width = ref.shape
    for gg in range(groups):
        ref[gg] = val[:, gg * width:(gg + 1) * width]


def _load_groups(ref):
    return jnp.concatenate([ref[gg] for gg in range(ref.shape[0])], axis=1)


def _const_spec(shape):
    return pl.BlockSpec(shape, lambda i: (0,) * len(shape), pipeline_mode=pl.Buffered(1))


def _layer_spec(stacked, layer):
    tail = (0,) * (stacked.ndim - 1)
    return pl.BlockSpec((None,) + stacked.shape[1:], lambda i: (layer,) + tail,
                        pipeline_mode=pl.Buffered(1))


def _mla_in(x, layer_norm, layer, w, j, cos_t, sin_t):
    tokens = x.shape[0]
    rows = MLA_PROJ_ROWS
    row_spec = lambda width: pl.BlockSpec((rows, width), lambda i: (i, 0))
    consts = [(layer_norm, layer), (w["w_in"], j), (w["q_norm"], j), (w["kv_norm"], j),
              (w["w_uq"], j), (w["w_ukv"], j)]
    pairs = MLA_HEADS // 2
    per_blk = FLASH_BLOCK // rows

    def pair_major(width):
        return (pl.BlockSpec((pairs, rows, width), lambda i: (0, i, 0)),
                jax.ShapeDtypeStruct((pairs, tokens, width), BF16))

    outs = [pair_major(2 * MLA_NOPE),
            pair_major(2 * MLA_ROPE),
            pair_major(2 * MLA_NOPE),
            (pl.BlockSpec((1, MLA_WIDTH, rows), lambda i: (i // per_blk, 0, i % per_blk)),
             jax.ShapeDtypeStruct((tokens // FLASH_BLOCK, MLA_WIDTH, FLASH_BLOCK), BF16)),
            (row_spec(2 * LANES), jax.ShapeDtypeStruct((tokens, 2 * LANES), BF16)),
            pair_major(2 * MLA_V)]
    return pl.pallas_call(
        _mla_in_kernel,
        grid=(tokens // rows,),
        in_specs=[row_spec(D_MODEL)] + [_layer_spec(c, ll) for c, ll in consts]
                 + [row_spec(LANES), row_spec(LANES)],
        out_specs=[o[0] for o in outs],
        out_shape=[o[1] for o in outs],
        compiler_params=pltpu.CompilerParams(
            dimension_semantics=("arbitrary",), vmem_limit_bytes=VMEM_LIMIT_BYTES),
        name="mla_in",
    )(x, *[c for c, _ in consts], cos_t, sin_t)


def _mla_flash_kernel(qn_ref, qr_ref, kn_ref, kr_ref, vt_ref, g_ref, o_ref,
                      kcat_sc, s_sc, m_sc, l_sc, acc_sc, *, seq):
    blk = FLASH_BLOCK
    pair = (0, 1)
    hcols = [slice(hh * LANES, (hh + 1) * LANES) for hh in pair]
    for hh in pair:
        kcat_sc[hh, :, :LANES] = kn_ref[:, hcols[hh]]
        kcat_sc[hh, :, LANES:] = kr_ref[:, hcols[hh]]
    causal_t = (lax.broadcasted_iota(jnp.int32, (blk, blk), 0)
                <= lax.broadcasted_iota(jnp.int32, (blk, blk), 1))

    for qi in range(seq // blk):
        rows = slice(qi * blk, (qi + 1) * blk)
        qs = [jnp.concatenate([qn_ref[rows, hcols[hh]], qr_ref[rows, :]], axis=1) for hh in pair]
        m_sc[...] = jnp.full(m_sc.shape, NEG_INF, F32)
        l_sc[...] = jnp.zeros(l_sc.shape, F32)
        acc_sc[...] = jnp.zeros(acc_sc.shape, F32)

        def scores(hh, kb, slot, masked, qs=qs):
            kstart = pl.multiple_of(kb * blk, blk)
            st = _dot_nt(kcat_sc[hh, pl.ds(kstart, blk), :], qs[hh])
            s_sc[hh, slot] = jnp.where(causal_t, st, NEG_INF) if masked else st

        def accumulate(hh, kb, slot):
            st = s_sc[hh, slot]
            m_prev = m_sc[hh]
            m_new = jnp.maximum(m_prev, jnp.max(st, axis=0, keepdims=True))
            alpha = jnp.exp2(m_prev - m_new)
            p = jnp.exp2(st - m_new)
            l_sc[hh] = alpha * l_sc[hh] + jnp.sum(p, axis=0, keepdims=True)
            acc_sc[hh] = alpha * acc_sc[hh] + _dot(vt_ref[kb, hcols[hh], :], p.astype(BF16))
            m_sc[hh] = m_new

        visits = [qi] + list(range(qi))
        for hh in pair:
            scores(hh, visits[0], 0, True)
        for i, kb in enumerate(visits):
            for hh in pair:
                if i + 1 < len(visits):
                    scores(hh, visits[i + 1], (i + 1) % 2, False)
                accumulate(hh, kb, i % 2)

        for hh in pair:
            y = (acc_sc[hh] * (1.0 / l_sc[hh])).T
            o_ref[rows, hcols[hh]] = (y * g_ref[rows, hcols[hh]].astype(F32)).astype(BF16)


def _mla_flash(qn, qr, kn, krp, vt, g, batch, seq):
    tokens = batch * seq
    nblk = seq // FLASH_BLOCK
    pair_spec = pl.BlockSpec((None, seq, 2 * LANES), lambda b, p: (p, b, 0))
    return pl.pallas_call(
        functools.partial(_mla_flash_kernel, seq=seq),
        grid=(batch, MLA_HEADS // 2),
        in_specs=[pair_spec,
                  pl.BlockSpec((None, seq, LANES), lambda b, p: (p, b, 0)),
                  pair_spec,
                  pl.BlockSpec((seq, 2 * LANES), lambda b, p: (b, 0)),
                  pl.BlockSpec((nblk, 2 * LANES, FLASH_BLOCK), lambda b, p: (b, p, 0)),
                  pair_spec],
        out_specs=pair_spec,
        out_shape=jax.ShapeDtypeStruct((MLA_HEADS // 2, tokens, 2 * MLA_V), BF16),
        scratch_shapes=[pltpu.VMEM((2, seq, 2 * LANES), BF16),
                        pltpu.VMEM((2, 2, FLASH_BLOCK, FLASH_BLOCK), F32),
                        pltpu.VMEM((2, 1, FLASH_BLOCK), F32),
                        pltpu.VMEM((2, 1, FLASH_BLOCK), F32),
                        pltpu.VMEM((2, MLA_V, FLASH_BLOCK), F32)],
        compiler_params=pltpu.CompilerParams(
            dimension_semantics=("arbitrary", "arbitrary"), vmem_limit_bytes=VMEM_LIMIT_BYTES),
        name="mla_flash",
    )(qn, qr, kn, krp, vt, g)


def _dup_heads(t):
    parts = []
    for hh in range(t.shape[1] // SWA_HEAD_DIM):
        piece = t[:, hh * SWA_HEAD_DIM:(hh + 1) * SWA_HEAD_DIM]
        parts += [piece, piece]
    return jnp.concatenate(parts, axis=1)


def _swa_in_kernel(x_ref, ln_ref, win_ref, c_ref, s1_ref, s2_ref,
                   q_ref, kd_ref, vt_ref, g_ref):
    scale = SWA_HEAD_DIM ** -0.5 * math.log2(math.e)
    o1 = SWA_WIDTH
    o2 = o1 + SWA_KV_WIDTH
    o3 = o2 + SWA_KV_WIDTH
    h = _rmsnorm(x_ref[...], ln_ref[...]).astype(BF16)
    c_t, s1_t, s2_t = c_ref[...], s1_ref[...], s2_ref[...]
    q = _rope_partial(_dot(h, win_ref[:, :o1]), c_t, s1_t, s2_t)
    q_ref[...] = (q * scale).astype(BF16)
    kv = _dot(h, win_ref[:, o1:o3])
    k = _rope_partial(kv[:, :SWA_KV_WIDTH], c_t, s1_t, s2_t)
    kd_ref[...] = _dup_heads(k).astype(BF16)
    vt = kv[:, SWA_KV_WIDTH:].T.astype(BF16)
    for cc in range(vt_ref.shape[0]):
        vt_ref[cc] = vt[:, cc * BLOCK:(cc + 1) * BLOCK]
    gate = _dot(h, win_ref[:, o3:])
    g_ref[...] = (gate * jax.nn.sigmoid(gate)).astype(BF16)


def _swa_in(x, layer_norm, layer, w_in, j, c_t, s1_t, s2_t):
    tokens = x.shape[0]
    rows = SWA_PROJ_ROWS
    row_spec = lambda width: pl.BlockSpec((rows, width), lambda i: (i, 0))
    consts = [(layer_norm, layer), (w_in, j)]
    out_widths = [SWA_WIDTH, SWA_KV_HEADS * LANES, None, SWA_WIDTH]
    vt_spec = pl.BlockSpec((rows // BLOCK, SWA_KV_WIDTH, BLOCK), lambda i: (i, 0, 0))
    vt_shape = jax.ShapeDtypeStruct((tokens // BLOCK, SWA_KV_WIDTH, BLOCK), BF16)
    return pl.pallas_call(
        _swa_in_kernel,
        grid=(tokens // rows,),
        in_specs=[row_spec(D_MODEL)] + [_layer_spec(c, ll) for c, ll in consts]
                 + [row_spec(LANES)] * 3,
        out_specs=[vt_spec if wd is None else row_spec(wd) for wd in out_widths],
        out_shape=[vt_shape if wd is None else jax.ShapeDtypeStruct((tokens, wd), BF16)
                   for wd in out_widths],
        compiler_params=pltpu.CompilerParams(
            dimension_semantics=("arbitrary",), vmem_limit_bytes=VMEM_LIMIT_BYTES),
        name="swa_in",
    )(x, *[c for c, _ in consts], c_t, s1_t, s2_t)


def _swa_attn_kernel(sink_ref, q_ref, kc_ref, kp_ref, vtc_ref, vtp_ref, g_ref, o_ref):
    has_prev = pl.program_id(1) > 0
    pairs = SWA_GROUP // 2
    key_i = lax.broadcasted_iota(jnp.int32, (BLOCK, pairs * BLOCK), 0)
    qry_i = lax.broadcasted_iota(jnp.int32, (BLOCK, pairs * BLOCK), 1) % BLOCK
    in_cur = key_i <= qry_i
    cur_w = jnp.where(in_cur, 1.0, 0.0).astype(BF16)
    low = _lane_index((BLOCK, LANES)) < SWA_HEAD_DIM
    zero = jnp.zeros((BLOCK, LANES), BF16)

    def scores(kvh, sub):
        cols = slice(kvh * LANES, (kvh + 1) * LANES)
        kc, kp = kc_ref[:, cols], kp_ref[:, cols]
        if sub == 0:
            k_half = jnp.concatenate([jnp.where(low, kp, zero), jnp.where(low, kc, zero)], axis=0)
        else:
            k_half = jnp.concatenate([jnp.where(low, zero, kp), jnp.where(low, zero, kc)], axis=0)
        head0 = kvh * SWA_GROUP
        q_rows = jnp.concatenate(
            [q_ref[:, (head0 + 2 * pp) * SWA_HEAD_DIM:(head0 + 2 * pp + 2) * SWA_HEAD_DIM]
             for pp in range(pairs)], axis=0)
        return _dot_nt(k_half, q_rows)

    def attend(kvh, sub, st):
        s_prev = jnp.where(has_prev, st[:BLOCK], NEG_INF)
        s = jnp.where(in_cur, st[BLOCK:], s_prev)
        sink = sink_ref[sub * SWA_KV_HEADS + kvh:sub * SWA_KV_HEADS + kvh + 1, :]
        m = jnp.maximum(jnp.max(s, axis=0, keepdims=True), sink)
        e = jnp.exp2(s - m)
        denom = jnp.sum(e, axis=0, keepdims=True) + jnp.exp2(sink - m)
        p = e.astype(BF16)
        p_cur = p * cur_w
        pt = jnp.concatenate([p - p_cur, p_cur], axis=0)
        vrows = slice(kvh * SWA_HEAD_DIM, (kvh + 1) * SWA_HEAD_DIM)
        vt = jnp.concatenate([vtp_ref[vrows, :], vtc_ref[vrows, :]], axis=1)
        return _dot(vt, pt) * (1.0 / denom)

    chains = [(kvh, sub) for kvh in range(SWA_KV_HEADS) for sub in range(2)]
    st_next = scores(*chains[0])
    outs = {}
    for ci, (kvh, sub) in enumerate(chains):
        st = st_next
        if ci + 1 < len(chains):
            st_next = scores(*chains[ci + 1])
        outs[sub] = attend(kvh, sub, st)
        if sub == 1:
            head0 = kvh * SWA_GROUP
            for pp in range(pairs):
                qc = slice(pp * BLOCK, (pp + 1) * BLOCK)
                y = jnp.concatenate([outs[0][:, qc], outs[1][:, qc]], axis=0).T
                pcols = slice((head0 + 2 * pp) * SWA_HEAD_DIM,
                              (head0 + 2 * pp + 2) * SWA_HEAD_DIM)
                o_ref[:, pcols] = (y * g_ref[:, pcols].astype(F32)).astype(BF16)


def _swa_sink_rows(sinks):
    s2 = (sinks * math.log2(math.e)).reshape(SWA_KV_HEADS, SWA_GROUP // 2, 2)
    rows = jnp.transpose(s2, (2, 0, 1)).reshape(2 * SWA_KV_HEADS, SWA_GROUP // 2, 1)
    return jnp.broadcast_to(rows, (2 * SWA_KV_HEADS, SWA_GROUP // 2, BLOCK)).reshape(
        2 * SWA_KV_HEADS, -1)


def _swa_attn(sinks, q, kd, vt, g, batch, seq):
    tokens = batch * seq
    nb = seq // BLOCK
    cur = lambda b, i: (b * nb + i, 0)
    prev = lambda b, i: (b * nb + jnp.maximum(i - 1, 0), 0)
    cur_t = lambda b, i: (b * nb + i, 0, 0)
    prev_t = lambda b, i: (b * nb + jnp.maximum(i - 1, 0), 0, 0)
    kv_width = SWA_KV_HEADS * LANES
    sink_rows = _swa_sink_rows(sinks)
    return pl.pallas_call(
        _swa_attn_kernel,
        grid=(batch, nb),
        in_specs=[pl.BlockSpec(sink_rows.shape, lambda b, i: (0, 0)),
                  pl.BlockSpec((BLOCK, SWA_WIDTH), cur),
                  pl.BlockSpec((BLOCK, kv_width), cur),
                  pl.BlockSpec((BLOCK, kv_width), prev),
                  pl.BlockSpec((None, SWA_KV_WIDTH, BLOCK), cur_t),
                  pl.BlockSpec((None, SWA_KV_WIDTH, BLOCK), prev_t),
                  pl.BlockSpec((BLOCK, SWA_WIDTH), cur)],
        out_specs=pl.BlockSpec((BLOCK, SWA_WIDTH), cur),
        out_shape=jax.ShapeDtypeStruct((tokens, SWA_WIDTH), BF16),
        compiler_params=pltpu.CompilerParams(
            dimension_semantics=("arbitrary", "arbitrary"), vmem_limit_bytes=VMEM_LIMIT_BYTES),
        name="swa_attn",
    )(sink_rows, q, kd, kd, vt, vt, g)


def _out_kernel(x_ref, y_ref, w_ref, fn_ref, o_ref, w_sc, *, final):
    @pl.when(pl.program_id(0) == 0)
    def _():
        w_sc[...] = w_ref[...].astype(BF16)

    y = _load_groups(y_ref) if len(y_ref.shape) == 3 else y_ref[...]
    x_new = x_ref[...] + _dot(y, w_sc[...])
    o_ref[...] = _rmsnorm(x_new, fn_ref[...]) if final else x_new


def _out_proj(x, y, w_out, j, final_norm, final):
    tokens = x.shape[0]
    rows = OUT_ROWS
    row_spec = lambda width: pl.BlockSpec((rows, width), lambda i: (i, 0))
    if y.ndim == 3:
        y_spec = pl.BlockSpec((y.shape[0], rows, y.shape[2]), lambda i: (0, i, 0))
    else:
        y_spec = row_spec(y.shape[1])
    return pl.pallas_call(
        functools.partial(_out_kernel, final=final),
        grid=(tokens // rows,),
        in_specs=[row_spec(D_MODEL), y_spec, _layer_spec(w_out, j),
                  _const_spec(final_norm.shape)],
        out_specs=row_spec(D_MODEL),
        out_shape=jax.ShapeDtypeStruct((tokens, D_MODEL), F32),
        scratch_shapes=[pltpu.VMEM(w_out.shape[1:], BF16)],
        compiler_params=pltpu.CompilerParams(
            dimension_semantics=("arbitrary",), vmem_limit_bytes=VMEM_LIMIT_BYTES),
        name="out_proj",
    )(x, y, w_out, final_norm)


def kernel(x, positions, layer_norm, mla_w_in, mla_q_norm, mla_w_uq, mla_kv_norm, mla_w_ukv,
           mla_w_out, swa_w_in, swa_sinks, swa_w_out, final_norm):
    batch, seq, d_model = x.shape
    tokens = batch * seq
    assert d_model == D_MODEL and seq % FLASH_BLOCK == 0 and tokens % OUT_ROWS == 0
    assert FLASH_BLOCK % MLA_PROJ_ROWS == 0 and SWA_PROJ_ROWS % BLOCK == 0
    mcos, msin, sc, s1, s2 = _rope_tables(positions)
    xt = x.reshape(tokens, d_model)
    fn = final_norm.reshape(1, -1)
    ln_stack = layer_norm[:, None, :]
    mla_w = {"w_in": mla_w_in.astype(BF16), "w_uq": mla_w_uq.astype(BF16),
             "w_ukv": mla_w_ukv.astype(BF16), "q_norm": mla_q_norm[:, None, :],
             "kv_norm": mla_kv_norm[:, None, :]}
    swa_w = swa_w_in.astype(BF16)
    for i in range(DEPTH):
        j = i // N_MIXERS
        final = i == DEPTH - 1
        if i % N_MIXERS == 0:
            qn, qr, kn, vt, krp, g = _mla_in(xt, ln_stack, i, mla_w, j, mcos, msin)
            y = _mla_flash(qn, qr, kn, krp, vt, g, batch, seq)
            w_out = mla_w_out
        else:
            q, kd, vt, g = _swa_in(xt, ln_stack, i, swa_w, j, sc, s1, s2)
            y = _swa_attn(swa_sinks[j], q, kd, vt, g, batch, seq)
            w_out = swa_w_out
        xt = _out_proj(xt, y, w_out, j, fn, final)
    return xt.reshape(batch, seq, d_model)
```

```python
import functools
import math

import jax
import jax.numpy as jnp
from jax import lax
from jax.experimental import pallas as pl
from jax.experimental.pallas import tpu as pltpu

D_MODEL = 2048
DEPTH = 4
N_MIXERS = 2
ROPE_THETA = 500000.0
NORM_EPS = 1e-6
BLOCK = 128
NEG_INF = -1e30

MLA_HEADS = 16
MLA_Q_RANK = 512
MLA_KV_RANK = 512
MLA_NOPE = 128
MLA_ROPE = 64
MLA_V = 128
MLA_WIDTH = MLA_HEADS * MLA_V

SWA_Q_HEADS = 32
SWA_KV_HEADS = 4
SWA_GROUP = SWA_Q_HEADS // SWA_KV_HEADS
SWA_HEAD_DIM = 64
SWA_ROPE_DIM = SWA_HEAD_DIM // 4
SWA_WIDTH = SWA_Q_HEADS * SWA_HEAD_DIM
SWA_KV_WIDTH = SWA_KV_HEADS * SWA_HEAD_DIM

LANES = 128
VMEM_LIMIT_BYTES = 56 * 1024 * 1024

MLA_PROJ_ROWS = 256
SWA_PROJ_ROWS = 256
WEIGHT_PREP_STEPS = 16
OUT_ROWS = 512
FLASH_BLOCK = 512

BF16 = jnp.bfloat16
F32 = jnp.float32


def _dot(a, b):
    return jnp.dot(a, b, preferred_element_type=F32)


def _dot_nt(a, b):
    return lax.dot_general(a, b, (((1,), (1,)), ((), ())), preferred_element_type=F32)


def _rmsnorm(xf, g):
    y = xf * lax.rsqrt(jnp.mean(xf * xf, axis=-1, keepdims=True) + NORM_EPS)
    return y * g


def _tile_lanes(t, width):
    reps = width // t.shape[1]
    return t if reps == 1 else jnp.concatenate([t] * reps, axis=1)


def _lane_index(shape):
    return lax.broadcasted_iota(jnp.int32, shape, 1)


def _rope_half64(x, cos_t, sin_t):
    w = x.shape[1]
    fwd = pltpu.roll(x, w - MLA_ROPE // 2, 1)
    bwd = pltpu.roll(x, MLA_ROPE // 2, 1)
    first_half = (_lane_index(x.shape) % MLA_ROPE) < (MLA_ROPE // 2)
    swapped = jnp.where(first_half, fwd, bwd)
    return x * _tile_lanes(cos_t, w) + swapped * _tile_lanes(sin_t, w)


def _rope_partial(x, c_t, s1_t, s2_t):
    w = x.shape[1]
    half = SWA_ROPE_DIM // 2
    fwd = pltpu.roll(x, w - half, 1)
    bwd = pltpu.roll(x, half, 1)
    return (x * _tile_lanes(c_t, w) + fwd * _tile_lanes(s1_t, w)
            + bwd * _tile_lanes(s2_t, w))


def _mla_rope_tables(pos_ref, const_ref):
    ang = pos_ref[...].astype(F32) * const_ref[0:1, :]
    return jnp.cos(ang), jnp.sin(ang) * const_ref[1:2, :]


def _swa_rope_tables(pos_ref, const_ref):
    ang = pos_ref[...].astype(F32) * const_ref[2:3, :]
    sin = jnp.sin(ang)
    return jnp.cos(ang), sin * const_ref[3:4, :], sin * const_ref[4:5, :]


def _rope_constants():
    lane = jnp.arange(LANES)
    f_mla = ROPE_THETA ** (-jnp.arange(0, MLA_ROPE, 2, dtype=F32) / MLA_ROPE)
    f_swa = ROPE_THETA ** (-jnp.arange(0, SWA_ROPE_DIM, 2, dtype=F32) / SWA_ROPE_DIM)
    half_m = MLA_ROPE // 2
    half_s = SWA_ROPE_DIM // 2
    d_m = lane % MLA_ROPE
    d_s = lane % SWA_HEAD_DIM
    rows = [
        f_mla[d_m % half_m],
        jnp.where(d_m < half_m, -1.0, 1.0),
        jnp.where(d_s < SWA_ROPE_DIM, f_swa[d_s % half_s], 0.0),
        jnp.where(d_s < half_s, -1.0, 0.0),
        jnp.where((d_s >= half_s) & (d_s < SWA_ROPE_DIM), 1.0, 0.0),
    ]
    rows += [jnp.zeros((LANES,), F32)] * 3
    return jnp.stack([r.astype(F32) for r in rows])


def _mla_in_kernel(x_ref, ln_ref, win_chunk, qnorm_ref, kvnorm_ref, wuq_chunk, wukv_chunk,
                   pos_ref, rope_ref, qn_ref, qr_ref, kn_ref, vt_ref, krp_ref, g_ref,
                   win_sc, wuq_sc, wukv_sc):
    step = pl.program_id(0)

    @pl.when(step < WEIGHT_PREP_STEPS)
    def _():
        _cast_chunk(step, win_chunk, win_sc)
        _cast_chunk(step, wuq_chunk, wuq_sc)
        _cast_chunk(step, wukv_chunk, wukv_sc)

    @pl.when(step >= WEIGHT_PREP_STEPS)
    def _():
        _mla_in_body(x_ref, ln_ref, win_sc, qnorm_ref, kvnorm_ref, wuq_sc, wukv_sc,
                     pos_ref, rope_ref, qn_ref, qr_ref, kn_ref, vt_ref, krp_ref, g_ref)


def _mla_in_body(x_ref, ln_ref, win_ref, qnorm_ref, kvnorm_ref, wuq_ref, wukv_ref,
                 pos_ref, rope_ref, qn_ref, qr_ref, kn_ref, vt_ref, krp_ref, g_ref):
    scale = (MLA_NOPE + MLA_ROPE) ** -0.5 * math.log2(math.e)
    h = _rmsnorm(x_ref[...], ln_ref[...]).astype(BF16)
    cos_t, sin_t = _mla_rope_tables(pos_ref, rope_ref)
    rank2 = MLA_Q_RANK + MLA_KV_RANK
    heads = range(MLA_HEADS)

    c = _dot(h, win_ref[:, :rank2])
    cq = _rmsnorm(c[:, :MLA_Q_RANK], qnorm_ref[...]).astype(BF16)
    ckv = _rmsnorm(c[:, MLA_Q_RANK:], kvnorm_ref[...]).astype(BF16)

    q = _dot(cq, wuq_ref[...])
    qd = MLA_NOPE + MLA_ROPE
    qn = jnp.concatenate([q[:, hh * qd:hh * qd + MLA_NOPE] for hh in heads], axis=1)
    qr = jnp.concatenate([q[:, hh * qd + MLA_NOPE:(hh + 1) * qd] for hh in heads], axis=1)
    _store_groups(qn_ref, (qn * scale).astype(BF16))
    _store_groups(qr_ref, (_rope_half64(qr, cos_t, sin_t) * scale).astype(BF16))

    kv = _dot(ckv, wukv_ref[...])
    kd = MLA_NOPE + MLA_V
    kn = jnp.concatenate([kv[:, hh * kd:hh * kd + MLA_NOPE] for hh in heads], axis=1)
    v = jnp.concatenate([kv[:, hh * kd + MLA_NOPE:(hh + 1) * kd] for hh in heads], axis=1)
    _store_groups(kn_ref, kn.astype(BF16))
    vt_ref[0] = v.T.astype(BF16)

    tail = _dot(h, win_ref[:, rank2:])
    kr = tail[:, :MLA_ROPE]
    kr = _rope_half64(jnp.concatenate([kr, kr], axis=1), cos_t, sin_t)
    low = _lane_index(kr.shape) < MLA_ROPE
    krp_ref[:, :LANES] = jnp.where(low, kr, 0.0).astype(BF16)
    krp_ref[:, LANES:] = jnp.where(low, 0.0, kr).astype(BF16)

    gate = tail[:, MLA_ROPE:]
    _store_groups(g_ref, (gate * jax.nn.sigmoid(gate)).astype(BF16))


def _store_groups(ref, val):
    groups, _, width = ref.shape
    for gg in range(groups):
        ref[gg] = val[:, gg * width:(gg + 1) * width]


def _load_groups(ref):
    return jnp.concatenate([ref[gg] for gg in range(ref.shape[0])], axis=1)


def _const_spec(shape):
    return pl.BlockSpec(shape, lambda i: (0,) * len(shape), pipeline_mode=pl.Buffered(1))


def _layer_spec(stacked, layer):
    tail = (0,) * (stacked.ndim - 1)
    return pl.BlockSpec((None,) + stacked.shape[1:], lambda i: (layer,) + tail,
                        pipeline_mode=pl.Buffered(1))


def _chunk_spec(stacked, layer):
    _, nrows, ncols = stacked.shape
    last = WEIGHT_PREP_STEPS - 1
    return pl.BlockSpec((None, nrows // WEIGHT_PREP_STEPS, ncols),
                        lambda s: (layer, jnp.minimum(s, last), 0))


def _cast_chunk(step, chunk_ref, w_sc):
    nrows = chunk_ref.shape[0]
    w_sc[pl.ds(pl.multiple_of(step * nrows, nrows), nrows), :] = chunk_ref[...].astype(BF16)


def _tile_index(step):
    return jnp.maximum(step - WEIGHT_PREP_STEPS, 0)


def _mla_in(x, layer_norm, layer, w, j, pos, rope_consts):
    tokens = x.shape[0]
    rows = MLA_PROJ_ROWS
    row_spec = lambda width: pl.BlockSpec((rows, width), lambda s: (_tile_index(s), 0))
    pairs = MLA_HEADS // 2
    per_blk = FLASH_BLOCK // rows

    def pair_major(width):
        return (pl.BlockSpec((pairs, rows, width), lambda s: (0, _tile_index(s), 0)),
                jax.ShapeDtypeStruct((pairs, tokens, width), BF16))

    outs = [pair_major(2 * MLA_NOPE),
            pair_major(2 * MLA_ROPE),
            pair_major(2 * MLA_NOPE),
            (pl.BlockSpec((1, MLA_WIDTH, rows),
                          lambda s: (_tile_index(s) // per_blk, 0, _tile_index(s) % per_blk)),
             jax.ShapeDtypeStruct((tokens // FLASH_BLOCK, MLA_WIDTH, FLASH_BLOCK), BF16)),
            (row_spec(2 * LANES), jax.ShapeDtypeStruct((tokens, 2 * LANES), BF16)),
            pair_major(2 * MLA_V)]
    return pl.pallas_call(
        _mla_in_kernel,
        grid=(WEIGHT_PREP_STEPS + tokens // rows,),
        in_specs=[row_spec(D_MODEL), _layer_spec(layer_norm, layer), _chunk_spec(w["w_in"], j),
                  _layer_spec(w["q_norm"], j), _layer_spec(w["kv_norm"], j),
                  _chunk_spec(w["w_uq"], j), _chunk_spec(w["w_ukv"], j),
                  row_spec(1), _const_spec(rope_consts.shape)],
        out_specs=[o[0] for o in outs],
        out_shape=[o[1] for o in outs],
        scratch_shapes=[pltpu.VMEM(w[name].shape[1:], BF16) for name in ("w_in", "w_uq", "w_ukv")],
        compiler_params=pltpu.CompilerParams(
            dimension_semantics=("arbitrary",), vmem_limit_bytes=VMEM_LIMIT_BYTES),
        name="mla_in",
    )(x, layer_norm, w["w_in"], w["q_norm"], w["kv_norm"], w["w_uq"], w["w_ukv"],
      pos, rope_consts)


def _mla_flash_kernel(qn_ref, qr_ref, kn_ref, kr_ref, vt_ref, g_ref, o_ref,
                      kcat_sc, s_sc, m_sc, l_sc, acc_sc, *, seq):
    blk = FLASH_BLOCK
    pair = (0, 1)
    hcols = [slice(hh * LANES, (hh + 1) * LANES) for hh in pair]
    for hh in pair:
        kcat_sc[hh, :, :LANES] = kn_ref[:, hcols[hh]]
        kcat_sc[hh, :, LANES:] = kr_ref[:, hcols[hh]]
    causal_t = (lax.broadcasted_iota(jnp.int32, (blk, blk), 0)
                <= lax.broadcasted_iota(jnp.int32, (blk, blk), 1))

    for qi in range(seq // blk):
        rows = slice(qi * blk, (qi + 1) * blk)
        qs = [jnp.concatenate([qn_ref[rows, hcols[hh]], qr_ref[rows, :]], axis=1) for hh in pair]
        m_sc[...] = jnp.full(m_sc.shape, NEG_INF, F32)
        l_sc[...] = jnp.zeros(l_sc.shape, F32)
        acc_sc[...] = jnp.zeros(acc_sc.shape, F32)

        def scores(hh, kb, slot, masked, qs=qs):
            kstart = pl.multiple_of(kb * blk, blk)
            st = _dot_nt(kcat_sc[hh, pl.ds(kstart, blk), :], qs[hh])
            s_sc[hh, slot] = jnp.where(causal_t, st, NEG_INF) if masked else st

        def accumulate(hh, kb, slot):
            st = s_sc[hh, slot]
            m_prev = m_sc[hh]
            m_new = jnp.maximum(m_prev, jnp.max(st, axis=0, keepdims=True))
            alpha = jnp.exp2(m_prev - m_new)
            p = jnp.exp2(st - m_new)
            l_sc[hh] = alpha * l_sc[hh] + jnp.sum(p, axis=0, keepdims=True)
            acc_sc[hh] = alpha * acc_sc[hh] + _dot(vt_ref[kb, hcols[hh], :], p.astype(BF16))
            m_sc[hh] = m_new

        visits = [qi] + list(range(qi))
        for hh in pair:
            scores(hh, visits[0], 0, True)
        for i, kb in enumerate(visits):
            for hh in pair:
                if i + 1 < len(visits):
                    scores(hh, visits[i + 1], (i + 1) % 2, False)
                accumulate(hh, kb, i % 2)

        for hh in pair:
            y = (acc_sc[hh] * (1.0 / l_sc[hh])).T
            o_ref[rows, hcols[hh]] = (y * g_ref[rows, hcols[hh]].astype(F32)).astype(BF16)


def _mla_flash(qn, qr, kn, krp, vt, g, batch, seq):
    tokens = batch * seq
    nblk = seq // FLASH_BLOCK
    pair_spec = pl.BlockSpec((None, seq, 2 * LANES), lambda b, p: (p, b, 0))
    return pl.pallas_call(
        functools.partial(_mla_flash_kernel, seq=seq),
        grid=(batch, MLA_HEADS // 2),
        in_specs=[pair_spec,
                  pl.BlockSpec((None, seq, LANES), lambda b, p: (p, b, 0)),
                  pair_spec,
                  pl.BlockSpec((seq, 2 * LANES), lambda b, p: (b, 0)),
                  pl.BlockSpec((nblk, 2 * LANES, FLASH_BLOCK), lambda b, p: (b, p, 0)),
                  pair_spec],
        out_specs=pair_spec,
        out_shape=jax.ShapeDtypeStruct((MLA_HEADS // 2, tokens, 2 * MLA_V), BF16),
        scratch_shapes=[pltpu.VMEM((2, seq, 2 * LANES), BF16),
                        pltpu.VMEM((2, 2, FLASH_BLOCK, FLASH_BLOCK), F32),
                        pltpu.VMEM((2, 1, FLASH_BLOCK), F32),
                        pltpu.VMEM((2, 1, FLASH_BLOCK), F32),
                        pltpu.VMEM((2, MLA_V, FLASH_BLOCK), F32)],
        compiler_params=pltpu.CompilerParams(
            dimension_semantics=("arbitrary", "arbitrary"), vmem_limit_bytes=VMEM_LIMIT_BYTES),
        name="mla_flash",
    )(qn, qr, kn, krp, vt, g)


def _dup_heads(t):
    parts = []
    for hh in range(t.shape[1] // SWA_HEAD_DIM):
        piece = t[:, hh * SWA_HEAD_DIM:(hh + 1) * SWA_HEAD_DIM]
        parts += [piece, piece]
    return jnp.concatenate(parts, axis=1)


def _swa_in_kernel(x_ref, ln_ref, win_chunk, pos_ref, rope_ref,
                   q_ref, kd_ref, vt_ref, g_ref, win_sc):
    step = pl.program_id(0)

    @pl.when(step < WEIGHT_PREP_STEPS)
    def _():
        _cast_chunk(step, win_chunk, win_sc)

    @pl.when(step >= WEIGHT_PREP_STEPS)
    def _():
        _swa_in_body(x_ref, ln_ref, win_sc, pos_ref, rope_ref, q_ref, kd_ref, vt_ref, g_ref)


def _swa_in_body(x_ref, ln_ref, win_ref, pos_ref, rope_ref, q_ref, kd_ref, vt_ref, g_ref):
    scale = SWA_HEAD_DIM ** -0.5 * math.log2(math.e)
    o1 = SWA_WIDTH
    o2 = o1 + SWA_KV_WIDTH
    o3 = o2 + SWA_KV_WIDTH
    h = _rmsnorm(x_ref[...], ln_ref[...]).astype(BF16)
    c_t, s1_t, s2_t = _swa_rope_tables(pos_ref, rope_ref)
    q = _rope_partial(_dot(h, win_ref[:, :o1]), c_t, s1_t, s2_t)
    q_ref[...] = (q * scale).astype(BF16)
    kv = _dot(h, win_ref[:, o1:o3])
    k = _rope_partial(kv[:, :SWA_KV_WIDTH], c_t, s1_t, s2_t)
    kd_ref[...] = _dup_heads(k).astype(BF16)
    vt = kv[:, SWA_KV_WIDTH:].T.astype(BF16)
    for cc in range(vt_ref.shape[0]):
        vt_ref[cc] = vt[:, cc * BLOCK:(cc + 1) * BLOCK]
    gate = _dot(h, win_ref[:, o3:])
    g_ref[...] = (gate * jax.nn.sigmoid(gate)).astype(BF16)


def _swa_in(x, layer_norm, layer, w_in, j, pos, rope_consts):
    tokens = x.shape[0]
    rows = SWA_PROJ_ROWS
    row_spec = lambda width: pl.BlockSpec((rows, width), lambda s: (_tile_index(s), 0))
    out_widths = [SWA_WIDTH, SWA_KV_HEADS * LANES, None, SWA_WIDTH]
    vt_spec = pl.BlockSpec((rows // BLOCK, SWA_KV_WIDTH, BLOCK),
                           lambda s: (_tile_index(s), 0, 0))
    vt_shape = jax.ShapeDtypeStruct((tokens // BLOCK, SWA_KV_WIDTH, BLOCK), BF16)
    return pl.pallas_call(
        _swa_in_kernel,
        grid=(WEIGHT_PREP_STEPS + tokens // rows,),
        in_specs=[row_spec(D_MODEL), _layer_spec(layer_norm, layer), _chunk_spec(w_in, j),
                  row_spec(1), _const_spec(rope_consts.shape)],
        out_specs=[vt_spec if wd is None else row_spec(wd) for wd in out_widths],
        out_shape=[vt_shape if wd is None else jax.ShapeDtypeStruct((tokens, wd), BF16)
                   for wd in out_widths],
        scratch_shapes=[pltpu.VMEM(w_in.shape[1:], BF16)],
        compiler_params=pltpu.CompilerParams(
            dimension_semantics=("arbitrary",), vmem_limit_bytes=VMEM_LIMIT_BYTES),
        name="swa_in",
    )(x, layer_norm, w_in, pos, rope_consts)


def _swa_attn_kernel(sink_ref, q_ref, kc_ref, kp_ref, vtc_ref, vtp_ref, g_ref, o_ref):
    has_prev = pl.program_id(1) > 0
    pairs = SWA_GROUP // 2
    key_i = lax.broadcasted_iota(jnp.int32, (BLOCK, pairs * BLOCK), 0)
    qry_i = lax.broadcasted_iota(jnp.int32, (BLOCK, pairs * BLOCK), 1) % BLOCK
    in_cur = key_i <= qry_i
    cur_w = jnp.where(in_cur, 1.0, 0.0).astype(BF16)
    low = _lane_index((BLOCK, LANES)) < SWA_HEAD_DIM
    zero = jnp.zeros((BLOCK, LANES), BF16)

    def scores(kvh, sub):
        cols = slice(kvh * LANES, (kvh + 1) * LANES)
        kc, kp = kc_ref[:, cols], kp_ref[:, cols]
        if sub == 0:
            k_half = jnp.concatenate([jnp.where(low, kp, zero), jnp.where(low, kc, zero)], axis=0)
        else:
            k_half = jnp.concatenate([jnp.where(low, zero, kp), jnp.where(low, zero, kc)], axis=0)
        head0 = kvh * SWA_GROUP
        q_rows = jnp.concatenate(
            [q_ref[:, (head0 + 2 * pp) * SWA_HEAD_DIM:(head0 + 2 * pp + 2) * SWA_HEAD_DIM]
             for pp in range(pairs)], axis=0)
        return _dot_nt(k_half, q_rows)

    def attend(kvh, sub, st):
        s_prev = jnp.where(has_prev, st[:BLOCK], NEG_INF)
        s = jnp.where(in_cur, st[BLOCK:], s_prev)
        sink = sink_ref[sub * SWA_KV_HEADS + kvh:sub * SWA_KV_HEADS + kvh + 1, :]
        m = jnp.maximum(jnp.max(s, axis=0, keepdims=True), sink)
        e = jnp.exp2(s - m)
        denom = jnp.sum(e, axis=0, keepdims=True) + jnp.exp2(sink - m)
        p = e.astype(BF16)
        p_cur = p * cur_w
        pt = jnp.concatenate([p - p_cur, p_cur], axis=0)
        vrows = slice(kvh * SWA_HEAD_DIM, (kvh + 1) * SWA_HEAD_DIM)
        vt = jnp.concatenate([vtp_ref[vrows, :], vtc_ref[vrows, :]], axis=1)
        return _dot(vt, pt) * (1.0 / denom)

    chains = [(kvh, sub) for kvh in range(SWA_KV_HEADS) for sub in range(2)]
    st_next = scores(*chains[0])
    outs = {}
    for ci, (kvh, sub) in enumerate(chains):
        st = st_next
        if ci + 1 < len(chains):
            st_next = scores(*chains[ci + 1])
        outs[sub] = attend(kvh, sub, st)
        if sub == 1:
            head0 = kvh * SWA_GROUP
            for pp in range(pairs):
                qc = slice(pp * BLOCK, (pp + 1) * BLOCK)
                y = jnp.concatenate([outs[0][:, qc], outs[1][:, qc]], axis=0).T
                pcols = slice((head0 + 2 * pp) * SWA_HEAD_DIM,
                              (head0 + 2 * pp + 2) * SWA_HEAD_DIM)
                o_ref[:, pcols] = (y * g_ref[:, pcols].astype(F32)).astype(BF16)


def _swa_sink_rows(sinks):
    s2 = (sinks * math.log2(math.e)).reshape(SWA_KV_HEADS, SWA_GROUP // 2, 2)
    rows = jnp.transpose(s2, (2, 0, 1)).reshape(2 * SWA_KV_HEADS, SWA_GROUP // 2, 1)
    return jnp.broadcast_to(rows, (2 * SWA_KV_HEADS, SWA_GROUP // 2, BLOCK)).reshape(
        2 * SWA_KV_HEADS, -1)


def _swa_attn(sinks, q, kd, vt, g, batch, seq):
    tokens = batch * seq
    nb = seq // BLOCK
    cur = lambda b, i: (b * nb + i, 0)
    prev = lambda b, i: (b * nb + jnp.maximum(i - 1, 0), 0)
    cur_t = lambda b, i: (b * nb + i, 0, 0)
    prev_t = lambda b, i: (b * nb + jnp.maximum(i - 1, 0), 0, 0)
    kv_width = SWA_KV_HEADS * LANES
    sink_rows = _swa_sink_rows(sinks)
    return pl.pallas_call(
        _swa_attn_kernel,
        grid=(batch, nb),
        in_specs=[pl.BlockSpec(sink_rows.shape, lambda b, i: (0, 0)),
                  pl.BlockSpec((BLOCK, SWA_WIDTH), cur),
                  pl.BlockSpec((BLOCK, kv_width), cur),
                  pl.BlockSpec((BLOCK, kv_width), prev),
                  pl.BlockSpec((None, SWA_KV_WIDTH, BLOCK), cur_t),
                  pl.BlockSpec((None, SWA_KV_WIDTH, BLOCK), prev_t),
                  pl.BlockSpec((BLOCK, SWA_WIDTH), cur)],
        out_specs=pl.BlockSpec((BLOCK, SWA_WIDTH), cur),
        out_shape=jax.ShapeDtypeStruct((tokens, SWA_WIDTH), BF16),
        compiler_params=pltpu.CompilerParams(
            dimension_semantics=("arbitrary", "arbitrary"), vmem_limit_bytes=VMEM_LIMIT_BYTES),
        name="swa_attn",
    )(sink_rows, q, kd, kd, vt, vt, g)


def _out_kernel(x_ref, y_ref, w_ref, fn_ref, o_ref, w_sc, *, final):
    @pl.when(pl.program_id(0) == 0)
    def _():
        w_sc[...] = w_ref[...].astype(BF16)

    y = _load_groups(y_ref) if len(y_ref.shape) == 3 else y_ref[...]
    x_new = x_ref[...] + _dot(y, w_sc[...])
    o_ref[...] = _rmsnorm(x_new, fn_ref[...]) if final else x_new


def _out_proj(x, y, w_out, j, final_norm, final):
    tokens = x.shape[0]
    rows = OUT_ROWS
    row_spec = lambda width: pl.BlockSpec((rows, width), lambda i: (i, 0))
    if y.ndim == 3:
        y_spec = pl.BlockSpec((y.shape[0], rows, y.shape[2]), lambda i: (0, i, 0))
    else:
        y_spec = row_spec(y.shape[1])
    return pl.pallas_call(
        functools.partial(_out_kernel, final=final),
        grid=(tokens // rows,),
        in_specs=[row_spec(D_MODEL), y_spec, _layer_spec(w_out, j),
                  _const_spec(final_norm.shape)],
        out_specs=row_spec(D_MODEL),
        out_shape=jax.ShapeDtypeStruct((tokens, D_MODEL), F32),
        scratch_shapes=[pltpu.VMEM(w_out.shape[1:], BF16)],
        compiler_params=pltpu.CompilerParams(
            dimension_semantics=("arbitrary",), vmem_limit_bytes=VMEM_LIMIT_BYTES),
        name="out_proj",
    )(x, y, w_out, final_norm)


def kernel(x, positions, layer_norm, mla_w_in, mla_q_norm, mla_w_uq, mla_kv_norm, mla_w_ukv,
           mla_w_out, swa_w_in, swa_sinks, swa_w_out, final_norm):
    batch, seq, d_model = x.shape
    tokens = batch * seq
    assert d_model == D_MODEL and seq % FLASH_BLOCK == 0 and tokens % OUT_ROWS == 0
    assert FLASH_BLOCK % MLA_PROJ_ROWS == 0 and SWA_PROJ_ROWS % BLOCK == 0
    pos = positions.reshape(tokens, 1)
    rope_consts = _rope_constants()
    xt = x.reshape(tokens, d_model)
    fn = final_norm.reshape(1, -1)
    ln_stack = layer_norm[:, None, :]
    mla_w = {"w_in": mla_w_in, "w_uq": mla_w_uq, "w_ukv": mla_w_ukv,
             "q_norm": mla_q_norm[:, None, :], "kv_norm": mla_kv_norm[:, None, :]}
    swa_w = swa_w_in
    for i in range(DEPTH):
        j = i // N_MIXERS
        final = i == DEPTH - 1
        if i % N_MIXERS == 0:
            qn, qr, kn, vt, krp, g = _mla_in(xt, ln_stack, i, mla_w, j, pos, rope_consts)
            y = _mla_flash(qn, qr, kn, krp, vt, g, batch, seq)
            w_out = mla_w_out
        else:
            q, kd, vt, g = _swa_in(xt, ln_stack, i, swa_w, j, pos, rope_consts)
            y = _swa_attn(swa_sinks[j], q, kd, vt, g, batch, seq)
            w_out = swa_w_out
        xt = _out_proj(xt, y, w_out, j, fn, final)
    return xt.reshape(batch, seq, d_model)
```

```python
import functools
import math

import jax
import jax.numpy as jnp
from jax import lax
from jax.experimental import pallas as pl
from jax.experimental.pallas import tpu as pltpu

D_MODEL = 2048
DEPTH = 4
N_MIXERS = 2
ROPE_THETA = 500000.0
NORM_EPS = 1e-6
BLOCK = 128
NEG_INF = -1e30

MLA_HEADS = 16
MLA_Q_RANK = 512
MLA_KV_RANK = 512
MLA_NOPE = 128
MLA_ROPE = 64
MLA_V = 128
MLA_WIDTH = MLA_HEADS * MLA_V

SWA_Q_HEADS = 32
SWA_KV_HEADS = 4
SWA_GROUP = SWA_Q_HEADS // SWA_KV_HEADS
SWA_HEAD_DIM = 64
SWA_ROPE_DIM = SWA_HEAD_DIM // 4
SWA_WIDTH = SWA_Q_HEADS * SWA_HEAD_DIM
SWA_KV_WIDTH = SWA_KV_HEADS * SWA_HEAD_DIM

LANES = 128
VMEM_LIMIT_BYTES = 56 * 1024 * 1024

MLA_PROJ_ROWS = 256
SWA_PROJ_ROWS = 512
OUT_ROWS = 512
FLASH_BLOCK = 512

BF16 = jnp.bfloat16
F32 = jnp.float32


def _dot(a, b):
    return jnp.dot(a, b, preferred_element_type=F32)


def _dot_nt(a, b):
    return lax.dot_general(a, b, (((1,), (1,)), ((), ())), preferred_element_type=F32)


def _rmsnorm(xf, g):
    y = xf * lax.rsqrt(jnp.mean(xf * xf, axis=-1, keepdims=True) + NORM_EPS)
    return y * g


def _tile_lanes(t, width):
    reps = width // t.shape[1]
    return t if reps == 1 else jnp.concatenate([t] * reps, axis=1)


def _lane_index(shape):
    return lax.broadcasted_iota(jnp.int32, shape, 1)


def _rope_half64(x, cos_t, sin_t):
    w = x.shape[1]
    fwd = pltpu.roll(x, w - MLA_ROPE // 2, 1)
    bwd = pltpu.roll(x, MLA_ROPE // 2, 1)
    first_half = (_lane_index(x.shape) % MLA_ROPE) < (MLA_ROPE // 2)
    swapped = jnp.where(first_half, fwd, bwd)
    return x * _tile_lanes(cos_t, w) + swapped * _tile_lanes(sin_t, w)


def _rope_partial(x, c_t, s1_t, s2_t):
    w = x.shape[1]
    half = SWA_ROPE_DIM // 2
    fwd = pltpu.roll(x, w - half, 1)
    bwd = pltpu.roll(x, half, 1)
    return (x * _tile_lanes(c_t, w) + fwd * _tile_lanes(s1_t, w)
            + bwd * _tile_lanes(s2_t, w))


def _rope_table_kernel(pos_ref, const_ref, mcos_ref, msin_ref, sc_ref, s1_ref, s2_ref):
    pos = pos_ref[...].astype(F32)
    ang_m = pos * const_ref[0:1, :]
    mcos_ref[...] = jnp.cos(ang_m)
    msin_ref[...] = jnp.sin(ang_m) * const_ref[1:2, :]
    ang_s = pos * const_ref[2:3, :]
    sin_s = jnp.sin(ang_s)
    sc_ref[...] = jnp.cos(ang_s)
    s1_ref[...] = sin_s * const_ref[3:4, :]
    s2_ref[...] = sin_s * const_ref[4:5, :]


def _rope_constants():
    lane = jnp.arange(LANES)
    f_mla = ROPE_THETA ** (-jnp.arange(0, MLA_ROPE, 2, dtype=F32) / MLA_ROPE)
    f_swa = ROPE_THETA ** (-jnp.arange(0, SWA_ROPE_DIM, 2, dtype=F32) / SWA_ROPE_DIM)
    half_m = MLA_ROPE // 2
    half_s = SWA_ROPE_DIM // 2
    d_m = lane % MLA_ROPE
    d_s = lane % SWA_HEAD_DIM
    rows = [
        f_mla[d_m % half_m],
        jnp.where(d_m < half_m, -1.0, 1.0),
        jnp.where(d_s < SWA_ROPE_DIM, f_swa[d_s % half_s], 0.0),
        jnp.where(d_s < half_s, -1.0, 0.0),
        jnp.where((d_s >= half_s) & (d_s < SWA_ROPE_DIM), 1.0, 0.0),
    ]
    rows += [jnp.zeros((LANES,), F32)] * 3
    return jnp.stack([r.astype(F32) for r in rows])


def _rope_tables(positions):
    tokens = positions.size
    rows = 1024
    tab = jax.ShapeDtypeStruct((tokens, LANES), F32)
    spec = pl.BlockSpec((rows, LANES), lambda i: (i, 0))
    return pl.pallas_call(
        _rope_table_kernel,
        grid=(tokens // rows,),
        in_specs=[pl.BlockSpec((rows, 1), lambda i: (i, 0)),
                  pl.BlockSpec((8, LANES), lambda i: (0, 0))],
        out_specs=[spec] * 5,
        out_shape=[tab] * 5,
        name="rope_tables",
    )(positions.reshape(tokens, 1), _rope_constants())


def _mla_in_kernel(x_ref, ln_ref, win_ref, qnorm_ref, kvnorm_ref, wuq_ref, wukv_ref,
                   cos_ref, sin_ref, qn_ref, qr_ref, kn_ref, vt_ref, krp_ref, g_ref):
    scale = (MLA_NOPE + MLA_ROPE) ** -0.5 * math.log2(math.e)
    h = _rmsnorm(x_ref[...], ln_ref[...]).astype(BF16)
    cos_t = cos_ref[...]
    sin_t = sin_ref[...]
    rank2 = MLA_Q_RANK + MLA_KV_RANK
    heads = range(MLA_HEADS)

    c = _dot(h, win_ref[:, :rank2])
    cq = _rmsnorm(c[:, :MLA_Q_RANK], qnorm_ref[...]).astype(BF16)
    ckv = _rmsnorm(c[:, MLA_Q_RANK:], kvnorm_ref[...]).astype(BF16)

    tail = _dot(h, win_ref[:, rank2:])
    kr = tail[:, :MLA_ROPE]
    kr = _rope_half64(jnp.concatenate([kr, kr], axis=1), cos_t, sin_t)
    low = _lane_index(kr.shape) < MLA_ROPE
    krp_ref[:, :LANES] = jnp.where(low, kr, 0.0).astype(BF16)
    krp_ref[:, LANES:] = jnp.where(low, 0.0, kr).astype(BF16)

    gate = tail[:, MLA_ROPE:]
    _store_groups(g_ref, (gate * jax.nn.sigmoid(gate)).astype(BF16))

    q = _dot(cq, wuq_ref[...])
    qd = MLA_NOPE + MLA_ROPE
    qn = jnp.concatenate([q[:, hh * qd:hh * qd + MLA_NOPE] for hh in heads], axis=1)
    qr = jnp.concatenate([q[:, hh * qd + MLA_NOPE:(hh + 1) * qd] for hh in heads], axis=1)
    _store_groups(qn_ref, (qn * scale).astype(BF16))
    _store_groups(qr_ref, (_rope_half64(qr, cos_t, sin_t) * scale).astype(BF16))

    kv = _dot(ckv, wukv_ref[...])
    kd = MLA_NOPE + MLA_V
    kn = jnp.concatenate([kv[:, hh * kd:hh * kd + MLA_NOPE] for hh in heads], axis=1)
    v = jnp.concatenate([kv[:, hh * kd + MLA_NOPE:(hh + 1) * kd] for hh in heads], axis=1)
    _store_groups(kn_ref, kn.astype(BF16))
    vt_ref[0] = v.T.astype(BF16)


def _store_groups(ref, val):
    groups, _, width = ref.shape
    for gg in range(groups):
        ref[gg] = val[:, gg * width:(gg + 1) * width]


def _load_groups(ref):
    return jnp.concatenate([ref[gg] for gg in range(ref.shape[0])], axis=1)


def _const_spec(shape):
    return pl.BlockSpec(shape, lambda i: (0,) * len(shape), pipeline_mode=pl.Buffered(1))


def _layer_spec(stacked, layer):
    tail = (0,) * (stacked.ndim - 1)
    return pl.BlockSpec((None,) + stacked.shape[1:], lambda i: (layer,) + tail,
                        pipeline_mode=pl.Buffered(1))


def _mla_in(x, layer_norm, layer, w, j, cos_t, sin_t):
    tokens = x.shape[0]
    rows = MLA_PROJ_ROWS
    row_spec = lambda width: pl.BlockSpec((rows, width), lambda i: (i, 0))
    consts = [(layer_norm, layer), (w["w_in"], j), (w["q_norm"], j), (w["kv_norm"], j),
              (w["w_uq"], j), (w["w_ukv"], j)]
    pairs = MLA_HEADS // 2
    per_blk = FLASH_BLOCK // rows

    def pair_major(width):
        return (pl.BlockSpec((pairs, rows, width), lambda i: (0, i, 0)),
                jax.ShapeDtypeStruct((pairs, tokens, width), BF16))

    outs = [pair_major(2 * MLA_NOPE),
            pair_major(2 * MLA_ROPE),
            pair_major(2 * MLA_NOPE),
            (pl.BlockSpec((1, MLA_WIDTH, rows), lambda i: (i // per_blk, 0, i % per_blk)),
             jax.ShapeDtypeStruct((tokens // FLASH_BLOCK, MLA_WIDTH, FLASH_BLOCK), BF16)),
            (row_spec(2 * LANES), jax.ShapeDtypeStruct((tokens, 2 * LANES), BF16)),
            pair_major(2 * MLA_V)]
    return pl.pallas_call(
        _mla_in_kernel,
        grid=(tokens // rows,),
        in_specs=[row_spec(D_MODEL)] + [_layer_spec(c, ll) for c, ll in consts]
                 + [row_spec(LANES), row_spec(LANES)],
        out_specs=[o[0] for o in outs],
        out_shape=[o[1] for o in outs],
        compiler_params=pltpu.CompilerParams(
            dimension_semantics=("arbitrary",), vmem_limit_bytes=VMEM_LIMIT_BYTES),
        name="mla_in",
    )(x, *[c for c, _ in consts], cos_t, sin_t)


def _mla_flash_kernel(qn_ref, qr_ref, kn_ref, kr_ref, vt_ref, g_ref, o_ref,
                      kcat_sc, s_sc, m_sc, acc_sc, *, seq):
    blk = FLASH_BLOCK
    pair = (0, 1)
    hcols = [slice(hh * LANES, (hh + 1) * LANES) for hh in pair]
    for hh in pair:
        kcat_sc[hh, :, :LANES] = kn_ref[:, hcols[hh]]
        kcat_sc[hh, :, LANES:] = kr_ref[:, hcols[hh]]
    ones_rows = jnp.ones((16, blk), BF16)
    causal_t = (lax.broadcasted_iota(jnp.int32, (blk, blk), 0)
                <= lax.broadcasted_iota(jnp.int32, (blk, blk), 1))

    for qi in range(seq // blk):
        rows = slice(qi * blk, (qi + 1) * blk)
        qs = [jnp.concatenate([qn_ref[rows, hcols[hh]], qr_ref[rows, :]], axis=1) for hh in pair]
        m_sc[...] = jnp.full(m_sc.shape, NEG_INF, F32)
        acc_sc[...] = jnp.zeros(acc_sc.shape, F32)

        def scores(hh, kb, slot, masked, qs=qs):
            kstart = pl.multiple_of(kb * blk, blk)
            st = _dot_nt(kcat_sc[hh, pl.ds(kstart, blk), :], qs[hh])
            s_sc[hh, slot] = jnp.where(causal_t, st, NEG_INF) if masked else st

        def accumulate(hh, kb, slot):
            st = s_sc[hh, slot]
            m_prev = m_sc[hh]
            m_new = jnp.maximum(m_prev, jnp.max(st, axis=0, keepdims=True))
            alpha = jnp.exp2(m_prev - m_new)
            p = jnp.exp2(st - m_new)
            vt_aug = jnp.concatenate([vt_ref[kb, hcols[hh], :], ones_rows], axis=0)
            acc_sc[hh] = alpha * acc_sc[hh] + _dot(vt_aug, p.astype(BF16))
            m_sc[hh] = m_new

        visits = [qi] + list(range(qi))
        for hh in pair:
            scores(hh, visits[0], 0, True)
        for i, kb in enumerate(visits):
            for hh in pair:
                if i + 1 < len(visits):
                    scores(hh, visits[i + 1], (i + 1) % 2, False)
                accumulate(hh, kb, i % 2)

        for hh in pair:
            acc = acc_sc[hh]
            y = (acc[:MLA_V] * (1.0 / acc[MLA_V:MLA_V + 1])).T
            o_ref[rows, hcols[hh]] = (y * g_ref[rows, hcols[hh]].astype(F32)).astype(BF16)


def _mla_flash(qn, qr, kn, krp, vt, g, batch, seq):
    tokens = batch * seq
    nblk = seq // FLASH_BLOCK
    pair_spec = pl.BlockSpec((None, seq, 2 * LANES), lambda b, p: (p, b, 0))
    return pl.pallas_call(
        functools.partial(_mla_flash_kernel, seq=seq),
        grid=(batch, MLA_HEADS // 2),
        in_specs=[pair_spec,
                  pl.BlockSpec((None, seq, LANES), lambda b, p: (p, b, 0)),
                  pair_spec,
                  pl.BlockSpec((seq, 2 * LANES), lambda b, p: (b, 0)),
                  pl.BlockSpec((nblk, 2 * LANES, FLASH_BLOCK), lambda b, p: (b, p, 0)),
                  pair_spec],
        out_specs=pair_spec,
        out_shape=jax.ShapeDtypeStruct((MLA_HEADS // 2, tokens, 2 * MLA_V), BF16),
        scratch_shapes=[pltpu.VMEM((2, seq, 2 * LANES), BF16),
                        pltpu.VMEM((2, 2, FLASH_BLOCK, FLASH_BLOCK), F32),
                        pltpu.VMEM((2, 1, FLASH_BLOCK), F32),
                        pltpu.VMEM((2, MLA_V + 16, FLASH_BLOCK), F32)],
        compiler_params=pltpu.CompilerParams(
            dimension_semantics=("arbitrary", "arbitrary"), vmem_limit_bytes=VMEM_LIMIT_BYTES),
        name="mla_flash",
    )(qn, qr, kn, krp, vt, g)


def _dup_heads(t):
    parts = []
    for hh in range(t.shape[1] // SWA_HEAD_DIM):
        piece = t[:, hh * SWA_HEAD_DIM:(hh + 1) * SWA_HEAD_DIM]
        parts += [piece, piece]
    return jnp.concatenate(parts, axis=1)


def _swa_in_kernel(x_ref, ln_ref, win_ref, c_ref, s1_ref, s2_ref,
                   q_ref, kd_ref, vt_ref, g_ref):
    scale = SWA_HEAD_DIM ** -0.5 * math.log2(math.e)
    o1 = SWA_WIDTH
    o2 = o1 + SWA_KV_WIDTH
    o3 = o2 + SWA_KV_WIDTH
    h = _rmsnorm(x_ref[...], ln_ref[...]).astype(BF16)
    c_t, s1_t, s2_t = c_ref[...], s1_ref[...], s2_ref[...]
    q = _rope_partial(_dot(h, win_ref[:, :o1]), c_t, s1_t, s2_t)
    q_ref[...] = (q * scale).astype(BF16)
    kv = _dot(h, win_ref[:, o1:o3])
    k = _rope_partial(kv[:, :SWA_KV_WIDTH], c_t, s1_t, s2_t)
    kd_ref[...] = _dup_heads(k).astype(BF16)
    vt = kv[:, SWA_KV_WIDTH:].T.astype(BF16)
    for cc in range(vt_ref.shape[0]):
        vt_ref[cc] = vt[:, cc * BLOCK:(cc + 1) * BLOCK]
    gate = _dot(h, win_ref[:, o3:])
    g_ref[...] = (gate * jax.nn.sigmoid(gate)).astype(BF16)


def _swa_in(x, layer_norm, layer, w_in, j, c_t, s1_t, s2_t):
    tokens = x.shape[0]
    rows = SWA_PROJ_ROWS
    row_spec = lambda width: pl.BlockSpec((rows, width), lambda i: (i, 0))
    consts = [(layer_norm, layer), (w_in, j)]
    out_widths = [SWA_WIDTH, SWA_KV_HEADS * LANES, None, SWA_WIDTH]
    vt_spec = pl.BlockSpec((rows // BLOCK, SWA_KV_WIDTH, BLOCK), lambda i: (i, 0, 0))
    vt_shape = jax.ShapeDtypeStruct((tokens // BLOCK, SWA_KV_WIDTH, BLOCK), BF16)
    return pl.pallas_call(
        _swa_in_kernel,
        grid=(tokens // rows,),
        in_specs=[row_spec(D_MODEL)] + [_layer_spec(c, ll) for c, ll in consts]
                 + [row_spec(LANES)] * 3,
        out_specs=[vt_spec if wd is None else row_spec(wd) for wd in out_widths],
        out_shape=[vt_shape if wd is None else jax.ShapeDtypeStruct((tokens, wd), BF16)
                   for wd in out_widths],
        compiler_params=pltpu.CompilerParams(
            dimension_semantics=("arbitrary",), vmem_limit_bytes=VMEM_LIMIT_BYTES),
        name="swa_in",
    )(x, *[c for c, _ in consts], c_t, s1_t, s2_t)


def _swa_attn_kernel(sink_ref, q_ref, kc_ref, kp_ref, vtc_ref, vtp_ref, g_ref, o_ref):
    has_prev = pl.program_id(1) > 0
    pairs = SWA_GROUP // 2
    key_i = lax.broadcasted_iota(jnp.int32, (BLOCK, pairs * BLOCK), 0)
    qry_i = lax.broadcasted_iota(jnp.int32, (BLOCK, pairs * BLOCK), 1) % BLOCK
    in_cur = key_i <= qry_i
    cur_w = jnp.where(in_cur, 1.0, 0.0).astype(BF16)
    low = _lane_index((BLOCK, LANES)) < SWA_HEAD_DIM
    zero = jnp.zeros((BLOCK, LANES), BF16)

    def scores(kvh, sub):
        cols = slice(kvh * LANES, (kvh + 1) * LANES)
        kc, kp = kc_ref[:, cols], kp_ref[:, cols]
        if sub == 0:
            k_half = jnp.concatenate([jnp.where(low, kp, zero), jnp.where(low, kc, zero)], axis=0)
        else:
            k_half = jnp.concatenate([jnp.where(low, zero, kp), jnp.where(low, zero, kc)], axis=0)
        head0 = kvh * SWA_GROUP
        q_rows = jnp.concatenate(
            [q_ref[:, (head0 + 2 * pp) * SWA_HEAD_DIM:(head0 + 2 * pp + 2) * SWA_HEAD_DIM]
             for pp in range(pairs)], axis=0)
        return _dot_nt(k_half, q_rows)

    def attend(kvh, sub, st):
        s_prev = jnp.where(has_prev, st[:BLOCK], NEG_INF)
        s = jnp.where(in_cur, st[BLOCK:], s_prev)
        sink = sink_ref[sub * SWA_KV_HEADS + kvh:sub * SWA_KV_HEADS + kvh + 1, :]
        m = jnp.maximum(jnp.max(s, axis=0, keepdims=True), sink)
        e = jnp.exp2(s - m)
        denom = jnp.sum(e, axis=0, keepdims=True) + jnp.exp2(sink - m)
        p = e.astype(BF16)
        p_cur = p * cur_w
        pt = jnp.concatenate([p - p_cur, p_cur], axis=0)
        vrows = slice(kvh * SWA_HEAD_DIM, (kvh + 1) * SWA_HEAD_DIM)
        vt = jnp.concatenate([vtp_ref[vrows, :], vtc_ref[vrows, :]], axis=1)
        return _dot(vt, pt) * (1.0 / denom)

    chains = [(kvh, sub) for kvh in range(SWA_KV_HEADS) for sub in range(2)]
    st_next = scores(*chains[0])
    outs = {}
    for ci, (kvh, sub) in enumerate(chains):
        st = st_next
        if ci + 1 < len(chains):
            st_next = scores(*chains[ci + 1])
        outs[sub] = attend(kvh, sub, st)
        if sub == 1:
            head0 = kvh * SWA_GROUP
            for pp in range(pairs):
                qc = slice(pp * BLOCK, (pp + 1) * BLOCK)
                y = jnp.concatenate([outs[0][:, qc], outs[1][:, qc]], axis=0).T
                pcols = slice((head0 + 2 * pp) * SWA_HEAD_DIM,
                              (head0 + 2 * pp + 2) * SWA_HEAD_DIM)
                o_ref[:, pcols] = (y * g_ref[:, pcols].astype(F32)).astype(BF16)


def _swa_sink_rows(sinks):
    s2 = (sinks * math.log2(math.e)).reshape(SWA_KV_HEADS, SWA_GROUP // 2, 2)
    rows = jnp.transpose(s2, (2, 0, 1)).reshape(2 * SWA_KV_HEADS, SWA_GROUP // 2, 1)
    return jnp.broadcast_to(rows, (2 * SWA_KV_HEADS, SWA_GROUP // 2, BLOCK)).reshape(
        2 * SWA_KV_HEADS, -1)


def _swa_attn(sinks, q, kd, vt, g, batch, seq):
    tokens = batch * seq
    nb = seq // BLOCK
    cur = lambda b, i: (b * nb + i, 0)
    prev = lambda b, i: (b * nb + jnp.maximum(i - 1, 0), 0)
    cur_t = lambda b, i: (b * nb + i, 0, 0)
    prev_t = lambda b, i: (b * nb + jnp.maximum(i - 1, 0), 0, 0)
    kv_width = SWA_KV_HEADS * LANES
    sink_rows = _swa_sink_rows(sinks)
    return pl.pallas_call(
        _swa_attn_kernel,
        grid=(batch, nb),
        in_specs=[pl.BlockSpec(sink_rows.shape, lambda b, i: (0, 0)),
                  pl.BlockSpec((BLOCK, SWA_WIDTH), cur),
                  pl.BlockSpec((BLOCK, kv_width), cur),
                  pl.BlockSpec((BLOCK, kv_width), prev),
                  pl.BlockSpec((None, SWA_KV_WIDTH, BLOCK), cur_t),
                  pl.BlockSpec((None, SWA_KV_WIDTH, BLOCK), prev_t),
                  pl.BlockSpec((BLOCK, SWA_WIDTH), cur)],
        out_specs=pl.BlockSpec((BLOCK, SWA_WIDTH), cur),
        out_shape=jax.ShapeDtypeStruct((tokens, SWA_WIDTH), BF16),
        compiler_params=pltpu.CompilerParams(
            dimension_semantics=("arbitrary", "arbitrary"), vmem_limit_bytes=VMEM_LIMIT_BYTES),
        name="swa_attn",
    )(sink_rows, q, kd, kd, vt, vt, g)


def _out_kernel(x_ref, y_ref, w_ref, fn_ref, o_ref, w_sc, *, final):
    @pl.when(pl.program_id(0) == 0)
    def _():
        w_sc[...] = w_ref[...].astype(BF16)

    y = _load_groups(y_ref) if len(y_ref.shape) == 3 else y_ref[...]
    x_new = x_ref[...] + _dot(y, w_sc[...])
    o_ref[...] = _rmsnorm(x_new, fn_ref[...]) if final else x_new


def _out_proj(x, y, w_out, j, final_norm, final):
    tokens = x.shape[0]
    rows = OUT_ROWS
    row_spec = lambda width: pl.BlockSpec((rows, width), lambda i: (i, 0))
    if y.ndim == 3:
        y_spec = pl.BlockSpec((y.shape[0], rows, y.shape[2]), lambda i: (0, i, 0))
    else:
        y_spec = row_spec(y.shape[1])
    return pl.pallas_call(
        functools.partial(_out_kernel, final=final),
        grid=(tokens // rows,),
        in_specs=[row_spec(D_MODEL), y_spec, _layer_spec(w_out, j),
                  _const_spec(final_norm.shape)],
        out_specs=row_spec(D_MODEL),
        out_shape=jax.ShapeDtypeStruct((tokens, D_MODEL), F32),
        scratch_shapes=[pltpu.VMEM(w_out.shape[1:], BF16)],
        compiler_params=pltpu.CompilerParams(
            dimension_semantics=("arbitrary",), vmem_limit_bytes=VMEM_LIMIT_BYTES),
        name="out_proj",
    )(x, y, w_out, final_norm)


def kernel(x, positions, layer_norm, mla_w_in, mla_q_norm, mla_w_uq, mla_kv_norm, mla_w_ukv,
           mla_w_out, swa_w_in, swa_sinks, swa_w_out, final_norm):
    batch, seq, d_model = x.shape
    tokens = batch * seq
    assert d_model == D_MODEL and seq % FLASH_BLOCK == 0 and tokens % OUT_ROWS == 0
    assert FLASH_BLOCK % MLA_PROJ_ROWS == 0 and SWA_PROJ_ROWS % BLOCK == 0
    mcos, msin, sc, s1, s2 = _rope_tables(positions)
    xt = x.reshape(tokens, d_model)
    fn = final_norm.reshape(1, -1)
    ln_stack = layer_norm[:, None, :]
    mla_w = {"w_in": mla_w_in.astype(BF16), "w_uq": mla_w_uq.astype(BF16),
             "w_ukv": mla_w_ukv.astype(BF16), "q_norm": mla_q_norm[:, None, :],
             "kv_norm": mla_kv_norm[:, None, :]}
    swa_w = swa_w_in.astype(BF16)
    for i in range(DEPTH):
        j = i // N_MIXERS
        final = i == DEPTH - 1
        if i % N_MIXERS == 0:
            qn, qr, kn, vt, krp, g = _mla_in(xt, ln_stack, i, mla_w, j, mcos, msin)
            y = _mla_flash(qn, qr, kn, krp, vt, g, batch, seq)
            w_out = mla_w_out
        else:
            q, kd, vt, g = _swa_in(xt, ln_stack, i, swa_w, j, sc, s1, s2)
            y = _swa_attn(swa_sinks[j], q, kd, vt, g, batch, seq)
            w_out = swa_w_out
        xt = _out_proj(xt, y, w_out, j, fn, final)
    return xt.reshape(batch, seq, d_model)
```

```python
import functools
import math

import jax
import jax.numpy as jnp
from jax import lax
from jax.experimental import pallas as pl
from jax.experimental.pallas import tpu as pltpu

D_MODEL = 2048
DEPTH = 4
N_MIXERS = 2
ROPE_THETA = 500000.0
NORM_EPS = 1e-6
BLOCK = 128
NEG_INF = -1e30

MLA_HEADS = 16
MLA_Q_RANK = 512
MLA_KV_RANK = 512
MLA_NOPE = 128
MLA_ROPE = 64
MLA_V = 128
MLA_WIDTH = MLA_HEADS * MLA_V

SWA_Q_HEADS = 32
SWA_KV_HEADS = 4
SWA_GROUP = SWA_Q_HEADS // SWA_KV_HEADS
SWA_HEAD_DIM = 64
SWA_ROPE_DIM = SWA_HEAD_DIM // 4
SWA_WIDTH = SWA_Q_HEADS * SWA_HEAD_DIM
SWA_KV_WIDTH = SWA_KV_HEADS * SWA_HEAD_DIM

LANES = 128
VMEM_LIMIT_BYTES = 58 * 1024 * 1024

MLA_PROJ_ROWS = 512
SWA_PROJ_ROWS = 512
OUT_ROWS = 512
FLASH_BLOCK = 512

BF16 = jnp.bfloat16
F32 = jnp.float32


def _dot(a, b):
    return jnp.dot(a, b, preferred_element_type=F32)


def _dot_nt(a, b):
    return lax.dot_general(a, b, (((1,), (1,)), ((), ())), preferred_element_type=F32)


def _rmsnorm(xf, g):
    y = xf * lax.rsqrt(jnp.mean(xf * xf, axis=-1, keepdims=True) + NORM_EPS)
    return y * g


def _tile_lanes(t, width):
    reps = width // t.shape[1]
    return t if reps == 1 else jnp.concatenate([t] * reps, axis=1)


def _lane_index(shape):
    return lax.broadcasted_iota(jnp.int32, shape, 1)


def _rope_half64(x, cos_t, sin_t):
    w = x.shape[1]
    fwd = pltpu.roll(x, w - MLA_ROPE // 2, 1)
    bwd = pltpu.roll(x, MLA_ROPE // 2, 1)
    first_half = (_lane_index(x.shape) % MLA_ROPE) < (MLA_ROPE // 2)
    swapped = jnp.where(first_half, fwd, bwd)
    return x * _tile_lanes(cos_t, w) + swapped * _tile_lanes(sin_t, w)


def _rope_partial(x, c_t, s1_t, s2_t):
    w = x.shape[1]
    half = SWA_ROPE_DIM // 2
    fwd = pltpu.roll(x, w - half, 1)
    bwd = pltpu.roll(x, half, 1)
    return (x * _tile_lanes(c_t, w) + fwd * _tile_lanes(s1_t, w)
            + bwd * _tile_lanes(s2_t, w))


def _rope_table_kernel(pos_ref, const_ref, mcos_ref, msin_ref, sc_ref, s1_ref, s2_ref):
    pos = pos_ref[...].astype(F32)
    ang_m = pos * const_ref[0:1, :]
    mcos_ref[...] = jnp.cos(ang_m)
    msin_ref[...] = jnp.sin(ang_m) * const_ref[1:2, :]
    ang_s = pos * const_ref[2:3, :]
    sin_s = jnp.sin(ang_s)
    sc_ref[...] = jnp.cos(ang_s)
    s1_ref[...] = sin_s * const_ref[3:4, :]
    s2_ref[...] = sin_s * const_ref[4:5, :]


def _rope_constants():
    lane = jnp.arange(LANES)
    f_mla = ROPE_THETA ** (-jnp.arange(0, MLA_ROPE, 2, dtype=F32) / MLA_ROPE)
    f_swa = ROPE_THETA ** (-jnp.arange(0, SWA_ROPE_DIM, 2, dtype=F32) / SWA_ROPE_DIM)
    half_m = MLA_ROPE // 2
    half_s = SWA_ROPE_DIM // 2
    d_m = lane % MLA_ROPE
    d_s = lane % SWA_HEAD_DIM
    rows = [
        f_mla[d_m % half_m],
        jnp.where(d_m < half_m, -1.0, 1.0),
        jnp.where(d_s < SWA_ROPE_DIM, f_swa[d_s % half_s], 0.0),
        jnp.where(d_s < half_s, -1.0, 0.0),
        jnp.where((d_s >= half_s) & (d_s < SWA_ROPE_DIM), 1.0, 0.0),
    ]
    rows += [jnp.zeros((LANES,), F32)] * 3
    return jnp.stack([r.astype(F32) for r in rows])


def _rope_tables(positions):
    tokens = positions.size
    rows = 1024
    tab = jax.ShapeDtypeStruct((tokens, LANES), F32)
    spec = pl.BlockSpec((rows, LANES), lambda i: (i, 0))
    return pl.pallas_call(
        _rope_table_kernel,
        grid=(tokens // rows,),
        in_specs=[pl.BlockSpec((rows, 1), lambda i: (i, 0)),
                  pl.BlockSpec((8, LANES), lambda i: (0, 0))],
        out_specs=[spec] * 5,
        out_shape=[tab] * 5,
        name="rope_tables",
    )(positions.reshape(tokens, 1), _rope_constants())


def _mla_in_kernel(x_ref, ln_ref, win_ref, qnorm_ref, kvnorm_ref, wuq_ref, wukv_ref,
                   cos_ref, sin_ref, qn_ref, qr_ref, kn_ref, vt_ref, krp_ref, g_ref):
    scale = (MLA_NOPE + MLA_ROPE) ** -0.5 * math.log2(math.e)
    h = _rmsnorm(x_ref[...], ln_ref[...]).astype(BF16)
    cos_t = cos_ref[...]
    sin_t = sin_ref[...]
    rank2 = MLA_Q_RANK + MLA_KV_RANK
    heads = range(MLA_HEADS)

    c = _dot(h, win_ref[:, :rank2])
    cq = _rmsnorm(c[:, :MLA_Q_RANK], qnorm_ref[...]).astype(BF16)
    ckv = _rmsnorm(c[:, MLA_Q_RANK:], kvnorm_ref[...]).astype(BF16)

    tail = _dot(h, win_ref[:, rank2:])
    kr = tail[:, :MLA_ROPE]
    kr = _rope_half64(jnp.concatenate([kr, kr], axis=1), cos_t, sin_t)
    low = _lane_index(kr.shape) < MLA_ROPE
    krp_ref[:, :LANES] = jnp.where(low, kr, 0.0).astype(BF16)
    krp_ref[:, LANES:] = jnp.where(low, 0.0, kr).astype(BF16)

    gate = tail[:, MLA_ROPE:]
    _store_groups(g_ref, (gate * jax.nn.sigmoid(gate)).astype(BF16))

    q = _dot(cq, wuq_ref[...])
    qd = MLA_NOPE + MLA_ROPE
    qn = jnp.concatenate([q[:, hh * qd:hh * qd + MLA_NOPE] for hh in heads], axis=1)
    qr = jnp.concatenate([q[:, hh * qd + MLA_NOPE:(hh + 1) * qd] for hh in heads], axis=1)
    _store_groups(qn_ref, (qn * scale).astype(BF16))
    _store_groups(qr_ref, (_rope_half64(qr, cos_t, sin_t) * scale).astype(BF16))

    kv = _dot(ckv, wukv_ref[...])
    kd = MLA_NOPE + MLA_V
    kn = jnp.concatenate([kv[:, hh * kd:hh * kd + MLA_NOPE] for hh in heads], axis=1)
    v = jnp.concatenate([kv[:, hh * kd + MLA_NOPE:(hh + 1) * kd] for hh in heads], axis=1)
    _store_groups(kn_ref, kn.astype(BF16))
    vt_ref[0] = v.T.astype(BF16)


def _store_groups(ref, val):
    groups, _, width = ref.shape
    for gg in range(groups):
        ref[gg] = val[:, gg * width:(gg + 1) * width]


def _load_groups(ref):
    return jnp.concatenate([ref[gg] for gg in range(ref.shape[0])], axis=1)


def _const_spec(shape):
    return pl.BlockSpec(shape, lambda i: (0,) * len(shape), pipeline_mode=pl.Buffered(1))


def _layer_spec(stacked, layer):
    tail = (0,) * (stacked.ndim - 1)
    return pl.BlockSpec((None,) + stacked.shape[1:], lambda i: (layer,) + tail,
                        pipeline_mode=pl.Buffered(1))


def _mla_in(x, layer_norm, layer, w, j, cos_t, sin_t):
    tokens = x.shape[0]
    rows = MLA_PROJ_ROWS
    row_spec = lambda width: pl.BlockSpec((rows, width), lambda i: (i, 0))
    consts = [(layer_norm, layer), (w["w_in"], j), (w["q_norm"], j), (w["kv_norm"], j),
              (w["w_uq"], j), (w["w_ukv"], j)]
    pairs = MLA_HEADS // 2
    per_blk = FLASH_BLOCK // rows

    def pair_major(width):
        return (pl.BlockSpec((pairs, rows, width), lambda i: (0, i, 0)),
                jax.ShapeDtypeStruct((pairs, tokens, width), BF16))

    outs = [pair_major(2 * MLA_NOPE),
            pair_major(2 * MLA_ROPE),
            pair_major(2 * MLA_NOPE),
            (pl.BlockSpec((1, MLA_WIDTH, rows), lambda i: (i // per_blk, 0, i % per_blk)),
             jax.ShapeDtypeStruct((tokens // FLASH_BLOCK, MLA_WIDTH, FLASH_BLOCK), BF16)),
            (row_spec(2 * LANES), jax.ShapeDtypeStruct((tokens, 2 * LANES), BF16)),
            pair_major(2 * MLA_V)]
    return pl.pallas_call(
        _mla_in_kernel,
        grid=(tokens // rows,),
        in_specs=[row_spec(D_MODEL)] + [_layer_spec(c, ll) for c, ll in consts]
                 + [row_spec(LANES), row_spec(LANES)],
        out_specs=[o[0] for o in outs],
        out_shape=[o[1] for o in outs],
        compiler_params=pltpu.CompilerParams(
            dimension_semantics=("arbitrary",), vmem_limit_bytes=VMEM_LIMIT_BYTES),
        name="mla_in",
    )(x, *[c for c, _ in consts], cos_t, sin_t)


def _mla_flash_kernel(qn_ref, qr_ref, kn_ref, kr_ref, vt_ref, g_ref, o_ref,
                      kcat_sc, s_sc, m_sc, acc_sc, *, seq):
    blk = FLASH_BLOCK
    pair = (0, 1)
    hcols = [slice(hh * LANES, (hh + 1) * LANES) for hh in pair]
    for hh in pair:
        kcat_sc[hh, :, :LANES] = kn_ref[:, hcols[hh]]
        kcat_sc[hh, :, LANES:] = kr_ref[:, hcols[hh]]
    ones_rows = jnp.ones((16, blk), BF16)
    causal_t = (lax.broadcasted_iota(jnp.int32, (blk, blk), 0)
                <= lax.broadcasted_iota(jnp.int32, (blk, blk), 1))

    for qi in range(seq // blk):
        rows = slice(qi * blk, (qi + 1) * blk)
        qs = [jnp.concatenate([qn_ref[rows, hcols[hh]], qr_ref[rows, :]], axis=1) for hh in pair]
        m_sc[...] = jnp.full(m_sc.shape, NEG_INF, F32)
        acc_sc[...] = jnp.zeros(acc_sc.shape, F32)

        def scores(hh, kb, slot, masked, qs=qs):
            kstart = pl.multiple_of(kb * blk, blk)
            st = _dot_nt(kcat_sc[hh, pl.ds(kstart, blk), :], qs[hh])
            s_sc[hh, slot] = jnp.where(causal_t, st, NEG_INF) if masked else st

        def accumulate(hh, kb, slot):
            st = s_sc[hh, slot]
            m_prev = m_sc[hh]
            m_new = jnp.maximum(m_prev, jnp.max(st, axis=0, keepdims=True))
            alpha = jnp.exp2(m_prev - m_new)
            p = jnp.exp2(st - m_new)
            vt_aug = jnp.concatenate([vt_ref[kb, hcols[hh], :], ones_rows], axis=0)
            acc_sc[hh] = alpha * acc_sc[hh] + _dot(vt_aug, p.astype(BF16))
            m_sc[hh] = m_new

        visits = [qi] + list(range(qi))
        for hh in pair:
            scores(hh, visits[0], 0, True)
        for i, kb in enumerate(visits):
            for hh in pair:
                if i + 1 < len(visits):
                    scores(hh, visits[i + 1], (i + 1) % 2, False)
                accumulate(hh, kb, i % 2)

        for hh in pair:
            acc = acc_sc[hh]
            y = (acc[:MLA_V] * (1.0 / acc[MLA_V:MLA_V + 1])).T
            o_ref[rows, hcols[hh]] = (y * g_ref[rows, hcols[hh]].astype(F32)).astype(BF16)


def _mla_flash(qn, qr, kn, krp, vt, g, batch, seq):
    tokens = batch * seq
    nblk = seq // FLASH_BLOCK
    pair_spec = pl.BlockSpec((None, seq, 2 * LANES), lambda b, p: (p, b, 0))
    return pl.pallas_call(
        functools.partial(_mla_flash_kernel, seq=seq),
        grid=(batch, MLA_HEADS // 2),
        in_specs=[pair_spec,
                  pl.BlockSpec((None, seq, LANES), lambda b, p: (p, b, 0)),
                  pair_spec,
                  pl.BlockSpec((seq, 2 * LANES), lambda b, p: (b, 0)),
                  pl.BlockSpec((nblk, 2 * LANES, FLASH_BLOCK), lambda b, p: (b, p, 0)),
                  pair_spec],
        out_specs=pair_spec,
        out_shape=jax.ShapeDtypeStruct((MLA_HEADS // 2, tokens, 2 * MLA_V), BF16),
        scratch_shapes=[pltpu.VMEM((2, seq, 2 * LANES), BF16),
                        pltpu.VMEM((2, 2, FLASH_BLOCK, FLASH_BLOCK), F32),
                        pltpu.VMEM((2, 1, FLASH_BLOCK), F32),
                        pltpu.VMEM((2, MLA_V + 16, FLASH_BLOCK), F32)],
        compiler_params=pltpu.CompilerParams(
            dimension_semantics=("arbitrary", "arbitrary"), vmem_limit_bytes=VMEM_LIMIT_BYTES),
        name="mla_flash",
    )(qn, qr, kn, krp, vt, g)


def _dup_heads(t):
    parts = []
    for hh in range(t.shape[1] // SWA_HEAD_DIM):
        piece = t[:, hh * SWA_HEAD_DIM:(hh + 1) * SWA_HEAD_DIM]
        parts += [piece, piece]
    return jnp.concatenate(parts, axis=1)


def _swa_in_kernel(x_ref, ln_ref, win_ref, c_ref, s1_ref, s2_ref,
                   q_ref, kd_ref, vt_ref, g_ref):
    scale = SWA_HEAD_DIM ** -0.5 * math.log2(math.e)
    o1 = SWA_WIDTH
    o2 = o1 + SWA_KV_WIDTH
    o3 = o2 + SWA_KV_WIDTH
    h = _rmsnorm(x_ref[...], ln_ref[...]).astype(BF16)
    c_t, s1_t, s2_t = c_ref[...], s1_ref[...], s2_ref[...]
    q = _rope_partial(_dot(h, win_ref[:, :o1]), c_t, s1_t, s2_t)
    q_ref[...] = (q * scale).astype(BF16)
    kv = _dot(h, win_ref[:, o1:o3])
    k = _rope_partial(kv[:, :SWA_KV_WIDTH], c_t, s1_t, s2_t)
    kd_ref[...] = _dup_heads(k).astype(BF16)
    vt = kv[:, SWA_KV_WIDTH:].T.astype(BF16)
    for cc in range(vt_ref.shape[0]):
        vt_ref[cc] = vt[:, cc * BLOCK:(cc + 1) * BLOCK]
    gate = _dot(h, win_ref[:, o3:])
    g_ref[...] = (gate * jax.nn.sigmoid(gate)).astype(BF16)


def _swa_in(x, layer_norm, layer, w_in, j, c_t, s1_t, s2_t):
    tokens = x.shape[0]
    rows = SWA_PROJ_ROWS
    row_spec = lambda width: pl.BlockSpec((rows, width), lambda i: (i, 0))
    consts = [(layer_norm, layer), (w_in, j)]
    out_widths = [SWA_WIDTH, SWA_KV_HEADS * LANES, None, SWA_WIDTH]
    vt_spec = pl.BlockSpec((rows // BLOCK, SWA_KV_WIDTH, BLOCK), lambda i: (i, 0, 0))
    vt_shape = jax.ShapeDtypeStruct((tokens // BLOCK, SWA_KV_WIDTH, BLOCK), BF16)
    return pl.pallas_call(
        _swa_in_kernel,
        grid=(tokens // rows,),
        in_specs=[row_spec(D_MODEL)] + [_layer_spec(c, ll) for c, ll in consts]
                 + [row_spec(LANES)] * 3,
        out_specs=[vt_spec if wd is None else row_spec(wd) for wd in out_widths],
        out_shape=[vt_shape if wd is None else jax.ShapeDtypeStruct((tokens, wd), BF16)
                   for wd in out_widths],
        compiler_params=pltpu.CompilerParams(
            dimension_semantics=("arbitrary",), vmem_limit_bytes=VMEM_LIMIT_BYTES),
        name="swa_in",
    )(x, *[c for c, _ in consts], c_t, s1_t, s2_t)


def _swa_attn_kernel(*refs):
    @pl.when(pl.program_id(1) == 0)
    def _():
        _swa_attn_body(*refs, has_prev=False)

    @pl.when(pl.program_id(1) > 0)
    def _():
        _swa_attn_body(*refs, has_prev=True)


def _swa_attn_body(sink_ref, q_ref, kc_ref, kp_ref, vtc_ref, vtp_ref, g_ref, o_ref, *, has_prev):
    pairs = SWA_GROUP // 2
    key_i = lax.broadcasted_iota(jnp.int32, (BLOCK, pairs * BLOCK), 0)
    qry_i = lax.broadcasted_iota(jnp.int32, (BLOCK, pairs * BLOCK), 1) % BLOCK
    in_cur = key_i <= qry_i
    cur_w = jnp.where(in_cur, 1.0, 0.0).astype(BF16)
    low = _lane_index((BLOCK, LANES)) < SWA_HEAD_DIM
    zero = jnp.zeros((BLOCK, LANES), BF16)

    def scores(kvh, sub):
        cols = slice(kvh * LANES, (kvh + 1) * LANES)
        kc, kp = kc_ref[:, cols], kp_ref[:, cols]
        if sub == 0:
            k_half = jnp.concatenate([jnp.where(low, kp, zero), jnp.where(low, kc, zero)], axis=0)
        else:
            k_half = jnp.concatenate([jnp.where(low, zero, kp), jnp.where(low, zero, kc)], axis=0)
        head0 = kvh * SWA_GROUP
        q_rows = jnp.concatenate(
            [q_ref[:, (head0 + 2 * pp) * SWA_HEAD_DIM:(head0 + 2 * pp + 2) * SWA_HEAD_DIM]
             for pp in range(pairs)], axis=0)
        return _dot_nt(k_half, q_rows)

    def attend(kvh, sub, st):
        s = jnp.where(in_cur, st[BLOCK:], st[:BLOCK] if has_prev else NEG_INF)
        sink = sink_ref[sub * SWA_KV_HEADS + kvh:sub * SWA_KV_HEADS + kvh + 1, :]
        m = jnp.maximum(jnp.max(s, axis=0, keepdims=True), sink)
        e = jnp.exp2(s - m)
        denom = jnp.sum(e, axis=0, keepdims=True) + jnp.exp2(sink - m)
        p = e.astype(BF16)
        p_cur = p * cur_w
        pt = jnp.concatenate([p - p_cur, p_cur], axis=0)
        vrows = slice(kvh * SWA_HEAD_DIM, (kvh + 1) * SWA_HEAD_DIM)
        vt = jnp.concatenate([vtp_ref[vrows, :], vtc_ref[vrows, :]], axis=1)
        return _dot(vt, pt) * (1.0 / denom)

    chains = [(kvh, sub) for kvh in range(SWA_KV_HEADS) for sub in range(2)]
    st_next = scores(*chains[0])
    outs = {}
    for ci, (kvh, sub) in enumerate(chains):
        st = st_next
        if ci + 1 < len(chains):
            st_next = scores(*chains[ci + 1])
        outs[sub] = attend(kvh, sub, st)
        if sub == 1:
            head0 = kvh * SWA_GROUP
            for pp in range(pairs):
                qc = slice(pp * BLOCK, (pp + 1) * BLOCK)
                y = jnp.concatenate([outs[0][:, qc], outs[1][:, qc]], axis=0).T
                pcols = slice((head0 + 2 * pp) * SWA_HEAD_DIM,
                              (head0 + 2 * pp + 2) * SWA_HEAD_DIM)
                o_ref[:, pcols] = (y * g_ref[:, pcols].astype(F32)).astype(BF16)


def _swa_sink_rows(sinks):
    s2 = (sinks * math.log2(math.e)).reshape(SWA_KV_HEADS, SWA_GROUP // 2, 2)
    rows = jnp.transpose(s2, (2, 0, 1)).reshape(2 * SWA_KV_HEADS, SWA_GROUP // 2, 1)
    return jnp.broadcast_to(rows, (2 * SWA_KV_HEADS, SWA_GROUP // 2, BLOCK)).reshape(
        2 * SWA_KV_HEADS, -1)


def _swa_attn(sinks, q, kd, vt, g, batch, seq):
    tokens = batch * seq
    nb = seq // BLOCK
    cur = lambda b, i: (b * nb + i, 0)
    prev = lambda b, i: (b * nb + jnp.maximum(i - 1, 0), 0)
    cur_t = lambda b, i: (b * nb + i, 0, 0)
    prev_t = lambda b, i: (b * nb + jnp.maximum(i - 1, 0), 0, 0)
    kv_width = SWA_KV_HEADS * LANES
    sink_rows = _swa_sink_rows(sinks)
    return pl.pallas_call(
        _swa_attn_kernel,
        grid=(batch, nb),
        in_specs=[pl.BlockSpec(sink_rows.shape, lambda b, i: (0, 0)),
                  pl.BlockSpec((BLOCK, SWA_WIDTH), cur),
                  pl.BlockSpec((BLOCK, kv_width), cur),
                  pl.BlockSpec((BLOCK, kv_width), prev),
                  pl.BlockSpec((None, SWA_KV_WIDTH, BLOCK), cur_t),
                  pl.BlockSpec((None, SWA_KV_WIDTH, BLOCK), prev_t),
                  pl.BlockSpec((BLOCK, SWA_WIDTH), cur)],
        out_specs=pl.BlockSpec((BLOCK, SWA_WIDTH), cur),
        out_shape=jax.ShapeDtypeStruct((tokens, SWA_WIDTH), BF16),
        compiler_params=pltpu.CompilerParams(
            dimension_semantics=("arbitrary", "arbitrary"), vmem_limit_bytes=VMEM_LIMIT_BYTES),
        name="swa_attn",
    )(sink_rows, q, kd, kd, vt, vt, g)


def _out_kernel(x_ref, y_ref, w_ref, fn_ref, o_ref, w_sc, *, final):
    @pl.when(pl.program_id(0) == 0)
    def _():
        w_sc[...] = w_ref[...].astype(BF16)

    y = _load_groups(y_ref) if len(y_ref.shape) == 3 else y_ref[...]
    x_new = x_ref[...] + _dot(y, w_sc[...])
    o_ref[...] = _rmsnorm(x_new, fn_ref[...]) if final else x_new


def _out_proj(x, y, w_out, j, final_norm, final):
    tokens = x.shape[0]
    rows = OUT_ROWS
    row_spec = lambda width: pl.BlockSpec((rows, width), lambda i: (i, 0))
    if y.ndim == 3:
        y_spec = pl.BlockSpec((y.shape[0], rows, y.shape[2]), lambda i: (0, i, 0))
    else:
        y_spec = row_spec(y.shape[1])
    return pl.pallas_call(
        functools.partial(_out_kernel, final=final),
        grid=(tokens // rows,),
        in_specs=[row_spec(D_MODEL), y_spec, _layer_spec(w_out, j),
                  _const_spec(final_norm.shape)],
        out_specs=row_spec(D_MODEL),
        out_shape=jax.ShapeDtypeStruct((tokens, D_MODEL), F32),
        scratch_shapes=[pltpu.VMEM(w_out.shape[1:], BF16)],
        compiler_params=pltpu.CompilerParams(
            dimension_semantics=("arbitrary",), vmem_limit_bytes=VMEM_LIMIT_BYTES),
        name="out_proj",
    )(x, y, w_out, final_norm)


def kernel(x, positions, layer_norm, mla_w_in, mla_q_norm, mla_w_uq, mla_kv_norm, mla_w_ukv,
           mla_w_out, swa_w_in, swa_sinks, swa_w_out, final_norm):
    batch, seq, d_model = x.shape
    tokens = batch * seq
    assert d_model == D_MODEL and seq % FLASH_BLOCK == 0 and tokens % OUT_ROWS == 0
    assert FLASH_BLOCK % MLA_PROJ_ROWS == 0 and SWA_PROJ_ROWS % BLOCK == 0
    mcos, msin, sc, s1, s2 = _rope_tables(positions)
    xt = x.reshape(tokens, d_model)
    fn = final_norm.reshape(1, -1)
    ln_stack = layer_norm[:, None, :]
    mla_w = {"w_in": mla_w_in.astype(BF16), "w_uq": mla_w_uq.astype(BF16),
             "w_ukv": mla_w_ukv.astype(BF16), "q_norm": mla_q_norm[:, None, :],
             "kv_norm": mla_kv_norm[:, None, :]}
    swa_w = swa_w_in.astype(BF16)
    for i in range(DEPTH):
        j = i // N_MIXERS
        final = i == DEPTH - 1
        if i % N_MIXERS == 0:
            qn, qr, kn, vt, krp, g = _mla_in(xt, ln_stack, i, mla_w, j, mcos, msin)
            y = _mla_flash(qn, qr, kn, krp, vt, g, batch, seq)
            w_out = mla_w_out
        else:
            q, kd, vt, g = _swa_in(xt, ln_stack, i, swa_w, j, sc, s1, s2)
            y = _swa_attn(swa_sinks[j], q, kd, vt, g, batch, seq)
            w_out = swa_w_out
        xt = _out_proj(xt, y, w_out, j, fn, final)
    return xt.reshape(batch, seq, d_model)
```

```python
import functools
import math

import jax
import jax.numpy as jnp
from jax import lax
from jax.experimental import pallas as pl
from jax.experimental.pallas import tpu as pltpu

D_MODEL = 2048
DEPTH = 4
N_MIXERS = 2
ROPE_THETA = 500000.0
NORM_EPS = 1e-6
BLOCK = 128
NEG_INF = -1e30

MLA_HEADS = 16
MLA_Q_RANK = 512
MLA_KV_RANK = 512
MLA_NOPE = 128
MLA_ROPE = 64
MLA_V = 128
MLA_WIDTH = MLA_HEADS * MLA_V

SWA_Q_HEADS = 32
SWA_KV_HEADS = 4
SWA_GROUP = SWA_Q_HEADS // SWA_KV_HEADS
SWA_HEAD_DIM = 64
SWA_ROPE_DIM = SWA_HEAD_DIM // 4
SWA_WIDTH = SWA_Q_HEADS * SWA_HEAD_DIM
SWA_KV_WIDTH = SWA_KV_HEADS * SWA_HEAD_DIM

LANES = 128
VMEM_LIMIT_BYTES = 58 * 1024 * 1024

MLA_PROJ_ROWS = 512
SWA_PROJ_ROWS = 512
OUT_ROWS = 512
FLASH_BLOCK = 512

BF16 = jnp.bfloat16
F32 = jnp.float32


def _dot(a, b):
    return jnp.dot(a, b, preferred_element_type=F32)


def _dot_nt(a, b):
    return lax.dot_general(a, b, (((1,), (1,)), ((), ())), preferred_element_type=F32)


def _rmsnorm(xf, g):
    y = xf * lax.rsqrt(jnp.mean(xf * xf, axis=-1, keepdims=True) + NORM_EPS)
    return y * g


def _tile_lanes(t, width):
    reps = width // t.shape[1]
    return t if reps == 1 else jnp.concatenate([t] * reps, axis=1)


def _lane_index(shape):
    return lax.broadcasted_iota(jnp.int32, shape, 1)


def _rope_half64(x, cos_t, sin_t):
    w = x.shape[1]
    fwd = pltpu.roll(x, w - MLA_ROPE // 2, 1)
    bwd = pltpu.roll(x, MLA_ROPE // 2, 1)
    first_half = (_lane_index(x.shape) % MLA_ROPE) < (MLA_ROPE // 2)
    swapped = jnp.where(first_half, fwd, bwd)
    return x * _tile_lanes(cos_t, w) + swapped * _tile_lanes(sin_t, w)


def _rope_partial(x, c_t, s1_t, s2_t):
    w = x.shape[1]
    half = SWA_ROPE_DIM // 2
    fwd = pltpu.roll(x, w - half, 1)
    bwd = pltpu.roll(x, half, 1)
    return (x * _tile_lanes(c_t, w) + fwd * _tile_lanes(s1_t, w)
            + bwd * _tile_lanes(s2_t, w))


def _rope_table_kernel(pos_ref, const_ref, mcos_ref, msin_ref, sc_ref, s1_ref, s2_ref):
    pos = pos_ref[...].astype(F32)
    ang_m = pos * const_ref[0:1, :]
    mcos_ref[...] = jnp.cos(ang_m)
    msin_ref[...] = jnp.sin(ang_m) * const_ref[1:2, :]
    ang_s = pos * const_ref[2:3, :]
    sin_s = jnp.sin(ang_s)
    sc_ref[...] = jnp.cos(ang_s)
    s1_ref[...] = sin_s * const_ref[3:4, :]
    s2_ref[...] = sin_s * const_ref[4:5, :]


def _rope_constants():
    lane = jnp.arange(LANES)
    f_mla = ROPE_THETA ** (-jnp.arange(0, MLA_ROPE, 2, dtype=F32) / MLA_ROPE)
    f_swa = ROPE_THETA ** (-jnp.arange(0, SWA_ROPE_DIM, 2, dtype=F32) / SWA_ROPE_DIM)
    half_m = MLA_ROPE // 2
    half_s = SWA_ROPE_DIM // 2
    d_m = lane % MLA_ROPE
    d_s = lane % SWA_HEAD_DIM
    rows = [
        f_mla[d_m % half_m],
        jnp.where(d_m < half_m, -1.0, 1.0),
        jnp.where(d_s < SWA_ROPE_DIM, f_swa[d_s % half_s], 0.0),
        jnp.where(d_s < half_s, -1.0, 0.0),
        jnp.where((d_s >= half_s) & (d_s < SWA_ROPE_DIM), 1.0, 0.0),
    ]
    rows += [jnp.zeros((LANES,), F32)] * 3
    return jnp.stack([r.astype(F32) for r in rows])


def _rope_tables(positions):
    tokens = positions.size
    rows = 1024
    tab = jax.ShapeDtypeStruct((tokens, LANES), F32)
    spec = pl.BlockSpec((rows, LANES), lambda i: (i, 0))
    return pl.pallas_call(
        _rope_table_kernel,
        grid=(tokens // rows,),
        in_specs=[pl.BlockSpec((rows, 1), lambda i: (i, 0)),
                  pl.BlockSpec((8, LANES), lambda i: (0, 0))],
        out_specs=[spec] * 5,
        out_shape=[tab] * 5,
        name="rope_tables",
    )(positions.reshape(tokens, 1), _rope_constants())


def _mla_in_kernel(x_ref, ln_ref, win_ref, qnorm_ref, kvnorm_ref, wuq_ref, wukv_ref,
                   cos_ref, sin_ref, qn_ref, qr_ref, kn_ref, vt_ref, krp_ref, g_ref):
    scale = (MLA_NOPE + MLA_ROPE) ** -0.5 * math.log2(math.e)
    h = _rmsnorm(x_ref[...], ln_ref[...]).astype(BF16)
    cos_t = cos_ref[...]
    sin_t = sin_ref[...]
    rank2 = MLA_Q_RANK + MLA_KV_RANK
    heads = range(MLA_HEADS)

    c = _dot(h, win_ref[:, :rank2])
    cq = _rmsnorm(c[:, :MLA_Q_RANK], qnorm_ref[...]).astype(BF16)
    ckv = _rmsnorm(c[:, MLA_Q_RANK:], kvnorm_ref[...]).astype(BF16)

    tail = _dot(h, win_ref[:, rank2:])
    kr = tail[:, :MLA_ROPE]
    kr = _rope_half64(jnp.concatenate([kr, kr], axis=1), cos_t, sin_t)
    low = _lane_index(kr.shape) < MLA_ROPE
    krp_ref[:, :LANES] = jnp.where(low, kr, 0.0).astype(BF16)
    krp_ref[:, LANES:] = jnp.where(low, 0.0, kr).astype(BF16)

    gate = tail[:, MLA_ROPE:]
    _store_groups(g_ref, (gate * jax.nn.sigmoid(gate)).astype(BF16))

    q = _dot(cq, wuq_ref[...])
    qd = MLA_NOPE + MLA_ROPE
    qn = jnp.concatenate([q[:, hh * qd:hh * qd + MLA_NOPE] for hh in heads], axis=1)
    qr = jnp.concatenate([q[:, hh * qd + MLA_NOPE:(hh + 1) * qd] for hh in heads], axis=1)
    _store_groups(qn_ref, (qn * scale).astype(BF16))
    _store_groups(qr_ref, (_rope_half64(qr, cos_t, sin_t) * scale).astype(BF16))

    kv = _dot(ckv, wukv_ref[...])
    kd = MLA_NOPE + MLA_V
    kn = jnp.concatenate([kv[:, hh * kd:hh * kd + MLA_NOPE] for hh in heads], axis=1)
    v = jnp.concatenate([kv[:, hh * kd + MLA_NOPE:(hh + 1) * kd] for hh in heads], axis=1)
    _store_groups(kn_ref, kn.astype(BF16))
    vt_ref[0] = v.T.astype(BF16)


def _store_groups(ref, val):
    groups, _, width = ref.shape
    for gg in range(groups):
        ref[gg] = val[:, gg * width:(gg + 1) * width]


def _load_groups(ref):
    return jnp.concatenate([ref[gg] for gg in range(ref.shape[0])], axis=1)


def _const_spec(shape):
    return pl.BlockSpec(shape, lambda i: (0,) * len(shape), pipeline_mode=pl.Buffered(1))


def _layer_spec(stacked, layer):
    tail = (0,) * (stacked.ndim - 1)
    return pl.BlockSpec((None,) + stacked.shape[1:], lambda i: (layer,) + tail,
                        pipeline_mode=pl.Buffered(1))


def _mla_in(x, layer_norm, layer, w, j, cos_t, sin_t):
    tokens = x.shape[0]
    rows = MLA_PROJ_ROWS
    row_spec = lambda width: pl.BlockSpec((rows, width), lambda i: (i, 0))
    consts = [(layer_norm, layer), (w["w_in"], j), (w["q_norm"], j), (w["kv_norm"], j),
              (w["w_uq"], j), (w["w_ukv"], j)]
    pairs = MLA_HEADS // 2
    per_blk = FLASH_BLOCK // rows

    def pair_major(width):
        return (pl.BlockSpec((pairs, rows, width), lambda i: (0, i, 0)),
                jax.ShapeDtypeStruct((pairs, tokens, width), BF16))

    outs = [pair_major(2 * MLA_NOPE),
            pair_major(2 * MLA_ROPE),
            pair_major(2 * MLA_NOPE),
            (pl.BlockSpec((1, MLA_WIDTH, rows), lambda i: (i // per_blk, 0, i % per_blk)),
             jax.ShapeDtypeStruct((tokens // FLASH_BLOCK, MLA_WIDTH, FLASH_BLOCK), BF16)),
            (row_spec(2 * LANES), jax.ShapeDtypeStruct((tokens, 2 * LANES), BF16)),
            pair_major(2 * MLA_V)]
    return pl.pallas_call(
        _mla_in_kernel,
        grid=(tokens // rows,),
        in_specs=[row_spec(D_MODEL)] + [_layer_spec(c, ll) for c, ll in consts]
                 + [row_spec(LANES), row_spec(LANES)],
        out_specs=[o[0] for o in outs],
        out_shape=[o[1] for o in outs],
        compiler_params=pltpu.CompilerParams(
            dimension_semantics=("arbitrary",), vmem_limit_bytes=VMEM_LIMIT_BYTES),
        name="mla_in",
    )(x, *[c for c, _ in consts], cos_t, sin_t)


def _mla_flash_kernel(qn_ref, qr_ref, kn_ref, kr_ref, vt_ref, g_ref, o_ref,
                      kcat_sc, s_sc, m_sc, acc_sc, *, seq):
    blk = FLASH_BLOCK
    pair = (0, 1)
    hcols = [slice(hh * LANES, (hh + 1) * LANES) for hh in pair]
    for hh in pair:
        kcat_sc[hh, :, :LANES] = kn_ref[:, hcols[hh]]
        kcat_sc[hh, :, LANES:] = kr_ref[:, hcols[hh]]
    ones_rows = jnp.ones((16, blk), BF16)
    causal_t = (lax.broadcasted_iota(jnp.int32, (blk, blk), 0)
                <= lax.broadcasted_iota(jnp.int32, (blk, blk), 1))

    for qi in range(seq // blk):
        rows = slice(qi * blk, (qi + 1) * blk)
        qs = [jnp.concatenate([qn_ref[rows, hcols[hh]], qr_ref[rows, :]], axis=1) for hh in pair]
        def scores(hh, kb, slot, masked, qs=qs):
            kstart = pl.multiple_of(kb * blk, blk)
            st = _dot_nt(kcat_sc[hh, pl.ds(kstart, blk), :], qs[hh])
            s_sc[hh, slot] = jnp.where(causal_t, st, NEG_INF) if masked else st

        def accumulate(hh, kb, slot, first):
            st = s_sc[hh, slot]
            m_prev = jnp.float32(NEG_INF) if first else m_sc[hh]
            m_new = jnp.maximum(m_prev, jnp.max(st, axis=0, keepdims=True))
            p = jnp.exp2(st - m_new)
            vt_aug = jnp.concatenate([vt_ref[kb, hcols[hh], :], ones_rows], axis=0)
            pv = _dot(vt_aug, p.astype(BF16))
            acc_sc[hh] = pv if first else jnp.exp2(m_prev - m_new) * acc_sc[hh] + pv
            m_sc[hh] = m_new

        visits = [qi] + list(range(qi))
        for hh in pair:
            scores(hh, visits[0], 0, True)
        for i, kb in enumerate(visits):
            for hh in pair:
                if i + 1 < len(visits):
                    scores(hh, visits[i + 1], (i + 1) % 2, False)
                accumulate(hh, kb, i % 2, i == 0)

        for hh in pair:
            acc = acc_sc[hh]
            y = (acc[:MLA_V] * (1.0 / acc[MLA_V:MLA_V + 1])).T
            o_ref[rows, hcols[hh]] = (y * g_ref[rows, hcols[hh]].astype(F32)).astype(BF16)


def _mla_flash(qn, qr, kn, krp, vt, g, batch, seq):
    tokens = batch * seq
    nblk = seq // FLASH_BLOCK
    pair_spec = pl.BlockSpec((None, seq, 2 * LANES), lambda b, p: (p, b, 0))
    return pl.pallas_call(
        functools.partial(_mla_flash_kernel, seq=seq),
        grid=(batch, MLA_HEADS // 2),
        in_specs=[pair_spec,
                  pl.BlockSpec((None, seq, LANES), lambda b, p: (p, b, 0)),
                  pair_spec,
                  pl.BlockSpec((seq, 2 * LANES), lambda b, p: (b, 0)),
                  pl.BlockSpec((nblk, 2 * LANES, FLASH_BLOCK), lambda b, p: (b, p, 0)),
                  pair_spec],
        out_specs=pair_spec,
        out_shape=jax.ShapeDtypeStruct((MLA_HEADS // 2, tokens, 2 * MLA_V), BF16),
        scratch_shapes=[pltpu.VMEM((2, seq, 2 * LANES), BF16),
                        pltpu.VMEM((2, 2, FLASH_BLOCK, FLASH_BLOCK), F32),
                        pltpu.VMEM((2, 1, FLASH_BLOCK), F32),
                        pltpu.VMEM((2, MLA_V + 16, FLASH_BLOCK), F32)],
        compiler_params=pltpu.CompilerParams(
            dimension_semantics=("arbitrary", "arbitrary"), vmem_limit_bytes=VMEM_LIMIT_BYTES),
        name="mla_flash",
    )(qn, qr, kn, krp, vt, g)


def _dup_heads(t):
    parts = []
    for hh in range(t.shape[1] // SWA_HEAD_DIM):
        piece = t[:, hh * SWA_HEAD_DIM:(hh + 1) * SWA_HEAD_DIM]
        parts += [piece, piece]
    return jnp.concatenate(parts, axis=1)


def _swa_in_kernel(x_ref, ln_ref, win_ref, c_ref, s1_ref, s2_ref,
                   q_ref, kd_ref, vt_ref, g_ref):
    scale = SWA_HEAD_DIM ** -0.5 * math.log2(math.e)
    o1 = SWA_WIDTH
    o2 = o1 + SWA_KV_WIDTH
    o3 = o2 + SWA_KV_WIDTH
    h = _rmsnorm(x_ref[...], ln_ref[...]).astype(BF16)
    c_t, s1_t, s2_t = c_ref[...], s1_ref[...], s2_ref[...]
    q = _rope_partial(_dot(h, win_ref[:, :o1]), c_t, s1_t, s2_t)
    q_ref[...] = (q * scale).astype(BF16)
    kv = _dot(h, win_ref[:, o1:o3])
    k = _rope_partial(kv[:, :SWA_KV_WIDTH], c_t, s1_t, s2_t)
    kd_ref[...] = _dup_heads(k).astype(BF16)
    vt = kv[:, SWA_KV_WIDTH:].T.astype(BF16)
    for cc in range(vt_ref.shape[0]):
        vt_ref[cc] = vt[:, cc * BLOCK:(cc + 1) * BLOCK]
    gate = _dot(h, win_ref[:, o3:])
    g_ref[...] = (gate * jax.nn.sigmoid(gate)).astype(BF16)


def _swa_in(x, layer_norm, layer, w_in, j, c_t, s1_t, s2_t):
    tokens = x.shape[0]
    rows = SWA_PROJ_ROWS
    row_spec = lambda width: pl.BlockSpec((rows, width), lambda i: (i, 0))
    consts = [(layer_norm, layer), (w_in, j)]
    out_widths = [SWA_WIDTH, SWA_KV_HEADS * LANES, None, SWA_WIDTH]
    vt_spec = pl.BlockSpec((rows // BLOCK, SWA_KV_WIDTH, BLOCK), lambda i: (i, 0, 0))
    vt_shape = jax.ShapeDtypeStruct((tokens // BLOCK, SWA_KV_WIDTH, BLOCK), BF16)
    return pl.pallas_call(
        _swa_in_kernel,
        grid=(tokens // rows,),
        in_specs=[row_spec(D_MODEL)] + [_layer_spec(c, ll) for c, ll in consts]
                 + [row_spec(LANES)] * 3,
        out_specs=[vt_spec if wd is None else row_spec(wd) for wd in out_widths],
        out_shape=[vt_shape if wd is None else jax.ShapeDtypeStruct((tokens, wd), BF16)
                   for wd in out_widths],
        compiler_params=pltpu.CompilerParams(
            dimension_semantics=("arbitrary",), vmem_limit_bytes=VMEM_LIMIT_BYTES),
        name="swa_in",
    )(x, *[c for c, _ in consts], c_t, s1_t, s2_t)


def _swa_attn_kernel(sink_ref, q_ref, kc_ref, kp_ref, vtc_ref, vtp_ref, g_ref, o_ref):
    has_prev = pl.program_id(1) > 0
    pairs = SWA_GROUP // 2
    key_i = lax.broadcasted_iota(jnp.int32, (BLOCK, pairs * BLOCK), 0)
    qry_i = lax.broadcasted_iota(jnp.int32, (BLOCK, pairs * BLOCK), 1) % BLOCK
    in_cur = key_i <= qry_i
    cur_w = jnp.where(in_cur, 1.0, 0.0).astype(BF16)
    low = _lane_index((BLOCK, LANES)) < SWA_HEAD_DIM
    zero = jnp.zeros((BLOCK, LANES), BF16)

    def scores(kvh, sub):
        cols = slice(kvh * LANES, (kvh + 1) * LANES)
        kc, kp = kc_ref[:, cols], kp_ref[:, cols]
        if sub == 0:
            k_half = jnp.concatenate([jnp.where(low, kp, zero), jnp.where(low, kc, zero)], axis=0)
        else:
            k_half = jnp.concatenate([jnp.where(low, zero, kp), jnp.where(low, zero, kc)], axis=0)
        head0 = kvh * SWA_GROUP
        q_rows = jnp.concatenate(
            [q_ref[:, (head0 + 2 * pp) * SWA_HEAD_DIM:(head0 + 2 * pp + 2) * SWA_HEAD_DIM]
             for pp in range(pairs)], axis=0)
        return _dot_nt(k_half, q_rows)

    def attend(kvh, sub, st):
        s_prev = jnp.where(has_prev, st[:BLOCK], NEG_INF)
        s = jnp.where(in_cur, st[BLOCK:], s_prev)
        sink = sink_ref[sub * SWA_KV_HEADS + kvh:sub * SWA_KV_HEADS + kvh + 1, :]
        m = jnp.maximum(jnp.max(s, axis=0, keepdims=True), sink)
        e = jnp.exp2(s - m)
        denom = jnp.sum(e, axis=0, keepdims=True) + jnp.exp2(sink - m)
        p = e.astype(BF16)
        p_cur = p * cur_w
        pt = jnp.concatenate([p - p_cur, p_cur], axis=0)
        vrows = slice(kvh * SWA_HEAD_DIM, (kvh + 1) * SWA_HEAD_DIM)
        vt = jnp.concatenate([vtp_ref[vrows, :], vtc_ref[vrows, :]], axis=1)
        return _dot(vt, pt) * (1.0 / denom)

    chains = [(kvh, sub) for kvh in range(SWA_KV_HEADS) for sub in range(2)]
    st_next = scores(*chains[0])
    outs = {}
    for ci, (kvh, sub) in enumerate(chains):
        st = st_next
        if ci + 1 < len(chains):
            st_next = scores(*chains[ci + 1])
        outs[sub] = attend(kvh, sub, st)
        if sub == 1:
            head0 = kvh * SWA_GROUP
            for pp in range(pairs):
                qc = slice(pp * BLOCK, (pp + 1) * BLOCK)
                y = jnp.concatenate([outs[0][:, qc], outs[1][:, qc]], axis=0).T
                pcols = slice((head0 + 2 * pp) * SWA_HEAD_DIM,
                              (head0 + 2 * pp + 2) * SWA_HEAD_DIM)
                o_ref[:, pcols] = (y * g_ref[:, pcols].astype(F32)).astype(BF16)


def _swa_sink_rows(sinks):
    s2 = (sinks * math.log2(math.e)).reshape(SWA_KV_HEADS, SWA_GROUP // 2, 2)
    rows = jnp.transpose(s2, (2, 0, 1)).reshape(2 * SWA_KV_HEADS, SWA_GROUP // 2, 1)
    return jnp.broadcast_to(rows, (2 * SWA_KV_HEADS, SWA_GROUP // 2, BLOCK)).reshape(
        2 * SWA_KV_HEADS, -1)


def _swa_attn(sinks, q, kd, vt, g, batch, seq):
    tokens = batch * seq
    nb = seq // BLOCK
    cur = lambda b, i: (b * nb + i, 0)
    prev = lambda b, i: (b * nb + jnp.maximum(i - 1, 0), 0)
    cur_t = lambda b, i: (b * nb + i, 0, 0)
    prev_t = lambda b, i: (b * nb + jnp.maximum(i - 1, 0), 0, 0)
    kv_width = SWA_KV_HEADS * LANES
    sink_rows = _swa_sink_rows(sinks)
    return pl.pallas_call(
        _swa_attn_kernel,
        grid=(batch, nb),
        in_specs=[pl.BlockSpec(sink_rows.shape, lambda b, i: (0, 0)),
                  pl.BlockSpec((BLOCK, SWA_WIDTH), cur),
                  pl.BlockSpec((BLOCK, kv_width), cur),
                  pl.BlockSpec((BLOCK, kv_width), prev),
                  pl.BlockSpec((None, SWA_KV_WIDTH, BLOCK), cur_t),
                  pl.BlockSpec((None, SWA_KV_WIDTH, BLOCK), prev_t),
                  pl.BlockSpec((BLOCK, SWA_WIDTH), cur)],
        out_specs=pl.BlockSpec((BLOCK, SWA_WIDTH), cur),
        out_shape=jax.ShapeDtypeStruct((tokens, SWA_WIDTH), BF16),
        compiler_params=pltpu.CompilerParams(
            dimension_semantics=("arbitrary", "arbitrary"), vmem_limit_bytes=VMEM_LIMIT_BYTES),
        name="swa_attn",
    )(sink_rows, q, kd, kd, vt, vt, g)


def _out_kernel(x_ref, y_ref, w_ref, fn_ref, o_ref, w_sc, *, final):
    @pl.when(pl.program_id(0) == 0)
    def _():
        w_sc[...] = w_ref[...].astype(BF16)

    y = _load_groups(y_ref) if len(y_ref.shape) == 3 else y_ref[...]
    x_new = x_ref[...] + _dot(y, w_sc[...])
    o_ref[...] = _rmsnorm(x_new, fn_ref[...]) if final else x_new


def _out_proj(x, y, w_out, j, final_norm, final):
    tokens = x.shape[0]
    rows = OUT_ROWS
    row_spec = lambda width: pl.BlockSpec((rows, width), lambda i: (i, 0))
    if y.ndim == 3:
        y_spec = pl.BlockSpec((y.shape[0], rows, y.shape[2]), lambda i: (0, i, 0))
    else:
        y_spec = row_spec(y.shape[1])
    return pl.pallas_call(
        functools.partial(_out_kernel, final=final),
        grid=(tokens // rows,),
        in_specs=[row_spec(D_MODEL), y_spec, _layer_spec(w_out, j),
                  _const_spec(final_norm.shape)],
        out_specs=row_spec(D_MODEL),
        out_shape=jax.ShapeDtypeStruct((tokens, D_MODEL), F32),
        scratch_shapes=[pltpu.VMEM(w_out.shape[1:], BF16)],
        compiler_params=pltpu.CompilerParams(
            dimension_semantics=("arbitrary",), vmem_limit_bytes=VMEM_LIMIT_BYTES),
        name="out_proj",
    )(x, y, w_out, final_norm)


def kernel(x, positions, layer_norm, mla_w_in, mla_q_norm, mla_w_uq, mla_kv_norm, mla_w_ukv,
           mla_w_out, swa_w_in, swa_sinks, swa_w_out, final_norm):
    batch, seq, d_model = x.shape
    tokens = batch * seq
    assert d_model == D_MODEL and seq % FLASH_BLOCK == 0 and tokens % OUT_ROWS == 0
    assert FLASH_BLOCK % MLA_PROJ_ROWS == 0 and SWA_PROJ_ROWS % BLOCK == 0
    mcos, msin, sc, s1, s2 = _rope_tables(positions)
    xt = x.reshape(tokens, d_model)
    fn = final_norm.reshape(1, -1)
    ln_stack = layer_norm[:, None, :]
    mla_w = {"w_in": mla_w_in.astype(BF16), "w_uq": mla_w_uq.astype(BF16),
             "w_ukv": mla_w_ukv.astype(BF16), "q_norm": mla_q_norm[:, None, :],
             "kv_norm": mla_kv_norm[:, None, :]}
    swa_w = swa_w_in.astype(BF16)
    for i in range(DEPTH):
        j = i // N_MIXERS
        final = i == DEPTH - 1
        if i % N_MIXERS == 0:
            qn, qr, kn, vt, krp, g = _mla_in(xt, ln_stack, i, mla_w, j, mcos, msin)
            y = _mla_flash(qn, qr, kn, krp, vt, g, batch, seq)
            w_out = mla_w_out
        else:
            q, kd, vt, g = _swa_in(xt, ln_stack, i, swa_w, j, sc, s1, s2)
            y = _swa_attn(swa_sinks[j], q, kd, vt, g, batch, seq)
            w_out = swa_w_out
        xt = _out_proj(xt, y, w_out, j, fn, final)
    return xt.reshape(batch, seq, d_model)
```

```python
import functools
import math

import jax
import jax.numpy as jnp
from jax import lax
from jax.experimental import pallas as pl
from jax.experimental.pallas import tpu as pltpu

D_MODEL = 2048
DEPTH = 4
N_MIXERS = 2
ROPE_THETA = 500000.0
NORM_EPS = 1e-6
BLOCK = 128
NEG_INF = -1e30

MLA_HEADS = 16
MLA_Q_RANK = 512
MLA_KV_RANK = 512
MLA_NOPE = 128
MLA_ROPE = 64
MLA_V = 128
MLA_WIDTH = MLA_HEADS * MLA_V

SWA_Q_HEADS = 32
SWA_KV_HEADS = 4
SWA_GROUP = SWA_Q_HEADS // SWA_KV_HEADS
SWA_HEAD_DIM = 64
SWA_ROPE_DIM = SWA_HEAD_DIM // 4
SWA_WIDTH = SWA_Q_HEADS * SWA_HEAD_DIM
SWA_KV_WIDTH = SWA_KV_HEADS * SWA_HEAD_DIM

LANES = 128
VMEM_LIMIT_BYTES = 58 * 1024 * 1024

MLA_PROJ_ROWS = 512
SWA_PROJ_ROWS = 512
OUT_ROWS = 512
FLASH_BLOCK = 512
SWA_Q_BLOCKS = 4

BF16 = jnp.bfloat16
F32 = jnp.float32


def _dot(a, b):
    return jnp.dot(a, b, preferred_element_type=F32)


def _dot_nt(a, b):
    return lax.dot_general(a, b, (((1,), (1,)), ((), ())), preferred_element_type=F32)


def _rmsnorm(xf, g):
    y = xf * lax.rsqrt(jnp.mean(xf * xf, axis=-1, keepdims=True) + NORM_EPS)
    return y * g


def _tile_lanes(t, width):
    reps = width // t.shape[1]
    return t if reps == 1 else jnp.concatenate([t] * reps, axis=1)


def _lane_index(shape):
    return lax.broadcasted_iota(jnp.int32, shape, 1)


def _rope_half64(x, cos_t, sin_t):
    w = x.shape[1]
    fwd = pltpu.roll(x, w - MLA_ROPE // 2, 1)
    bwd = pltpu.roll(x, MLA_ROPE // 2, 1)
    first_half = (_lane_index(x.shape) % MLA_ROPE) < (MLA_ROPE // 2)
    swapped = jnp.where(first_half, fwd, bwd)
    return x * _tile_lanes(cos_t, w) + swapped * _tile_lanes(sin_t, w)


def _rope_partial(x, c_t, s1_t, s2_t):
    w = x.shape[1]
    half = SWA_ROPE_DIM // 2
    fwd = pltpu.roll(x, w - half, 1)
    bwd = pltpu.roll(x, half, 1)
    return (x * _tile_lanes(c_t, w) + fwd * _tile_lanes(s1_t, w)
            + bwd * _tile_lanes(s2_t, w))


def _rope_table_kernel(pos_ref, const_ref, mcos_ref, msin_ref, sc_ref, s1_ref, s2_ref):
    pos = pos_ref[...].astype(F32)
    ang_m = pos * const_ref[0:1, :]
    mcos_ref[...] = jnp.cos(ang_m)
    msin_ref[...] = jnp.sin(ang_m) * const_ref[1:2, :]
    ang_s = pos * const_ref[2:3, :]
    sin_s = jnp.sin(ang_s)
    sc_ref[...] = jnp.cos(ang_s)
    s1_ref[...] = sin_s * const_ref[3:4, :]
    s2_ref[...] = sin_s * const_ref[4:5, :]


def _rope_constants():
    lane = jnp.arange(LANES)
    f_mla = ROPE_THETA ** (-jnp.arange(0, MLA_ROPE, 2, dtype=F32) / MLA_ROPE)
    f_swa = ROPE_THETA ** (-jnp.arange(0, SWA_ROPE_DIM, 2, dtype=F32) / SWA_ROPE_DIM)
    half_m = MLA_ROPE // 2
    half_s = SWA_ROPE_DIM // 2
    d_m = lane % MLA_ROPE
    d_s = lane % SWA_HEAD_DIM
    rows = [
        f_mla[d_m % half_m],
        jnp.where(d_m < half_m, -1.0, 1.0),
        jnp.where(d_s < SWA_ROPE_DIM, f_swa[d_s % half_s], 0.0),
        jnp.where(d_s < half_s, -1.0, 0.0),
        jnp.where((d_s >= half_s) & (d_s < SWA_ROPE_DIM), 1.0, 0.0),
    ]
    rows += [jnp.zeros((LANES,), F32)] * 3
    return jnp.stack([r.astype(F32) for r in rows])


def _rope_tables(positions):
    tokens = positions.size
    rows = 1024
    tab = jax.ShapeDtypeStruct((tokens, LANES), F32)
    spec = pl.BlockSpec((rows, LANES), lambda i: (i, 0))
    return pl.pallas_call(
        _rope_table_kernel,
        grid=(tokens // rows,),
        in_specs=[pl.BlockSpec((rows, 1), lambda i: (i, 0)),
                  pl.BlockSpec((8, LANES), lambda i: (0, 0))],
        out_specs=[spec] * 5,
        out_shape=[tab] * 5,
        name="rope_tables",
    )(positions.reshape(tokens, 1), _rope_constants())


def _mla_in_kernel(x_ref, ln_ref, win_ref, qnorm_ref, kvnorm_ref, wuq_ref, wukv_ref,
                   cos_ref, sin_ref, qn_ref, qr_ref, kn_ref, vt_ref, krp_ref, g_ref):
    scale = (MLA_NOPE + MLA_ROPE) ** -0.5 * math.log2(math.e)
    h = _rmsnorm(x_ref[...], ln_ref[...]).astype(BF16)
    cos_t = cos_ref[...]
    sin_t = sin_ref[...]
    rank2 = MLA_Q_RANK + MLA_KV_RANK
    heads = range(MLA_HEADS)

    c = _dot(h, win_ref[:, :rank2])
    cq = _rmsnorm(c[:, :MLA_Q_RANK], qnorm_ref[...]).astype(BF16)
    ckv = _rmsnorm(c[:, MLA_Q_RANK:], kvnorm_ref[...]).astype(BF16)

    tail = _dot(h, win_ref[:, rank2:])
    kr = tail[:, :MLA_ROPE]
    kr = _rope_half64(jnp.concatenate([kr, kr], axis=1), cos_t, sin_t)
    low = _lane_index(kr.shape) < MLA_ROPE
    krp_ref[:, :LANES] = jnp.where(low, kr, 0.0).astype(BF16)
    krp_ref[:, LANES:] = jnp.where(low, 0.0, kr).astype(BF16)

    gate = tail[:, MLA_ROPE:]
    _store_groups(g_ref, (gate * jax.nn.sigmoid(gate)).astype(BF16))

    q = _dot(cq, wuq_ref[...])
    qd = MLA_NOPE + MLA_ROPE
    qn = jnp.concatenate([q[:, hh * qd:hh * qd + MLA_NOPE] for hh in heads], axis=1)
    qr = jnp.concatenate([q[:, hh * qd + MLA_NOPE:(hh + 1) * qd] for hh in heads], axis=1)
    _store_groups(qn_ref, (qn * scale).astype(BF16))
    _store_groups(qr_ref, (_rope_half64(qr, cos_t, sin_t) * scale).astype(BF16))

    kv = _dot(ckv, wukv_ref[...])
    kd = MLA_NOPE + MLA_V
    kn = jnp.concatenate([kv[:, hh * kd:hh * kd + MLA_NOPE] for hh in heads], axis=1)
    v = jnp.concatenate([kv[:, hh * kd + MLA_NOPE:(hh + 1) * kd] for hh in heads], axis=1)
    _store_groups(kn_ref, kn.astype(BF16))
    vt_ref[0] = v.T.astype(BF16)


def _store_groups(ref, val):
    groups, _, width = ref.shape
    for gg in range(groups):
        ref[gg] = val[:, gg * width:(gg + 1) * width]


def _load_groups(ref):
    return jnp.concatenate([ref[gg] for gg in range(ref.shape[0])], axis=1)


def _const_spec(shape):
    return pl.BlockSpec(shape, lambda i: (0,) * len(shape), pipeline_mode=pl.Buffered(1))


def _layer_spec(stacked, layer):
    tail = (0,) * (stacked.ndim - 1)
    return pl.BlockSpec((None,) + stacked.shape[1:], lambda i: (layer,) + tail,
                        pipeline_mode=pl.Buffered(1))


def _mla_in(x, layer_norm, layer, w, j, cos_t, sin_t):
    tokens = x.shape[0]
    rows = MLA_PROJ_ROWS
    row_spec = lambda width: pl.BlockSpec((rows, width), lambda i: (i, 0))
    consts = [(layer_norm, layer), (w["w_in"], j), (w["q_norm"], j), (w["kv_norm"], j),
              (w["w_uq"], j), (w["w_ukv"], j)]
    pairs = MLA_HEADS // 2
    per_blk = FLASH_BLOCK // rows

    def pair_major(width):
        return (pl.BlockSpec((pairs, rows, width), lambda i: (0, i, 0)),
                jax.ShapeDtypeStruct((pairs, tokens, width), BF16))

    outs = [pair_major(2 * MLA_NOPE),
            pair_major(2 * MLA_ROPE),
            pair_major(2 * MLA_NOPE),
            (pl.BlockSpec((1, MLA_WIDTH, rows), lambda i: (i // per_blk, 0, i % per_blk)),
             jax.ShapeDtypeStruct((tokens // FLASH_BLOCK, MLA_WIDTH, FLASH_BLOCK), BF16)),
            (row_spec(2 * LANES), jax.ShapeDtypeStruct((tokens, 2 * LANES), BF16)),
            pair_major(2 * MLA_V)]
    return pl.pallas_call(
        _mla_in_kernel,
        grid=(tokens // rows,),
        in_specs=[row_spec(D_MODEL)] + [_layer_spec(c, ll) for c, ll in consts]
                 + [row_spec(LANES), row_spec(LANES)],
        out_specs=[o[0] for o in outs],
        out_shape=[o[1] for o in outs],
        compiler_params=pltpu.CompilerParams(
            dimension_semantics=("arbitrary",), vmem_limit_bytes=VMEM_LIMIT_BYTES),
        name="mla_in",
    )(x, *[c for c, _ in consts], cos_t, sin_t)


def _mla_flash_kernel(qn_ref, qr_ref, kn_ref, kr_ref, vt_ref, g_ref, o_ref,
                      kcat_sc, s_sc, m_sc, acc_sc, *, seq):
    blk = FLASH_BLOCK
    pair = (0, 1)
    hcols = [slice(hh * LANES, (hh + 1) * LANES) for hh in pair]
    for hh in pair:
        kcat_sc[hh, :, :LANES] = kn_ref[:, hcols[hh]]
        kcat_sc[hh, :, LANES:] = kr_ref[:, hcols[hh]]
    ones_rows = jnp.ones((16, blk), BF16)
    causal_t = (lax.broadcasted_iota(jnp.int32, (blk, blk), 0)
                <= lax.broadcasted_iota(jnp.int32, (blk, blk), 1))

    for qi in range(seq // blk):
        rows = slice(qi * blk, (qi + 1) * blk)
        qs = [jnp.concatenate([qn_ref[rows, hcols[hh]], qr_ref[rows, :]], axis=1) for hh in pair]
        def scores(hh, kb, slot, masked, qs=qs):
            kstart = pl.multiple_of(kb * blk, blk)
            st = _dot_nt(kcat_sc[hh, pl.ds(kstart, blk), :], qs[hh])
            s_sc[hh, slot] = jnp.where(causal_t, st, NEG_INF) if masked else st

        def accumulate(hh, kb, slot, first):
            st = s_sc[hh, slot]
            m_prev = jnp.float32(NEG_INF) if first else m_sc[hh]
            m_new = jnp.maximum(m_prev, jnp.max(st, axis=0, keepdims=True))
            p = jnp.exp2(st - m_new)
            vt_aug = jnp.concatenate([vt_ref[kb, hcols[hh], :], ones_rows], axis=0)
            pv = _dot(vt_aug, p.astype(BF16))
            acc_sc[hh] = pv if first else jnp.exp2(m_prev - m_new) * acc_sc[hh] + pv
            m_sc[hh] = m_new

        visits = [qi] + list(range(qi))
        for hh in pair:
            scores(hh, visits[0], 0, True)
        for i, kb in enumerate(visits):
            for hh in pair:
                if i + 1 < len(visits):
                    scores(hh, visits[i + 1], (i + 1) % 2, False)
                accumulate(hh, kb, i % 2, i == 0)

        for hh in pair:
            acc = acc_sc[hh]
            y = (acc[:MLA_V] * (1.0 / acc[MLA_V:MLA_V + 1])).T
            o_ref[rows, hcols[hh]] = (y * g_ref[rows, hcols[hh]].astype(F32)).astype(BF16)


def _mla_flash(qn, qr, kn, krp, vt, g, batch, seq):
    tokens = batch * seq
    nblk = seq // FLASH_BLOCK
    pair_spec = pl.BlockSpec((None, seq, 2 * LANES), lambda b, p: (p, b, 0))
    return pl.pallas_call(
        functools.partial(_mla_flash_kernel, seq=seq),
        grid=(batch, MLA_HEADS // 2),
        in_specs=[pair_spec,
                  pl.BlockSpec((None, seq, LANES), lambda b, p: (p, b, 0)),
                  pair_spec,
                  pl.BlockSpec((seq, 2 * LANES), lambda b, p: (b, 0)),
                  pl.BlockSpec((nblk, 2 * LANES, FLASH_BLOCK), lambda b, p: (b, p, 0)),
                  pair_spec],
        out_specs=pair_spec,
        out_shape=jax.ShapeDtypeStruct((MLA_HEADS // 2, tokens, 2 * MLA_V), BF16),
        scratch_shapes=[pltpu.VMEM((2, seq, 2 * LANES), BF16),
                        pltpu.VMEM((2, 2, FLASH_BLOCK, FLASH_BLOCK), F32),
                        pltpu.VMEM((2, 1, FLASH_BLOCK), F32),
                        pltpu.VMEM((2, MLA_V + 16, FLASH_BLOCK), F32)],
        compiler_params=pltpu.CompilerParams(
            dimension_semantics=("arbitrary", "arbitrary"), vmem_limit_bytes=VMEM_LIMIT_BYTES),
        name="mla_flash",
    )(qn, qr, kn, krp, vt, g)


def _dup_heads(t):
    parts = []
    for hh in range(t.shape[1] // SWA_HEAD_DIM):
        piece = t[:, hh * SWA_HEAD_DIM:(hh + 1) * SWA_HEAD_DIM]
        parts += [piece, piece]
    return jnp.concatenate(parts, axis=1)


def _swa_in_kernel(x_ref, ln_ref, win_ref, c_ref, s1_ref, s2_ref,
                   q_ref, kd_ref, vt_ref, g_ref):
    scale = SWA_HEAD_DIM ** -0.5 * math.log2(math.e)
    o1 = SWA_WIDTH
    o2 = o1 + SWA_KV_WIDTH
    o3 = o2 + SWA_KV_WIDTH
    h = _rmsnorm(x_ref[...], ln_ref[...]).astype(BF16)
    c_t, s1_t, s2_t = c_ref[...], s1_ref[...], s2_ref[...]
    q = _rope_partial(_dot(h, win_ref[:, :o1]), c_t, s1_t, s2_t)
    q_ref[...] = (q * scale).astype(BF16)
    kv = _dot(h, win_ref[:, o1:o3])
    k = _rope_partial(kv[:, :SWA_KV_WIDTH], c_t, s1_t, s2_t)
    kd_ref[...] = _dup_heads(k).astype(BF16)
    vt = kv[:, SWA_KV_WIDTH:].T.astype(BF16)
    for cc in range(vt_ref.shape[0]):
        vt_ref[cc] = vt[:, cc * BLOCK:(cc + 1) * BLOCK]
    gate = _dot(h, win_ref[:, o3:])
    g_ref[...] = (gate * jax.nn.sigmoid(gate)).astype(BF16)


def _swa_in(x, layer_norm, layer, w_in, j, c_t, s1_t, s2_t):
    tokens = x.shape[0]
    rows = SWA_PROJ_ROWS
    row_spec = lambda width: pl.BlockSpec((rows, width), lambda i: (i, 0))
    consts = [(layer_norm, layer), (w_in, j)]
    out_widths = [SWA_WIDTH, SWA_KV_HEADS * LANES, None, SWA_WIDTH]
    vt_spec = pl.BlockSpec((rows // BLOCK, SWA_KV_WIDTH, BLOCK), lambda i: (i, 0, 0))
    vt_shape = jax.ShapeDtypeStruct((tokens // BLOCK, SWA_KV_WIDTH, BLOCK), BF16)
    return pl.pallas_call(
        _swa_in_kernel,
        grid=(tokens // rows,),
        in_specs=[row_spec(D_MODEL)] + [_layer_spec(c, ll) for c, ll in consts]
                 + [row_spec(LANES)] * 3,
        out_specs=[vt_spec if wd is None else row_spec(wd) for wd in out_widths],
        out_shape=[vt_shape if wd is None else jax.ShapeDtypeStruct((tokens, wd), BF16)
                   for wd in out_widths],
        compiler_params=pltpu.CompilerParams(
            dimension_semantics=("arbitrary",), vmem_limit_bytes=VMEM_LIMIT_BYTES),
        name="swa_in",
    )(x, *[c for c, _ in consts], c_t, s1_t, s2_t)


def _swa_attn_kernel(sink_ref, q_ref, kc_ref, kp_ref, vtc_ref, vtp_ref, g_ref, o_ref):
    first_has_prev = pl.program_id(1) > 0
    pairs = SWA_GROUP // 2
    key_i = lax.broadcasted_iota(jnp.int32, (BLOCK, pairs * BLOCK), 0)
    qry_i = lax.broadcasted_iota(jnp.int32, (BLOCK, pairs * BLOCK), 1) % BLOCK
    in_cur = key_i <= qry_i
    cur_w = jnp.where(in_cur, 1.0, 0.0).astype(BF16)
    low = _lane_index((BLOCK, LANES)) < SWA_HEAD_DIM
    zero = jnp.zeros((BLOCK, LANES), BF16)

    def block_rows(qb):
        return slice(qb * BLOCK, (qb + 1) * BLOCK)

    def scores(qb, kvh, sub):
        cols = slice(kvh * LANES, (kvh + 1) * LANES)
        kc = kc_ref[block_rows(qb), cols]
        kp = kp_ref[:, cols] if qb == 0 else kc_ref[block_rows(qb - 1), cols]
        if sub == 0:
            k_half = jnp.concatenate([jnp.where(low, kp, zero), jnp.where(low, kc, zero)], axis=0)
        else:
            k_half = jnp.concatenate([jnp.where(low, zero, kp), jnp.where(low, zero, kc)], axis=0)
        head0 = kvh * SWA_GROUP
        q_rows = jnp.concatenate(
            [q_ref[block_rows(qb),
                   (head0 + 2 * pp) * SWA_HEAD_DIM:(head0 + 2 * pp + 2) * SWA_HEAD_DIM]
             for pp in range(pairs)], axis=0)
        return _dot_nt(k_half, q_rows)

    def attend(qb, kvh, sub, st):
        s_prev = jnp.where(first_has_prev, st[:BLOCK], NEG_INF) if qb == 0 else st[:BLOCK]
        s = jnp.where(in_cur, st[BLOCK:], s_prev)
        sink = sink_ref[sub * SWA_KV_HEADS + kvh:sub * SWA_KV_HEADS + kvh + 1, :]
        m = jnp.maximum(jnp.max(s, axis=0, keepdims=True), sink)
        e = jnp.exp2(s - m)
        denom = jnp.sum(e, axis=0, keepdims=True) + jnp.exp2(sink - m)
        p = e.astype(BF16)
        p_cur = p * cur_w
        pt = jnp.concatenate([p - p_cur, p_cur], axis=0)
        vrows = slice(kvh * SWA_HEAD_DIM, (kvh + 1) * SWA_HEAD_DIM)
        vtp = vtp_ref[vrows, :] if qb == 0 else vtc_ref[qb - 1, vrows, :]
        vt = jnp.concatenate([vtp, vtc_ref[qb, vrows, :]], axis=1)
        return _dot(vt, pt) * (1.0 / denom)

    chains = [(qb, kvh, sub) for qb in range(SWA_Q_BLOCKS) for kvh in range(SWA_KV_HEADS)
              for sub in range(2)]
    st_next = scores(*chains[0])
    outs = {}
    for ci, (qb, kvh, sub) in enumerate(chains):
        st = st_next
        if ci + 1 < len(chains):
            st_next = scores(*chains[ci + 1])
        outs[sub] = attend(qb, kvh, sub, st)
        if sub == 1:
            head0 = kvh * SWA_GROUP
            for pp in range(pairs):
                qc = slice(pp * BLOCK, (pp + 1) * BLOCK)
                y = jnp.concatenate([outs[0][:, qc], outs[1][:, qc]], axis=0).T
                pcols = slice((head0 + 2 * pp) * SWA_HEAD_DIM,
                              (head0 + 2 * pp + 2) * SWA_HEAD_DIM)
                o_ref[block_rows(qb), pcols] = (
                    y * g_ref[block_rows(qb), pcols].astype(F32)).astype(BF16)


def _swa_sink_rows(sinks):
    s2 = (sinks * math.log2(math.e)).reshape(SWA_KV_HEADS, SWA_GROUP // 2, 2)
    rows = jnp.transpose(s2, (2, 0, 1)).reshape(2 * SWA_KV_HEADS, SWA_GROUP // 2, 1)
    return jnp.broadcast_to(rows, (2 * SWA_KV_HEADS, SWA_GROUP // 2, BLOCK)).reshape(
        2 * SWA_KV_HEADS, -1)


def _swa_attn(sinks, q, kd, vt, g, batch, seq):
    tokens = batch * seq
    nb = seq // BLOCK
    steps = nb // SWA_Q_BLOCKS
    rows = SWA_Q_BLOCKS * BLOCK
    cur = lambda b, i: (b * steps + i, 0)
    prev = lambda b, i: (b * nb + jnp.maximum(i * SWA_Q_BLOCKS - 1, 0), 0)
    cur_t = lambda b, i: (b * steps + i, 0, 0)
    prev_t = lambda b, i: (b * nb + jnp.maximum(i * SWA_Q_BLOCKS - 1, 0), 0, 0)
    kv_width = SWA_KV_HEADS * LANES
    sink_rows = _swa_sink_rows(sinks)
    return pl.pallas_call(
        _swa_attn_kernel,
        grid=(batch, steps),
        in_specs=[pl.BlockSpec(sink_rows.shape, lambda b, i: (0, 0)),
                  pl.BlockSpec((rows, SWA_WIDTH), cur),
                  pl.BlockSpec((rows, kv_width), cur),
                  pl.BlockSpec((BLOCK, kv_width), prev),
                  pl.BlockSpec((SWA_Q_BLOCKS, SWA_KV_WIDTH, BLOCK), cur_t),
                  pl.BlockSpec((None, SWA_KV_WIDTH, BLOCK), prev_t),
                  pl.BlockSpec((rows, SWA_WIDTH), cur)],
        out_specs=pl.BlockSpec((rows, SWA_WIDTH), cur),
        out_shape=jax.ShapeDtypeStruct((tokens, SWA_WIDTH), BF16),
        compiler_params=pltpu.CompilerParams(
            dimension_semantics=("arbitrary", "arbitrary"), vmem_limit_bytes=VMEM_LIMIT_BYTES),
        name="swa_attn",
    )(sink_rows, q, kd, kd, vt, vt, g)


def _out_kernel(x_ref, y_ref, w_ref, fn_ref, o_ref, w_sc, *, final):
    @pl.when(pl.program_id(0) == 0)
    def _():
        w_sc[...] = w_ref[...].astype(BF16)

    y = _load_groups(y_ref) if len(y_ref.shape) == 3 else y_ref[...]
    x_new = x_ref[...] + _dot(y, w_sc[...])
    o_ref[...] = _rmsnorm(x_new, fn_ref[...]) if final else x_new


def _out_proj(x, y, w_out, j, final_norm, final):
    tokens = x.shape[0]
    rows = OUT_ROWS
    row_spec = lambda width: pl.BlockSpec((rows, width), lambda i: (i, 0))
    if y.ndim == 3:
        y_spec = pl.BlockSpec((y.shape[0], rows, y.shape[2]), lambda i: (0, i, 0))
    else:
        y_spec = row_spec(y.shape[1])
    return pl.pallas_call(
        functools.partial(_out_kernel, final=final),
        grid=(tokens // rows,),
        in_specs=[row_spec(D_MODEL), y_spec, _layer_spec(w_out, j),
                  _const_spec(final_norm.shape)],
        out_specs=row_spec(D_MODEL),
        out_shape=jax.ShapeDtypeStruct((tokens, D_MODEL), F32),
        scratch_shapes=[pltpu.VMEM(w_out.shape[1:], BF16)],
        compiler_params=pltpu.CompilerParams(
            dimension_semantics=("arbitrary",), vmem_limit_bytes=VMEM_LIMIT_BYTES),
        name="out_proj",
    )(x, y, w_out, final_norm)


def kernel(x, positions, layer_norm, mla_w_in, mla_q_norm, mla_w_uq, mla_kv_norm, mla_w_ukv,
           mla_w_out, swa_w_in, swa_sinks, swa_w_out, final_norm):
    batch, seq, d_model = x.shape
    tokens = batch * seq
    assert d_model == D_MODEL and seq % FLASH_BLOCK == 0 and tokens % OUT_ROWS == 0
    assert FLASH_BLOCK % MLA_PROJ_ROWS == 0 and SWA_PROJ_ROWS % BLOCK == 0
    assert seq % (SWA_Q_BLOCKS * BLOCK) == 0
    mcos, msin, sc, s1, s2 = _rope_tables(positions)
    xt = x.reshape(tokens, d_model)
    fn = final_norm.reshape(1, -1)
    ln_stack = layer_norm[:, None, :]
    mla_w = {"w_in": mla_w_in.astype(BF16), "w_uq": mla_w_uq.astype(BF16),
             "w_ukv": mla_w_ukv.astype(BF16), "q_norm": mla_q_norm[:, None, :],
             "kv_norm": mla_kv_norm[:, None, :]}
    swa_w = swa_w_in.astype(BF16)
    for i in range(DEPTH):
        j = i // N_MIXERS
        final = i == DEPTH - 1
        if i % N_MIXERS == 0:
            qn, qr, kn, vt, krp, g = _mla_in(xt, ln_stack, i, mla_w, j, mcos, msin)
            y = _mla_flash(qn, qr, kn, krp, vt, g, batch, seq)
            w_out = mla_w_out
        else:
            q, kd, vt, g = _swa_in(xt, ln_stack, i, swa_w, j, sc, s1, s2)
            y = _swa_attn(swa_sinks[j], q, kd, vt, g, batch, seq)
            w_out = swa_w_out
        xt = _out_proj(xt, y, w_out, j, fn, final)
    return xt.reshape(batch, seq, d_model)
```

```python
import functools
import math

import jax
import jax.numpy as jnp
from jax import lax
from jax.experimental import pallas as pl
from jax.experimental.pallas import tpu as pltpu

D_MODEL = 2048
DEPTH = 4
N_MIXERS = 2
ROPE_THETA = 500000.0
NORM_EPS = 1e-6
BLOCK = 128
NEG_INF = -1e30

MLA_HEADS = 16
MLA_Q_RANK = 512
MLA_KV_RANK = 512
MLA_NOPE = 128
MLA_ROPE = 64
MLA_V = 128
MLA_WIDTH = MLA_HEADS * MLA_V

SWA_Q_HEADS = 32
SWA_KV_HEADS = 4
SWA_GROUP = SWA_Q_HEADS // SWA_KV_HEADS
SWA_HEAD_DIM = 64
SWA_ROPE_DIM = SWA_HEAD_DIM // 4
SWA_WIDTH = SWA_Q_HEADS * SWA_HEAD_DIM
SWA_KV_WIDTH = SWA_KV_HEADS * SWA_HEAD_DIM

LANES = 128
VMEM_LIMIT_BYTES = 58 * 1024 * 1024

MLA_PROJ_ROWS = 512
SWA_PROJ_ROWS = 512
OUT_ROWS = 512
FLASH_BLOCK = 512
SWA_Q_BLOCKS = 4

BF16 = jnp.bfloat16
F32 = jnp.float32


def _dot(a, b):
    return jnp.dot(a, b, preferred_element_type=F32)


def _dot_nt(a, b):
    return lax.dot_general(a, b, (((1,), (1,)), ((), ())), preferred_element_type=F32)


def _rmsnorm(xf, g):
    y = xf * lax.rsqrt(jnp.mean(xf * xf, axis=-1, keepdims=True) + NORM_EPS)
    return y * g


def _tile_lanes(t, width):
    reps = width // t.shape[1]
    return t if reps == 1 else jnp.concatenate([t] * reps, axis=1)


def _lane_index(shape):
    return lax.broadcasted_iota(jnp.int32, shape, 1)


def _rope_half64(x, cos_t, sin_t):
    w = x.shape[1]
    fwd = pltpu.roll(x, w - MLA_ROPE // 2, 1)
    bwd = pltpu.roll(x, MLA_ROPE // 2, 1)
    first_half = (_lane_index(x.shape) % MLA_ROPE) < (MLA_ROPE // 2)
    swapped = jnp.where(first_half, fwd, bwd)
    return x * _tile_lanes(cos_t, w) + swapped * _tile_lanes(sin_t, w)


def _rope_partial(x, c_t, s1_t, s2_t):
    w = x.shape[1]
    half = SWA_ROPE_DIM // 2
    fwd = pltpu.roll(x, w - half, 1)
    bwd = pltpu.roll(x, half, 1)
    return (x * _tile_lanes(c_t, w) + fwd * _tile_lanes(s1_t, w)
            + bwd * _tile_lanes(s2_t, w))


def _rope_table_kernel(pos_ref, const_ref, mcos_ref, msin_ref, sc_ref, s1_ref, s2_ref):
    pos = pos_ref[...].astype(F32)
    ang_m = pos * const_ref[0:1, :]
    mcos_ref[...] = jnp.cos(ang_m)
    msin_ref[...] = jnp.sin(ang_m) * const_ref[1:2, :]
    ang_s = pos * const_ref[2:3, :]
    sin_s = jnp.sin(ang_s)
    sc_ref[...] = jnp.cos(ang_s)
    s1_ref[...] = sin_s * const_ref[3:4, :]
    s2_ref[...] = sin_s * const_ref[4:5, :]


def _rope_constants():
    lane = jnp.arange(LANES)
    f_mla = ROPE_THETA ** (-jnp.arange(0, MLA_ROPE, 2, dtype=F32) / MLA_ROPE)
    f_swa = ROPE_THETA ** (-jnp.arange(0, SWA_ROPE_DIM, 2, dtype=F32) / SWA_ROPE_DIM)
    half_m = MLA_ROPE // 2
    half_s = SWA_ROPE_DIM // 2
    d_m = lane % MLA_ROPE
    d_s = lane % SWA_HEAD_DIM
    rows = [
        f_mla[d_m % half_m],
        jnp.where(d_m < half_m, -1.0, 1.0),
        jnp.where(d_s < SWA_ROPE_DIM, f_swa[d_s % half_s], 0.0),
        jnp.where(d_s < half_s, -1.0, 0.0),
        jnp.where((d_s >= half_s) & (d_s < SWA_ROPE_DIM), 1.0, 0.0),
    ]
    rows += [jnp.zeros((LANES,), F32)] * 3
    return jnp.stack([r.astype(F32) for r in rows])


def _rope_tables(positions):
    tokens = positions.size
    rows = 1024
    tab = jax.ShapeDtypeStruct((tokens, LANES), F32)
    spec = pl.BlockSpec((rows, LANES), lambda i: (i, 0))
    return pl.pallas_call(
        _rope_table_kernel,
        grid=(tokens // rows,),
        in_specs=[pl.BlockSpec((rows, 1), lambda i: (i, 0)),
                  pl.BlockSpec((8, LANES), lambda i: (0, 0))],
        out_specs=[spec] * 5,
        out_shape=[tab] * 5,
        name="rope_tables",
    )(positions.reshape(tokens, 1), _rope_constants())


def _mla_in_kernel(x_ref, ln_ref, win_ref, qnorm_ref, kvnorm_ref, wuq_ref, wukv_ref,
                   cos_ref, sin_ref, qn_ref, qr_ref, kn_ref, vt_ref, krp_ref, g_ref):
    scale = (MLA_NOPE + MLA_ROPE) ** -0.5 * math.log2(math.e)
    h = _rmsnorm(x_ref[...], ln_ref[...]).astype(BF16)
    cos_t = cos_ref[...]
    sin_t = sin_ref[...]
    rank2 = MLA_Q_RANK + MLA_KV_RANK
    heads = range(MLA_HEADS)

    c = _dot(h, win_ref[:, :rank2])
    cq = _rmsnorm(c[:, :MLA_Q_RANK], qnorm_ref[...]).astype(BF16)
    ckv = _rmsnorm(c[:, MLA_Q_RANK:], kvnorm_ref[...]).astype(BF16)

    tail = _dot(h, win_ref[:, rank2:])
    kr = tail[:, :MLA_ROPE]
    kr = _rope_half64(jnp.concatenate([kr, kr], axis=1), cos_t, sin_t)
    low = _lane_index(kr.shape) < MLA_ROPE
    krp_ref[:, :LANES] = jnp.where(low, kr, 0.0).astype(BF16)
    krp_ref[:, LANES:] = jnp.where(low, 0.0, kr).astype(BF16)

    gate = tail[:, MLA_ROPE:]
    _store_groups(g_ref, (gate * jax.nn.sigmoid(gate)).astype(BF16))

    q = _dot(cq, wuq_ref[...])
    qd = MLA_NOPE + MLA_ROPE
    qn = jnp.concatenate([q[:, hh * qd:hh * qd + MLA_NOPE] for hh in heads], axis=1)
    qr = jnp.concatenate([q[:, hh * qd + MLA_NOPE:(hh + 1) * qd] for hh in heads], axis=1)
    _store_groups(qn_ref, (qn * scale).astype(BF16))
    _store_groups(qr_ref, (_rope_half64(qr, cos_t, sin_t) * scale).astype(BF16))

    kv = _dot(ckv, wukv_ref[...])
    kd = MLA_NOPE + MLA_V
    kn = jnp.concatenate([kv[:, hh * kd:hh * kd + MLA_NOPE] for hh in heads], axis=1)
    v = jnp.concatenate([kv[:, hh * kd + MLA_NOPE:(hh + 1) * kd] for hh in heads], axis=1)
    _store_groups(kn_ref, kn.astype(BF16))
    vt_ref[0] = v.T.astype(BF16)


def _store_groups(ref, val):
    groups, _, width = ref.shape
    for gg in range(groups):
        ref[gg] = val[:, gg * width:(gg + 1) * width]


def _load_groups(ref):
    return jnp.concatenate([ref[gg] for gg in range(ref.shape[0])], axis=1)


def _const_spec(shape):
    return pl.BlockSpec(shape, lambda i: (0,) * len(shape), pipeline_mode=pl.Buffered(1))


def _layer_spec(stacked, layer):
    tail = (0,) * (stacked.ndim - 1)
    return pl.BlockSpec((None,) + stacked.shape[1:], lambda i: (layer,) + tail,
                        pipeline_mode=pl.Buffered(1))


def _mla_in(x, layer_norm, layer, w, j, cos_t, sin_t):
    tokens = x.shape[0]
    rows = MLA_PROJ_ROWS
    row_spec = lambda width: pl.BlockSpec((rows, width), lambda i: (i, 0))
    consts = [(layer_norm, layer), (w["w_in"], j), (w["q_norm"], j), (w["kv_norm"], j),
              (w["w_uq"], j), (w["w_ukv"], j)]
    pairs = MLA_HEADS // 2
    per_blk = FLASH_BLOCK // rows

    def pair_major(width):
        return (pl.BlockSpec((pairs, rows, width), lambda i: (0, i, 0)),
                jax.ShapeDtypeStruct((pairs, tokens, width), BF16))

    outs = [pair_major(2 * MLA_NOPE),
            pair_major(2 * MLA_ROPE),
            pair_major(2 * MLA_NOPE),
            (pl.BlockSpec((1, MLA_WIDTH, rows), lambda i: (i // per_blk, 0, i % per_blk)),
             jax.ShapeDtypeStruct((tokens // FLASH_BLOCK, MLA_WIDTH, FLASH_BLOCK), BF16)),
            (row_spec(2 * LANES), jax.ShapeDtypeStruct((tokens, 2 * LANES), BF16)),
            pair_major(2 * MLA_V)]
    return pl.pallas_call(
        _mla_in_kernel,
        grid=(tokens // rows,),
        in_specs=[row_spec(D_MODEL)] + [_layer_spec(c, ll) for c, ll in consts]
                 + [row_spec(LANES), row_spec(LANES)],
        out_specs=[o[0] for o in outs],
        out_shape=[o[1] for o in outs],
        compiler_params=pltpu.CompilerParams(
            dimension_semantics=("arbitrary",), vmem_limit_bytes=VMEM_LIMIT_BYTES),
        name="mla_in",
    )(x, *[c for c, _ in consts], cos_t, sin_t)


def _mla_flash_kernel(qn_ref, qr_ref, kn_ref, kr_ref, vt_ref, g_ref, o_ref,
                      kcat_sc, s_sc, m_sc, acc_sc, *, seq):
    blk = FLASH_BLOCK
    pair = (0, 1)
    hcols = [slice(hh * LANES, (hh + 1) * LANES) for hh in pair]
    for hh in pair:
        kcat_sc[hh, :, :LANES] = kn_ref[:, hcols[hh]]
        kcat_sc[hh, :, LANES:] = kr_ref[:, hcols[hh]]
    ones_rows = jnp.ones((16, blk), BF16)
    causal_t = (lax.broadcasted_iota(jnp.int32, (blk, blk), 0)
                <= lax.broadcasted_iota(jnp.int32, (blk, blk), 1))

    def q_block(qi):
        rows = slice(qi * blk, (qi + 1) * blk)
        return [jnp.concatenate([qn_ref[rows, hcols[hh]], qr_ref[rows, :]], axis=1)
                for hh in pair]

    def scores(hh, qs, kb, slot, masked):
        kstart = pl.multiple_of(kb * blk, blk)
        st = _dot_nt(kcat_sc[hh, pl.ds(kstart, blk), :], qs[hh])
        s_sc[hh, slot] = jnp.where(causal_t, st, NEG_INF) if masked else st

    def accumulate(hh, kb, slot, first):
        st = s_sc[hh, slot]
        m_prev = jnp.float32(NEG_INF) if first else m_sc[hh]
        m_new = jnp.maximum(m_prev, jnp.max(st, axis=0, keepdims=True))
        p = jnp.exp2(st - m_new)
        vt_aug = jnp.concatenate([vt_ref[kb, hcols[hh], :], ones_rows], axis=0)
        pv = _dot(vt_aug, p.astype(BF16))
        acc_sc[hh] = pv if first else jnp.exp2(m_prev - m_new) * acc_sc[hh] + pv
        m_sc[hh] = m_new

    def finish(hh, qi):
        rows = slice(qi * blk, (qi + 1) * blk)
        acc = acc_sc[hh]
        y = (acc[:MLA_V] * (1.0 / acc[MLA_V:MLA_V + 1])).T
        o_ref[rows, hcols[hh]] = (y * g_ref[rows, hcols[hh]].astype(F32)).astype(BF16)

    visits = [(qi, kb) for qi in range(seq // blk) for kb in [qi] + list(range(qi))]
    qs_of = {}
    for n in range(len(visits) + 1):
        for hh in pair:
            if n < len(visits):
                qi, kb = visits[n]
                if qi not in qs_of:
                    qs_of[qi] = q_block(qi)
                scores(hh, qs_of[qi], kb, n % 2, kb == qi)
            if n > 0:
                qi, kb = visits[n - 1]
                accumulate(hh, kb, (n - 1) % 2, kb == qi)
                if n == len(visits) or visits[n][0] != qi:
                    finish(hh, qi)


def _mla_flash(qn, qr, kn, krp, vt, g, batch, seq):
    tokens = batch * seq
    nblk = seq // FLASH_BLOCK
    pair_spec = pl.BlockSpec((None, seq, 2 * LANES), lambda b, p: (p, b, 0))
    return pl.pallas_call(
        functools.partial(_mla_flash_kernel, seq=seq),
        grid=(batch, MLA_HEADS // 2),
        in_specs=[pair_spec,
                  pl.BlockSpec((None, seq, LANES), lambda b, p: (p, b, 0)),
                  pair_spec,
                  pl.BlockSpec((seq, 2 * LANES), lambda b, p: (b, 0)),
                  pl.BlockSpec((nblk, 2 * LANES, FLASH_BLOCK), lambda b, p: (b, p, 0)),
                  pair_spec],
        out_specs=pair_spec,
        out_shape=jax.ShapeDtypeStruct((MLA_HEADS // 2, tokens, 2 * MLA_V), BF16),
        scratch_shapes=[pltpu.VMEM((2, seq, 2 * LANES), BF16),
                        pltpu.VMEM((2, 2, FLASH_BLOCK, FLASH_BLOCK), F32),
                        pltpu.VMEM((2, 1, FLASH_BLOCK), F32),
                        pltpu.VMEM((2, MLA_V + 16, FLASH_BLOCK), F32)],
        compiler_params=pltpu.CompilerParams(
            dimension_semantics=("arbitrary", "arbitrary"), vmem_limit_bytes=VMEM_LIMIT_BYTES),
        name="mla_flash",
    )(qn, qr, kn, krp, vt, g)


def _dup_heads(t):
    parts = []
    for hh in range(t.shape[1] // SWA_HEAD_DIM):
        piece = t[:, hh * SWA_HEAD_DIM:(hh + 1) * SWA_HEAD_DIM]
        parts += [piece, piece]
    return jnp.concatenate(parts, axis=1)


def _swa_in_kernel(x_ref, ln_ref, win_ref, c_ref, s1_ref, s2_ref,
                   q_ref, kd_ref, vt_ref, g_ref):
    scale = SWA_HEAD_DIM ** -0.5 * math.log2(math.e)
    o1 = SWA_WIDTH
    o2 = o1 + SWA_KV_WIDTH
    o3 = o2 + SWA_KV_WIDTH
    half = x_ref.shape[0] // 2
    halves = (slice(0, half), slice(half, 2 * half))

    def norm(rs):
        return _rmsnorm(x_ref[rs, :], ln_ref[...]).astype(BF16)

    def tables(rs):
        return c_ref[rs, :], s1_ref[rs, :], s2_ref[rs, :]

    def finish_q(rs, q):
        q_ref[rs, :] = (_rope_partial(q, *tables(rs)) * scale).astype(BF16)

    def finish_kv(rs, kv, part):
        k = _rope_partial(kv[:, :SWA_KV_WIDTH], *tables(rs))
        kd_ref[rs, :] = _dup_heads(k).astype(BF16)
        vt = kv[:, SWA_KV_WIDTH:].T.astype(BF16)
        per_half = vt_ref.shape[0] // 2
        for cc in range(per_half):
            vt_ref[part * per_half + cc] = vt[:, cc * BLOCK:(cc + 1) * BLOCK]

    def finish_gate(rs, gate):
        g_ref[rs, :] = (gate * jax.nn.sigmoid(gate)).astype(BF16)

    ha = norm(halves[0])
    qa = _dot(ha, win_ref[:, :o1])
    hb = norm(halves[1])
    kva = _dot(ha, win_ref[:, o1:o3])
    ga = _dot(ha, win_ref[:, o3:])
    finish_q(halves[0], qa)
    qb = _dot(hb, win_ref[:, :o1])
    finish_kv(halves[0], kva, 0)
    finish_gate(halves[0], ga)
    kvb = _dot(hb, win_ref[:, o1:o3])
    gb = _dot(hb, win_ref[:, o3:])
    finish_q(halves[1], qb)
    finish_kv(halves[1], kvb, 1)
    finish_gate(halves[1], gb)


def _swa_in(x, layer_norm, layer, w_in, j, c_t, s1_t, s2_t):
    tokens = x.shape[0]
    rows = SWA_PROJ_ROWS
    row_spec = lambda width: pl.BlockSpec((rows, width), lambda i: (i, 0))
    consts = [(layer_norm, layer), (w_in, j)]
    out_widths = [SWA_WIDTH, SWA_KV_HEADS * LANES, None, SWA_WIDTH]
    vt_spec = pl.BlockSpec((rows // BLOCK, SWA_KV_WIDTH, BLOCK), lambda i: (i, 0, 0))
    vt_shape = jax.ShapeDtypeStruct((tokens // BLOCK, SWA_KV_WIDTH, BLOCK), BF16)
    return pl.pallas_call(
        _swa_in_kernel,
        grid=(tokens // rows,),
        in_specs=[row_spec(D_MODEL)] + [_layer_spec(c, ll) for c, ll in consts]
                 + [row_spec(LANES)] * 3,
        out_specs=[vt_spec if wd is None else row_spec(wd) for wd in out_widths],
        out_shape=[vt_shape if wd is None else jax.ShapeDtypeStruct((tokens, wd), BF16)
                   for wd in out_widths],
        compiler_params=pltpu.CompilerParams(
            dimension_semantics=("arbitrary",), vmem_limit_bytes=VMEM_LIMIT_BYTES),
        name="swa_in",
    )(x, *[c for c, _ in consts], c_t, s1_t, s2_t)


def _swa_attn_kernel(sink_ref, q_ref, kc_ref, kp_ref, vtc_ref, vtp_ref, g_ref, o_ref):
    first_has_prev = pl.program_id(1) > 0
    pairs = SWA_GROUP // 2
    key_i = lax.broadcasted_iota(jnp.int32, (BLOCK, pairs * BLOCK), 0)
    qry_i = lax.broadcasted_iota(jnp.int32, (BLOCK, pairs * BLOCK), 1) % BLOCK
    in_cur = key_i <= qry_i
    cur_w = jnp.where(in_cur, 1.0, 0.0).astype(BF16)
    low = _lane_index((BLOCK, LANES)) < SWA_HEAD_DIM
    zero = jnp.zeros((BLOCK, LANES), BF16)

    def block_rows(qb):
        return slice(qb * BLOCK, (qb + 1) * BLOCK)

    def scores(qb, kvh, sub):
        cols = slice(kvh * LANES, (kvh + 1) * LANES)
        kc = kc_ref[block_rows(qb), cols]
        kp = kp_ref[:, cols] if qb == 0 else kc_ref[block_rows(qb - 1), cols]
        if sub == 0:
            k_half = jnp.concatenate([jnp.where(low, kp, zero), jnp.where(low, kc, zero)], axis=0)
        else:
            k_half = jnp.concatenate([jnp.where(low, zero, kp), jnp.where(low, zero, kc)], axis=0)
        head0 = kvh * SWA_GROUP
        q_rows = jnp.concatenate(
            [q_ref[block_rows(qb),
                   (head0 + 2 * pp) * SWA_HEAD_DIM:(head0 + 2 * pp + 2) * SWA_HEAD_DIM]
             for pp in range(pairs)], axis=0)
        return _dot_nt(k_half, q_rows)

    def attend(qb, kvh, sub, st):
        s_prev = jnp.where(first_has_prev, st[:BLOCK], NEG_INF) if qb == 0 else st[:BLOCK]
        s = jnp.where(in_cur, st[BLOCK:], s_prev)
        sink = sink_ref[sub * SWA_KV_HEADS + kvh:sub * SWA_KV_HEADS + kvh + 1, :]
        m = jnp.maximum(jnp.max(s, axis=0, keepdims=True), sink)
        e = jnp.exp2(s - m)
        denom = jnp.sum(e, axis=0, keepdims=True) + jnp.exp2(sink - m)
        p = e.astype(BF16)
        p_cur = p * cur_w
        pt = jnp.concatenate([p - p_cur, p_cur], axis=0)
        vrows = slice(kvh * SWA_HEAD_DIM, (kvh + 1) * SWA_HEAD_DIM)
        vtp = vtp_ref[vrows, :] if qb == 0 else vtc_ref[qb - 1, vrows, :]
        vt = jnp.concatenate([vtp, vtc_ref[qb, vrows, :]], axis=1)
        return _dot(vt, pt) * (1.0 / denom)

    chains = [(qb, kvh, sub) for qb in range(SWA_Q_BLOCKS) for kvh in range(SWA_KV_HEADS)
              for sub in range(2)]
    st_next = scores(*chains[0])
    outs = {}
    for ci, (qb, kvh, sub) in enumerate(chains):
        st = st_next
        if ci + 1 < len(chains):
            st_next = scores(*chains[ci + 1])
        outs[sub] = attend(qb, kvh, sub, st)
        if sub == 1:
            head0 = kvh * SWA_GROUP
            for pp in range(pairs):
                qc = slice(pp * BLOCK, (pp + 1) * BLOCK)
                y = jnp.concatenate([outs[0][:, qc], outs[1][:, qc]], axis=0).T
                pcols = slice((head0 + 2 * pp) * SWA_HEAD_DIM,
                              (head0 + 2 * pp + 2) * SWA_HEAD_DIM)
                o_ref[block_rows(qb), pcols] = (
                    y * g_ref[block_rows(qb), pcols].astype(F32)).astype(BF16)


def _swa_sink_rows(sinks):
    s2 = (sinks * math.log2(math.e)).reshape(SWA_KV_HEADS, SWA_GROUP // 2, 2)
    rows = jnp.transpose(s2, (2, 0, 1)).reshape(2 * SWA_KV_HEADS, SWA_GROUP // 2, 1)
    return jnp.broadcast_to(rows, (2 * SWA_KV_HEADS, SWA_GROUP // 2, BLOCK)).reshape(
        2 * SWA_KV_HEADS, -1)


def _swa_attn(sinks, q, kd, vt, g, batch, seq):
    tokens = batch * seq
    nb = seq // BLOCK
    steps = nb // SWA_Q_BLOCKS
    rows = SWA_Q_BLOCKS * BLOCK
    cur = lambda b, i: (b * steps + i, 0)
    prev = lambda b, i: (b * nb + jnp.maximum(i * SWA_Q_BLOCKS - 1, 0), 0)
    cur_t = lambda b, i: (b * steps + i, 0, 0)
    prev_t = lambda b, i: (b * nb + jnp.maximum(i * SWA_Q_BLOCKS - 1, 0), 0, 0)
    kv_width = SWA_KV_HEADS * LANES
    sink_rows = _swa_sink_rows(sinks)
    return pl.pallas_call(
        _swa_attn_kernel,
        grid=(batch, steps),
        in_specs=[pl.BlockSpec(sink_rows.shape, lambda b, i: (0, 0)),
                  pl.BlockSpec((rows, SWA_WIDTH), cur),
                  pl.BlockSpec((rows, kv_width), cur),
                  pl.BlockSpec((BLOCK, kv_width), prev),
                  pl.BlockSpec((SWA_Q_BLOCKS, SWA_KV_WIDTH, BLOCK), cur_t),
                  pl.BlockSpec((None, SWA_KV_WIDTH, BLOCK), prev_t),
                  pl.BlockSpec((rows, SWA_WIDTH), cur)],
        out_specs=pl.BlockSpec((rows, SWA_WIDTH), cur),
        out_shape=jax.ShapeDtypeStruct((tokens, SWA_WIDTH), BF16),
        compiler_params=pltpu.CompilerParams(
            dimension_semantics=("arbitrary", "arbitrary"), vmem_limit_bytes=VMEM_LIMIT_BYTES),
        name="swa_attn",
    )(sink_rows, q, kd, kd, vt, vt, g)


def _out_kernel(x_ref, y_ref, w_ref, fn_ref, o_ref, w_sc, *, final):
    @pl.when(pl.program_id(0) == 0)
    def _():
        w_sc[...] = w_ref[...].astype(BF16)

    y = _load_groups(y_ref) if len(y_ref.shape) == 3 else y_ref[...]
    x_new = x_ref[...] + _dot(y, w_sc[...])
    o_ref[...] = _rmsnorm(x_new, fn_ref[...]) if final else x_new


def _out_proj(x, y, w_out, j, final_norm, final):
    tokens = x.shape[0]
    rows = OUT_ROWS
    row_spec = lambda width: pl.BlockSpec((rows, width), lambda i: (i, 0))
    if y.ndim == 3:
        y_spec = pl.BlockSpec((y.shape[0], rows, y.shape[2]), lambda i: (0, i, 0))
    else:
        y_spec = row_spec(y.shape[1])
    return pl.pallas_call(
        functools.partial(_out_kernel, final=final),
        grid=(tokens // rows,),
        in_specs=[row_spec(D_MODEL), y_spec, _layer_spec(w_out, j),
                  _const_spec(final_norm.shape)],
        out_specs=row_spec(D_MODEL),
        out_shape=jax.ShapeDtypeStruct((tokens, D_MODEL), F32),
        scratch_shapes=[pltpu.VMEM(w_out.shape[1:], BF16)],
        compiler_params=pltpu.CompilerParams(
            dimension_semantics=("arbitrary",), vmem_limit_bytes=VMEM_LIMIT_BYTES),
        name="out_proj",
    )(x, y, w_out, final_norm)


def kernel(x, positions, layer_norm, mla_w_in, mla_q_norm, mla_w_uq, mla_kv_norm, mla_w_ukv,
           mla_w_out, swa_w_in, swa_sinks, swa_w_out, final_norm):
    batch, seq, d_model = x.shape
    tokens = batch * seq
    assert d_model == D_MODEL and seq % FLASH_BLOCK == 0 and tokens % OUT_ROWS == 0
    assert FLASH_BLOCK % MLA_PROJ_ROWS == 0 and SWA_PROJ_ROWS % BLOCK == 0
    assert seq % (SWA_Q_BLOCKS * BLOCK) == 0
    mcos, msin, sc, s1, s2 = _rope_tables(positions)
    xt = x.reshape(tokens, d_model)
    fn = final_norm.reshape(1, -1)
    ln_stack = layer_norm[:, None, :]
    mla_w = {"w_in": mla_w_in.astype(BF16), "w_uq": mla_w_uq.astype(BF16),
             "w_ukv": mla_w_ukv.astype(BF16), "q_norm": mla_q_norm[:, None, :],
             "kv_norm": mla_kv_norm[:, None, :]}
    swa_w = swa_w_in.astype(BF16)
    for i in range(DEPTH):
        j = i // N_MIXERS
        final = i == DEPTH - 1
        if i % N_MIXERS == 0:
            qn, qr, kn, vt, krp, g = _mla_in(xt, ln_stack, i, mla_w, j, mcos, msin)
            y = _mla_flash(qn, qr, kn, krp, vt, g, batch, seq)
            w_out = mla_w_out
        else:
            q, kd, vt, g = _swa_in(xt, ln_stack, i, swa_w, j, sc, s1, s2)
            y = _swa_attn(swa_sinks[j], q, kd, vt, g, batch, seq)
            w_out = swa_w_out
        xt = _out_proj(xt, y, w_out, j, fn, final)
    return xt.reshape(batch, seq, d_model)
```

```python
import functools
import math

import jax
import jax.numpy as jnp
from jax import lax
from jax.experimental import pallas as pl
from jax.experimental.pallas import tpu as pltpu

D_MODEL = 2048
DEPTH = 4
N_MIXERS = 2
ROPE_THETA = 500000.0
NORM_EPS = 1e-6
BLOCK = 128
NEG_INF = -1e30

MLA_HEADS = 16
MLA_Q_RANK = 512
MLA_KV_RANK = 512
MLA_NOPE = 128
MLA_ROPE = 64
MLA_V = 128
MLA_WIDTH = MLA_HEADS * MLA_V

SWA_Q_HEADS = 32
SWA_KV_HEADS = 4
SWA_GROUP = SWA_Q_HEADS // SWA_KV_HEADS
SWA_HEAD_DIM = 64
SWA_ROPE_DIM = SWA_HEAD_DIM // 4
SWA_WIDTH = SWA_Q_HEADS * SWA_HEAD_DIM
SWA_KV_WIDTH = SWA_KV_HEADS * SWA_HEAD_DIM

LANES = 128
VMEM_LIMIT_BYTES = 58 * 1024 * 1024

MLA_PROJ_ROWS = 512
SWA_PROJ_ROWS = 512
OUT_ROWS = 512
FLASH_BLOCK = 512
SWA_Q_BLOCKS = 8

BF16 = jnp.bfloat16
F32 = jnp.float32


def _dot(a, b):
    return jnp.dot(a, b, preferred_element_type=F32)


def _dot_nt(a, b):
    return lax.dot_general(a, b, (((1,), (1,)), ((), ())), preferred_element_type=F32)


def _rmsnorm(xf, g):
    y = xf * lax.rsqrt(jnp.mean(xf * xf, axis=-1, keepdims=True) + NORM_EPS)
    return y * g


def _tile_lanes(t, width):
    reps = width // t.shape[1]
    return t if reps == 1 else jnp.concatenate([t] * reps, axis=1)


def _lane_index(shape):
    return lax.broadcasted_iota(jnp.int32, shape, 1)


def _rope_half64(x, cos_t, sin_t):
    w = x.shape[1]
    fwd = pltpu.roll(x, w - MLA_ROPE // 2, 1)
    bwd = pltpu.roll(x, MLA_ROPE // 2, 1)
    first_half = (_lane_index(x.shape) % MLA_ROPE) < (MLA_ROPE // 2)
    swapped = jnp.where(first_half, fwd, bwd)
    return x * _tile_lanes(cos_t, w) + swapped * _tile_lanes(sin_t, w)


def _rope_partial(x, c_t, s1_t, s2_t):
    w = x.shape[1]
    half = SWA_ROPE_DIM // 2
    fwd = pltpu.roll(x, w - half, 1)
    bwd = pltpu.roll(x, half, 1)
    return (x * _tile_lanes(c_t, w) + fwd * _tile_lanes(s1_t, w)
            + bwd * _tile_lanes(s2_t, w))


def _rope_table_kernel(pos_ref, const_ref, mcos_ref, msin_ref, sc_ref, s1_ref, s2_ref):
    pos = pos_ref[...].astype(F32)
    ang_m = pos * const_ref[0:1, :]
    mcos_ref[...] = jnp.cos(ang_m)
    msin_ref[...] = jnp.sin(ang_m) * const_ref[1:2, :]
    ang_s = pos * const_ref[2:3, :]
    sin_s = jnp.sin(ang_s)
    sc_ref[...] = jnp.cos(ang_s)
    s1_ref[...] = sin_s * const_ref[3:4, :]
    s2_ref[...] = sin_s * const_ref[4:5, :]


def _rope_constants():
    lane = jnp.arange(LANES)
    f_mla = ROPE_THETA ** (-jnp.arange(0, MLA_ROPE, 2, dtype=F32) / MLA_ROPE)
    f_swa = ROPE_THETA ** (-jnp.arange(0, SWA_ROPE_DIM, 2, dtype=F32) / SWA_ROPE_DIM)
    half_m = MLA_ROPE // 2
    half_s = SWA_ROPE_DIM // 2
    d_m = lane % MLA_ROPE
    d_s = lane % SWA_HEAD_DIM
    rows = [
        f_mla[d_m % half_m],
        jnp.where(d_m < half_m, -1.0, 1.0),
        jnp.where(d_s < SWA_ROPE_DIM, f_swa[d_s % half_s], 0.0),
        jnp.where(d_s < half_s, -1.0, 0.0),
        jnp.where((d_s >= half_s) & (d_s < SWA_ROPE_DIM), 1.0, 0.0),
    ]
    rows += [jnp.zeros((LANES,), F32)] * 3
    return jnp.stack([r.astype(F32) for r in rows])


def _rope_tables(positions):
    tokens = positions.size
    rows = 1024
    tab = jax.ShapeDtypeStruct((tokens, LANES), F32)
    spec = pl.BlockSpec((rows, LANES), lambda i: (i, 0))
    return pl.pallas_call(
        _rope_table_kernel,
        grid=(tokens // rows,),
        in_specs=[pl.BlockSpec((rows, 1), lambda i: (i, 0)),
                  pl.BlockSpec((8, LANES), lambda i: (0, 0))],
        out_specs=[spec] * 5,
        out_shape=[tab] * 5,
        name="rope_tables",
    )(positions.reshape(tokens, 1), _rope_constants())


def _mla_in_kernel(x_ref, ln_ref, win_ref, qnorm_ref, kvnorm_ref, wuq_ref, wukv_ref,
                   cos_ref, sin_ref, qn_ref, qr_ref, kn_ref, vt_ref, krp_ref, g_ref):
    scale = (MLA_NOPE + MLA_ROPE) ** -0.5 * math.log2(math.e)
    h = _rmsnorm(x_ref[...], ln_ref[...]).astype(BF16)
    cos_t = cos_ref[...]
    sin_t = sin_ref[...]
    rank2 = MLA_Q_RANK + MLA_KV_RANK
    heads = range(MLA_HEADS)

    c = _dot(h, win_ref[:, :rank2])
    cq = _rmsnorm(c[:, :MLA_Q_RANK], qnorm_ref[...]).astype(BF16)
    ckv = _rmsnorm(c[:, MLA_Q_RANK:], kvnorm_ref[...]).astype(BF16)

    tail = _dot(h, win_ref[:, rank2:])
    kr = tail[:, :MLA_ROPE]
    kr = _rope_half64(jnp.concatenate([kr, kr], axis=1), cos_t, sin_t)
    low = _lane_index(kr.shape) < MLA_ROPE
    krp_ref[:, :LANES] = jnp.where(low, kr, 0.0).astype(BF16)
    krp_ref[:, LANES:] = jnp.where(low, 0.0, kr).astype(BF16)

    gate = tail[:, MLA_ROPE:]
    _store_groups(g_ref, (gate * jax.nn.sigmoid(gate)).astype(BF16))

    q = _dot(cq, wuq_ref[...])
    qd = MLA_NOPE + MLA_ROPE
    qn = jnp.concatenate([q[:, hh * qd:hh * qd + MLA_NOPE] for hh in heads], axis=1)
    qr = jnp.concatenate([q[:, hh * qd + MLA_NOPE:(hh + 1) * qd] for hh in heads], axis=1)
    _store_groups(qn_ref, (qn * scale).astype(BF16))
    _store_groups(qr_ref, (_rope_half64(qr, cos_t, sin_t) * scale).astype(BF16))

    kv = _dot(ckv, wukv_ref[...])
    kd = MLA_NOPE + MLA_V
    kn = jnp.concatenate([kv[:, hh * kd:hh * kd + MLA_NOPE] for hh in heads], axis=1)
    v = jnp.concatenate([kv[:, hh * kd + MLA_NOPE:(hh + 1) * kd] for hh in heads], axis=1)
    _store_groups(kn_ref, kn.astype(BF16))
    vt_ref[0] = v.T.astype(BF16)


def _store_groups(ref, val):
    groups, _, width = ref.shape
    for gg in range(groups):
        ref[gg] = val[:, gg * width:(gg + 1) * width]


def _load_groups(ref):
    return jnp.concatenate([ref[gg] for gg in range(ref.shape[0])], axis=1)


def _const_spec(shape):
    return pl.BlockSpec(shape, lambda i: (0,) * len(shape), pipeline_mode=pl.Buffered(1))


def _layer_spec(stacked, layer):
    tail = (0,) * (stacked.ndim - 1)
    return pl.BlockSpec((None,) + stacked.shape[1:], lambda i: (layer,) + tail,
                        pipeline_mode=pl.Buffered(1))


def _mla_in(x, layer_norm, layer, w, j, cos_t, sin_t):
    tokens = x.shape[0]
    rows = MLA_PROJ_ROWS
    row_spec = lambda width: pl.BlockSpec((rows, width), lambda i: (i, 0))
    consts = [(layer_norm, layer), (w["w_in"], j), (w["q_norm"], j), (w["kv_norm"], j),
              (w["w_uq"], j), (w["w_ukv"], j)]
    pairs = MLA_HEADS // 2
    per_blk = FLASH_BLOCK // rows

    def pair_major(width):
        return (pl.BlockSpec((pairs, rows, width), lambda i: (0, i, 0)),
                jax.ShapeDtypeStruct((pairs, tokens, width), BF16))

    outs = [pair_major(2 * MLA_NOPE),
            pair_major(2 * MLA_ROPE),
            pair_major(2 * MLA_NOPE),
            (pl.BlockSpec((1, MLA_WIDTH, rows), lambda i: (i // per_blk, 0, i % per_blk)),
             jax.ShapeDtypeStruct((tokens // FLASH_BLOCK, MLA_WIDTH, FLASH_BLOCK), BF16)),
            (row_spec(2 * LANES), jax.ShapeDtypeStruct((tokens, 2 * LANES), BF16)),
            pair_major(2 * MLA_V)]
    return pl.pallas_call(
        _mla_in_kernel,
        grid=(tokens // rows,),
        in_specs=[row_spec(D_MODEL)] + [_layer_spec(c, ll) for c, ll in consts]
                 + [row_spec(LANES), row_spec(LANES)],
        out_specs=[o[0] for o in outs],
        out_shape=[o[1] for o in outs],
        compiler_params=pltpu.CompilerParams(
            dimension_semantics=("arbitrary",), vmem_limit_bytes=VMEM_LIMIT_BYTES),
        name="mla_in",
    )(x, *[c for c, _ in consts], cos_t, sin_t)


def _mla_flash_kernel(qn_ref, qr_ref, kn_ref, kr_ref, vt_ref, g_ref, o_ref,
                      kcat_sc, m_sc, acc_sc, *, seq):
    blk = FLASH_BLOCK
    pair = (0, 1)
    hcols = [slice(hh * LANES, (hh + 1) * LANES) for hh in pair]
    for hh in pair:
        kcat_sc[hh, :, :LANES] = kn_ref[:, hcols[hh]]
        kcat_sc[hh, :, LANES:] = kr_ref[:, hcols[hh]]
    ones_rows = jnp.ones((16, blk), BF16)
    causal_t = (lax.broadcasted_iota(jnp.int32, (blk, blk), 0)
                <= lax.broadcasted_iota(jnp.int32, (blk, blk), 1))

    for qi in range(seq // blk):
        rows = slice(qi * blk, (qi + 1) * blk)
        qs = [jnp.concatenate([qn_ref[rows, hcols[hh]], qr_ref[rows, :]], axis=1) for hh in pair]
        def scores(hh, kb, masked, qs=qs):
            kstart = pl.multiple_of(kb * blk, blk)
            st = _dot_nt(kcat_sc[hh, pl.ds(kstart, blk), :], qs[hh])
            return jnp.where(causal_t, st, NEG_INF) if masked else st

        def accumulate(hh, kb, st, first):
            m_prev = jnp.float32(NEG_INF) if first else m_sc[hh]
            m_new = jnp.maximum(m_prev, jnp.max(st, axis=0, keepdims=True))
            p = jnp.exp2(st - m_new)
            vt_aug = jnp.concatenate([vt_ref[kb, hcols[hh], :], ones_rows], axis=0)
            pv = _dot(vt_aug, p.astype(BF16))
            acc_sc[hh] = pv if first else jnp.exp2(m_prev - m_new) * acc_sc[hh] + pv
            m_sc[hh] = m_new

        visits = [qi] + list(range(qi))
        st_next = [scores(hh, visits[0], True) for hh in pair]
        for i, kb in enumerate(visits):
            for hh in pair:
                st = st_next[hh]
                if i + 1 < len(visits):
                    st_next[hh] = scores(hh, visits[i + 1], False)
                accumulate(hh, kb, st, i == 0)

        for hh in pair:
            acc = acc_sc[hh]
            y = (acc[:MLA_V] * (1.0 / acc[MLA_V:MLA_V + 1])).T
            o_ref[rows, hcols[hh]] = (y * g_ref[rows, hcols[hh]].astype(F32)).astype(BF16)


def _mla_flash(qn, qr, kn, krp, vt, g, batch, seq):
    tokens = batch * seq
    nblk = seq // FLASH_BLOCK
    pair_spec = pl.BlockSpec((None, seq, 2 * LANES), lambda b, p: (p, b, 0))
    return pl.pallas_call(
        functools.partial(_mla_flash_kernel, seq=seq),
        grid=(batch, MLA_HEADS // 2),
        in_specs=[pair_spec,
                  pl.BlockSpec((None, seq, LANES), lambda b, p: (p, b, 0)),
                  pair_spec,
                  pl.BlockSpec((seq, 2 * LANES), lambda b, p: (b, 0)),
                  pl.BlockSpec((nblk, 2 * LANES, FLASH_BLOCK), lambda b, p: (b, p, 0)),
                  pair_spec],
        out_specs=pair_spec,
        out_shape=jax.ShapeDtypeStruct((MLA_HEADS // 2, tokens, 2 * MLA_V), BF16),
        scratch_shapes=[pltpu.VMEM((2, seq, 2 * LANES), BF16),
                        pltpu.VMEM((2, 1, FLASH_BLOCK), F32),
                        pltpu.VMEM((2, MLA_V + 16, FLASH_BLOCK), F32)],
        compiler_params=pltpu.CompilerParams(
            dimension_semantics=("arbitrary", "arbitrary"), vmem_limit_bytes=VMEM_LIMIT_BYTES),
        name="mla_flash",
    )(qn, qr, kn, krp, vt, g)


def _dup_heads(t):
    parts = []
    for hh in range(t.shape[1] // SWA_HEAD_DIM):
        piece = t[:, hh * SWA_HEAD_DIM:(hh + 1) * SWA_HEAD_DIM]
        parts += [piece, piece]
    return jnp.concatenate(parts, axis=1)


def _swa_in_kernel(x_ref, ln_ref, win_ref, c_ref, s1_ref, s2_ref,
                   q_ref, kd_ref, vt_ref, g_ref):
    scale = SWA_HEAD_DIM ** -0.5 * math.log2(math.e)
    o1 = SWA_WIDTH
    o2 = o1 + SWA_KV_WIDTH
    o3 = o2 + SWA_KV_WIDTH
    half = x_ref.shape[0] // 2
    halves = (slice(0, half), slice(half, 2 * half))

    def norm(rs):
        return _rmsnorm(x_ref[rs, :], ln_ref[...]).astype(BF16)

    def tables(rs):
        return c_ref[rs, :], s1_ref[rs, :], s2_ref[rs, :]

    def finish_q(rs, q):
        q_ref[rs, :] = (_rope_partial(q, *tables(rs)) * scale).astype(BF16)

    def finish_kv(rs, kv, part):
        k = _rope_partial(kv[:, :SWA_KV_WIDTH], *tables(rs))
        kd_ref[rs, :] = _dup_heads(k).astype(BF16)
        vt = kv[:, SWA_KV_WIDTH:].T.astype(BF16)
        per_half = vt_ref.shape[0] // 2
        for cc in range(per_half):
            vt_ref[part * per_half + cc] = vt[:, cc * BLOCK:(cc + 1) * BLOCK]

    def finish_gate(rs, gate):
        g_ref[rs, :] = (gate * jax.nn.sigmoid(gate)).astype(BF16)

    ha = norm(halves[0])
    qa = _dot(ha, win_ref[:, :o1])
    hb = norm(halves[1])
    kva = _dot(ha, win_ref[:, o1:o3])
    ga = _dot(ha, win_ref[:, o3:])
    finish_q(halves[0], qa)
    qb = _dot(hb, win_ref[:, :o1])
    finish_kv(halves[0], kva, 0)
    finish_gate(halves[0], ga)
    kvb = _dot(hb, win_ref[:, o1:o3])
    gb = _dot(hb, win_ref[:, o3:])
    finish_q(halves[1], qb)
    finish_kv(halves[1], kvb, 1)
    finish_gate(halves[1], gb)


def _swa_in(x, layer_norm, layer, w_in, j, c_t, s1_t, s2_t):
    tokens = x.shape[0]
    rows = SWA_PROJ_ROWS
    row_spec = lambda width: pl.BlockSpec((rows, width), lambda i: (i, 0))
    consts = [(layer_norm, layer), (w_in, j)]
    out_widths = [SWA_WIDTH, SWA_KV_HEADS * LANES, None, SWA_WIDTH]
    vt_spec = pl.BlockSpec((rows // BLOCK, SWA_KV_WIDTH, BLOCK), lambda i: (i, 0, 0))
    vt_shape = jax.ShapeDtypeStruct((tokens // BLOCK, SWA_KV_WIDTH, BLOCK), BF16)
    return pl.pallas_call(
        _swa_in_kernel,
        grid=(tokens // rows,),
        in_specs=[row_spec(D_MODEL)] + [_layer_spec(c, ll) for c, ll in consts]
                 + [row_spec(LANES)] * 3,
        out_specs=[vt_spec if wd is None else row_spec(wd) for wd in out_widths],
        out_shape=[vt_shape if wd is None else jax.ShapeDtypeStruct((tokens, wd), BF16)
                   for wd in out_widths],
        compiler_params=pltpu.CompilerParams(
            dimension_semantics=("arbitrary",), vmem_limit_bytes=VMEM_LIMIT_BYTES),
        name="swa_in",
    )(x, *[c for c, _ in consts], c_t, s1_t, s2_t)


def _swa_attn_kernel(sink_ref, q_ref, kc_ref, kp_ref, vtc_ref, vtp_ref, g_ref, o_ref):
    first_has_prev = pl.program_id(1) > 0
    pairs = SWA_GROUP // 2
    key_i = lax.broadcasted_iota(jnp.int32, (BLOCK, pairs * BLOCK), 0)
    qry_i = lax.broadcasted_iota(jnp.int32, (BLOCK, pairs * BLOCK), 1) % BLOCK
    in_cur = key_i <= qry_i
    cur_w = jnp.where(in_cur, 1.0, 0.0).astype(BF16)
    low = _lane_index((BLOCK, LANES)) < SWA_HEAD_DIM
    zero = jnp.zeros((BLOCK, LANES), BF16)

    def block_rows(qb):
        return slice(qb * BLOCK, (qb + 1) * BLOCK)

    def scores(qb, kvh, sub):
        cols = slice(kvh * LANES, (kvh + 1) * LANES)
        kc = kc_ref[block_rows(qb), cols]
        kp = kp_ref[:, cols] if qb == 0 else kc_ref[block_rows(qb - 1), cols]
        if sub == 0:
            k_half = jnp.concatenate([jnp.where(low, kp, zero), jnp.where(low, kc, zero)], axis=0)
        else:
            k_half = jnp.concatenate([jnp.where(low, zero, kp), jnp.where(low, zero, kc)], axis=0)
        head0 = kvh * SWA_GROUP
        q_rows = jnp.concatenate(
            [q_ref[block_rows(qb),
                   (head0 + 2 * pp) * SWA_HEAD_DIM:(head0 + 2 * pp + 2) * SWA_HEAD_DIM]
             for pp in range(pairs)], axis=0)
        return _dot_nt(k_half, q_rows)

    def attend(qb, kvh, sub, st):
        s_prev = jnp.where(first_has_prev, st[:BLOCK], NEG_INF) if qb == 0 else st[:BLOCK]
        s = jnp.where(in_cur, st[BLOCK:], s_prev)
        sink = sink_ref[sub * SWA_KV_HEADS + kvh:sub * SWA_KV_HEADS + kvh + 1, :]
        m = jnp.maximum(jnp.max(s, axis=0, keepdims=True), sink)
        e = jnp.exp2(s - m)
        denom = jnp.sum(e, axis=0, keepdims=True) + jnp.exp2(sink - m)
        p = e.astype(BF16)
        p_cur = p * cur_w
        pt = jnp.concatenate([p - p_cur, p_cur], axis=0)
        vrows = slice(kvh * SWA_HEAD_DIM, (kvh + 1) * SWA_HEAD_DIM)
        vtp = vtp_ref[vrows, :] if qb == 0 else vtc_ref[qb - 1, vrows, :]
        vt = jnp.concatenate([vtp, vtc_ref[qb, vrows, :]], axis=1)
        return _dot(vt, pt) * (1.0 / denom)

    chains = [(qb, kvh, sub) for qb in range(SWA_Q_BLOCKS) for kvh in range(SWA_KV_HEADS)
              for sub in range(2)]
    st_next = scores(*chains[0])
    outs = {}
    for ci, (qb, kvh, sub) in enumerate(chains):
        st = st_next
        if ci + 1 < len(chains):
            st_next = scores(*chains[ci + 1])
        outs[sub] = attend(qb, kvh, sub, st)
        if sub == 1:
            head0 = kvh * SWA_GROUP
            for pp in range(pairs):
                qc = slice(pp * BLOCK, (pp + 1) * BLOCK)
                y = jnp.concatenate([outs[0][:, qc], outs[1][:, qc]], axis=0).T
                pcols = slice((head0 + 2 * pp) * SWA_HEAD_DIM,
                              (head0 + 2 * pp + 2) * SWA_HEAD_DIM)
                o_ref[block_rows(qb), pcols] = (
                    y * g_ref[block_rows(qb), pcols].astype(F32)).astype(BF16)


def _swa_sink_rows(sinks):
    s2 = (sinks * math.log2(math.e)).reshape(SWA_KV_HEADS, SWA_GROUP // 2, 2)
    rows = jnp.transpose(s2, (2, 0, 1)).reshape(2 * SWA_KV_HEADS, SWA_GROUP // 2, 1)
    return jnp.broadcast_to(rows, (2 * SWA_KV_HEADS, SWA_GROUP // 2, BLOCK)).reshape(
        2 * SWA_KV_HEADS, -1)


def _swa_attn(sinks, q, kd, vt, g, batch, seq):
    tokens = batch * seq
    nb = seq // BLOCK
    steps = nb // SWA_Q_BLOCKS
    rows = SWA_Q_BLOCKS * BLOCK
    cur = lambda b, i: (b * steps + i, 0)
    prev = lambda b, i: (b * nb + jnp.maximum(i * SWA_Q_BLOCKS - 1, 0), 0)
    cur_t = lambda b, i: (b * steps + i, 0, 0)
    prev_t = lambda b, i: (b * nb + jnp.maximum(i * SWA_Q_BLOCKS - 1, 0), 0, 0)
    kv_width = SWA_KV_HEADS * LANES
    sink_rows = _swa_sink_rows(sinks)
    return pl.pallas_call(
        _swa_attn_kernel,
        grid=(batch, steps),
        in_specs=[pl.BlockSpec(sink_rows.shape, lambda b, i: (0, 0)),
                  pl.BlockSpec((rows, SWA_WIDTH), cur),
                  pl.BlockSpec((rows, kv_width), cur),
                  pl.BlockSpec((BLOCK, kv_width), prev),
                  pl.BlockSpec((SWA_Q_BLOCKS, SWA_KV_WIDTH, BLOCK), cur_t),
                  pl.BlockSpec((None, SWA_KV_WIDTH, BLOCK), prev_t),
                  pl.BlockSpec((rows, SWA_WIDTH), cur)],
        out_specs=pl.BlockSpec((rows, SWA_WIDTH), cur),
        out_shape=jax.ShapeDtypeStruct((tokens, SWA_WIDTH), BF16),
        compiler_params=pltpu.CompilerParams(
            dimension_semantics=("arbitrary", "arbitrary"), vmem_limit_bytes=VMEM_LIMIT_BYTES),
        name="swa_attn",
    )(sink_rows, q, kd, kd, vt, vt, g)


def _out_kernel(x_ref, y_ref, w_ref, fn_ref, o_ref, w_sc, *, final):
    @pl.when(pl.program_id(0) == 0)
    def _():
        w_sc[...] = w_ref[...].astype(BF16)

    y = _load_groups(y_ref) if len(y_ref.shape) == 3 else y_ref[...]
    x_new = x_ref[...] + _dot(y, w_sc[...])
    o_ref[...] = _rmsnorm(x_new, fn_ref[...]) if final else x_new


def _out_proj(x, y, w_out, j, final_norm, final):
    tokens = x.shape[0]
    rows = OUT_ROWS
    row_spec = lambda width: pl.BlockSpec((rows, width), lambda i: (i, 0))
    if y.ndim == 3:
        y_spec = pl.BlockSpec((y.shape[0], rows, y.shape[2]), lambda i: (0, i, 0))
    else:
        y_spec = row_spec(y.shape[1])
    return pl.pallas_call(
        functools.partial(_out_kernel, final=final),
        grid=(tokens // rows,),
        in_specs=[row_spec(D_MODEL), y_spec, _layer_spec(w_out, j),
                  _const_spec(final_norm.shape)],
        out_specs=row_spec(D_MODEL),
        out_shape=jax.ShapeDtypeStruct((tokens, D_MODEL), F32),
        scratch_shapes=[pltpu.VMEM(w_out.shape[1:], BF16)],
        compiler_params=pltpu.CompilerParams(
            dimension_semantics=("arbitrary",), vmem_limit_bytes=VMEM_LIMIT_BYTES),
        name="out_proj",
    )(x, y, w_out, final_norm)


def kernel(x, positions, layer_norm, mla_w_in, mla_q_norm, mla_w_uq, mla_kv_norm, mla_w_ukv,
           mla_w_out, swa_w_in, swa_sinks, swa_w_out, final_norm):
    batch, seq, d_model = x.shape
    tokens = batch * seq
    assert d_model == D_MODEL and seq % FLASH_BLOCK == 0 and tokens % OUT_ROWS == 0
    assert FLASH_BLOCK % MLA_PROJ_ROWS == 0 and SWA_PROJ_ROWS % BLOCK == 0
    assert seq % (SWA_Q_BLOCKS * BLOCK) == 0
    mcos, msin, sc, s1, s2 = _rope_tables(positions)
    xt = x.reshape(tokens, d_model)
    fn = final_norm.reshape(1, -1)
    ln_stack = layer_norm[:, None, :]
    mla_w = {"w_in": mla_w_in.astype(BF16), "w_uq": mla_w_uq.astype(BF16),
             "w_ukv": mla_w_ukv.astype(BF16), "q_norm": mla_q_norm[:, None, :],
             "kv_norm": mla_kv_norm[:, None, :]}
    swa_w = swa_w_in.astype(BF16)
    for i in range(DEPTH):
        j = i // N_MIXERS
        final = i == DEPTH - 1
        if i % N_MIXERS == 0:
            qn, qr, kn, vt, krp, g = _mla_in(xt, ln_stack, i, mla_w, j, mcos, msin)
            y = _mla_flash(qn, qr, kn, krp, vt, g, batch, seq)
            w_out = mla_w_out
        else:
            q, kd, vt, g = _swa_in(xt, ln_stack, i, swa_w, j, sc, s1, s2)
            y = _swa_attn(swa_sinks[j], q, kd, vt, g, batch, seq)
            w_out = swa_w_out
        xt = _out_proj(xt, y, w_out, j, fn, final)
    return xt.reshape(batch, seq, d_model)
```

```python
import functools
import math

import jax
import jax.numpy as jnp
from jax import lax
from jax.experimental import pallas as pl
from jax.experimental.pallas import tpu as pltpu

D_MODEL = 2048
DEPTH = 4
N_MIXERS = 2
ROPE_THETA = 500000.0
NORM_EPS = 1e-6
BLOCK = 128
NEG_INF = -1e30

MLA_HEADS = 16
MLA_Q_RANK = 512
MLA_KV_RANK = 512
MLA_NOPE = 128
MLA_ROPE = 64
MLA_V = 128
MLA_WIDTH = MLA_HEADS * MLA_V

SWA_Q_HEADS = 32
SWA_KV_HEADS = 4
SWA_GROUP = SWA_Q_HEADS // SWA_KV_HEADS
SWA_HEAD_DIM = 64
SWA_ROPE_DIM = SWA_HEAD_DIM // 4
SWA_WIDTH = SWA_Q_HEADS * SWA_HEAD_DIM
SWA_KV_WIDTH = SWA_KV_HEADS * SWA_HEAD_DIM

LANES = 128
VMEM_LIMIT_BYTES = 58 * 1024 * 1024

MLA_PROJ_ROWS = 512
SWA_PROJ_ROWS = 512
OUT_ROWS = 512
FLASH_BLOCK = 512
SWA_Q_BLOCKS = 4

BF16 = jnp.bfloat16
F32 = jnp.float32


def _dot(a, b):
    return jnp.dot(a, b, preferred_element_type=F32)


def _dot_nt(a, b):
    return lax.dot_general(a, b, (((1,), (1,)), ((), ())), preferred_element_type=F32)


def _rmsnorm(xf, g):
    y = xf * lax.rsqrt(jnp.mean(xf * xf, axis=-1, keepdims=True) + NORM_EPS)
    return y * g


def _tile_lanes(t, width):
    reps = width // t.shape[1]
    return t if reps == 1 else jnp.concatenate([t] * reps, axis=1)


def _lane_index(shape):
    return lax.broadcasted_iota(jnp.int32, shape, 1)


def _rope_half64(x, cos_t, sin_t):
    w = x.shape[1]
    fwd = pltpu.roll(x, w - MLA_ROPE // 2, 1)
    bwd = pltpu.roll(x, MLA_ROPE // 2, 1)
    first_half = (_lane_index(x.shape) % MLA_ROPE) < (MLA_ROPE // 2)
    swapped = jnp.where(first_half, fwd, bwd)
    return x * _tile_lanes(cos_t, w) + swapped * _tile_lanes(sin_t, w)


def _rope_partial(x, c_t, s1_t, s2_t):
    w = x.shape[1]
    half = SWA_ROPE_DIM // 2
    fwd = pltpu.roll(x, w - half, 1)
    bwd = pltpu.roll(x, half, 1)
    return (x * _tile_lanes(c_t, w) + fwd * _tile_lanes(s1_t, w)
            + bwd * _tile_lanes(s2_t, w))


def _rope_table_kernel(pos_ref, const_ref, mcos_ref, msin_ref, sc_ref, s1_ref, s2_ref):
    pos = pos_ref[...].astype(F32)
    ang_m = pos * const_ref[0:1, :]
    mcos_ref[...] = jnp.cos(ang_m)
    msin_ref[...] = jnp.sin(ang_m) * const_ref[1:2, :]
    ang_s = pos * const_ref[2:3, :]
    sin_s = jnp.sin(ang_s)
    sc_ref[...] = jnp.cos(ang_s)
    s1_ref[...] = sin_s * const_ref[3:4, :]
    s2_ref[...] = sin_s * const_ref[4:5, :]


def _rope_constants():
    lane = jnp.arange(LANES)
    f_mla = ROPE_THETA ** (-jnp.arange(0, MLA_ROPE, 2, dtype=F32) / MLA_ROPE)
    f_swa = ROPE_THETA ** (-jnp.arange(0, SWA_ROPE_DIM, 2, dtype=F32) / SWA_ROPE_DIM)
    half_m = MLA_ROPE // 2
    half_s = SWA_ROPE_DIM // 2
    d_m = lane % MLA_ROPE
    d_s = lane % SWA_HEAD_DIM
    rows = [
        f_mla[d_m % half_m],
        jnp.where(d_m < half_m, -1.0, 1.0),
        jnp.where(d_s < SWA_ROPE_DIM, f_swa[d_s % half_s], 0.0),
        jnp.where(d_s < half_s, -1.0, 0.0),
        jnp.where((d_s >= half_s) & (d_s < SWA_ROPE_DIM), 1.0, 0.0),
    ]
    rows += [jnp.zeros((LANES,), F32)] * 3
    return jnp.stack([r.astype(F32) for r in rows])


def _rope_tables(positions):
    tokens = positions.size
    rows = 1024
    tab = jax.ShapeDtypeStruct((tokens, LANES), F32)
    spec = pl.BlockSpec((rows, LANES), lambda i: (i, 0))
    return pl.pallas_call(
        _rope_table_kernel,
        grid=(tokens // rows,),
        in_specs=[pl.BlockSpec((rows, 1), lambda i: (i, 0)),
                  pl.BlockSpec((8, LANES), lambda i: (0, 0))],
        out_specs=[spec] * 5,
        out_shape=[tab] * 5,
        name="rope_tables",
    )(positions.reshape(tokens, 1), _rope_constants())


def _mla_in_kernel(x_ref, ln_ref, win_ref, qnorm_ref, kvnorm_ref, wuq_ref, wukv_ref,
                   cos_ref, sin_ref, qn_ref, qr_ref, kn_ref, vt_ref, krp_ref, g_ref):
    scale = (MLA_NOPE + MLA_ROPE) ** -0.5 * math.log2(math.e)
    h = _rmsnorm(x_ref[...], ln_ref[...]).astype(BF16)
    cos_t = cos_ref[...]
    sin_t = sin_ref[...]
    rank2 = MLA_Q_RANK + MLA_KV_RANK
    heads = range(MLA_HEADS)

    c = _dot_nt(h, win_ref[:rank2, :])
    cq = _rmsnorm(c[:, :MLA_Q_RANK], qnorm_ref[...]).astype(BF16)
    ckv = _rmsnorm(c[:, MLA_Q_RANK:], kvnorm_ref[...]).astype(BF16)

    tail = _dot_nt(h, win_ref[rank2:, :])
    kr = tail[:, :MLA_ROPE]
    kr = _rope_half64(jnp.concatenate([kr, kr], axis=1), cos_t, sin_t)
    low = _lane_index(kr.shape) < MLA_ROPE
    krp_ref[:, :LANES] = jnp.where(low, kr, 0.0).astype(BF16)
    krp_ref[:, LANES:] = jnp.where(low, 0.0, kr).astype(BF16)

    gate = tail[:, MLA_ROPE:]
    _store_groups(g_ref, (gate * jax.nn.sigmoid(gate)).astype(BF16))

    q = _dot(cq, wuq_ref[...])
    qd = MLA_NOPE + MLA_ROPE
    qn = jnp.concatenate([q[:, hh * qd:hh * qd + MLA_NOPE] for hh in heads], axis=1)
    qr = jnp.concatenate([q[:, hh * qd + MLA_NOPE:(hh + 1) * qd] for hh in heads], axis=1)
    _store_groups(qn_ref, (qn * scale).astype(BF16))
    _store_groups(qr_ref, (_rope_half64(qr, cos_t, sin_t) * scale).astype(BF16))

    kv = _dot(ckv, wukv_ref[...])
    kd = MLA_NOPE + MLA_V
    kn = jnp.concatenate([kv[:, hh * kd:hh * kd + MLA_NOPE] for hh in heads], axis=1)
    v = jnp.concatenate([kv[:, hh * kd + MLA_NOPE:(hh + 1) * kd] for hh in heads], axis=1)
    _store_groups(kn_ref, kn.astype(BF16))
    vt_ref[0] = v.T.astype(BF16)


def _store_groups(ref, val):
    groups, _, width = ref.shape
    for gg in range(groups):
        ref[gg] = val[:, gg * width:(gg + 1) * width]


def _load_groups(ref):
    return jnp.concatenate([ref[gg] for gg in range(ref.shape[0])], axis=1)


def _const_spec(shape):
    return pl.BlockSpec(shape, lambda i: (0,) * len(shape), pipeline_mode=pl.Buffered(1))


def _layer_spec(stacked, layer):
    tail = (0,) * (stacked.ndim - 1)
    return pl.BlockSpec((None,) + stacked.shape[1:], lambda i: (layer,) + tail,
                        pipeline_mode=pl.Buffered(1))


def _mla_in(x, layer_norm, layer, w, j, cos_t, sin_t):
    tokens = x.shape[0]
    rows = MLA_PROJ_ROWS
    row_spec = lambda width: pl.BlockSpec((rows, width), lambda i: (i, 0))
    consts = [(layer_norm, layer), (w["w_in"], j), (w["q_norm"], j), (w["kv_norm"], j),
              (w["w_uq"], j), (w["w_ukv"], j)]
    pairs = MLA_HEADS // 2
    per_blk = FLASH_BLOCK // rows

    def pair_major(width):
        return (pl.BlockSpec((pairs, rows, width), lambda i: (0, i, 0)),
                jax.ShapeDtypeStruct((pairs, tokens, width), BF16))

    outs = [pair_major(2 * MLA_NOPE),
            pair_major(2 * MLA_ROPE),
            pair_major(2 * MLA_NOPE),
            (pl.BlockSpec((1, MLA_WIDTH, rows), lambda i: (i // per_blk, 0, i % per_blk)),
             jax.ShapeDtypeStruct((tokens // FLASH_BLOCK, MLA_WIDTH, FLASH_BLOCK), BF16)),
            (row_spec(2 * LANES), jax.ShapeDtypeStruct((tokens, 2 * LANES), BF16)),
            pair_major(2 * MLA_V)]
    return pl.pallas_call(
        _mla_in_kernel,
        grid=(tokens // rows,),
        in_specs=[row_spec(D_MODEL)] + [_layer_spec(c, ll) for c, ll in consts]
                 + [row_spec(LANES), row_spec(LANES)],
        out_specs=[o[0] for o in outs],
        out_shape=[o[1] for o in outs],
        compiler_params=pltpu.CompilerParams(
            dimension_semantics=("arbitrary",), vmem_limit_bytes=VMEM_LIMIT_BYTES),
        name="mla_in",
    )(x, *[c for c, _ in consts], cos_t, sin_t)


def _mla_flash_kernel(qn_ref, qr_ref, kn_ref, kr_ref, vt_ref, g_ref, o_ref,
                      kcat_sc, s_sc, m_sc, acc_sc, *, seq):
    blk = FLASH_BLOCK
    pair = (0, 1)
    hcols = [slice(hh * LANES, (hh + 1) * LANES) for hh in pair]
    for hh in pair:
        kcat_sc[hh, :, :LANES] = kn_ref[:, hcols[hh]]
        kcat_sc[hh, :, LANES:] = kr_ref[:, hcols[hh]]
    ones_rows = jnp.ones((16, blk), BF16)
    causal_t = (lax.broadcasted_iota(jnp.int32, (blk, blk), 0)
                <= lax.broadcasted_iota(jnp.int32, (blk, blk), 1))

    for qi in range(seq // blk):
        rows = slice(qi * blk, (qi + 1) * blk)
        qs = [jnp.concatenate([qn_ref[rows, hcols[hh]], qr_ref[rows, :]], axis=1) for hh in pair]
        def scores(hh, kb, slot, masked, qs=qs):
            kstart = pl.multiple_of(kb * blk, blk)
            st = _dot_nt(kcat_sc[hh, pl.ds(kstart, blk), :], qs[hh])
            s_sc[hh, slot] = jnp.where(causal_t, st, NEG_INF) if masked else st

        def accumulate(hh, kb, slot, first):
            st = s_sc[hh, slot]
            m_prev = jnp.float32(NEG_INF) if first else m_sc[hh]
            m_new = jnp.maximum(m_prev, jnp.max(st, axis=0, keepdims=True))
            p = jnp.exp2(st - m_new)
            vt_aug = jnp.concatenate([vt_ref[kb, hcols[hh], :], ones_rows], axis=0)
            pv = _dot(vt_aug, p.astype(BF16))
            acc_sc[hh] = pv if first else jnp.exp2(m_prev - m_new) * acc_sc[hh] + pv
            m_sc[hh] = m_new

        visits = [qi] + list(range(qi))
        for hh in pair:
            scores(hh, visits[0], 0, True)
        for i, kb in enumerate(visits):
            for hh in pair:
                if i + 1 < len(visits):
                    scores(hh, visits[i + 1], (i + 1) % 2, False)
                accumulate(hh, kb, i % 2, i == 0)

        for hh in pair:
            acc = acc_sc[hh]
            y = (acc[:MLA_V] * (1.0 / acc[MLA_V:MLA_V + 1])).T
            o_ref[rows, hcols[hh]] = (y * g_ref[rows, hcols[hh]].astype(F32)).astype(BF16)


def _mla_flash(qn, qr, kn, krp, vt, g, batch, seq):
    tokens = batch * seq
    nblk = seq // FLASH_BLOCK
    pair_spec = pl.BlockSpec((None, seq, 2 * LANES), lambda b, p: (p, b, 0))
    return pl.pallas_call(
        functools.partial(_mla_flash_kernel, seq=seq),
        grid=(batch, MLA_HEADS // 2),
        in_specs=[pair_spec,
                  pl.BlockSpec((None, seq, LANES), lambda b, p: (p, b, 0)),
                  pair_spec,
                  pl.BlockSpec((seq, 2 * LANES), lambda b, p: (b, 0)),
                  pl.BlockSpec((nblk, 2 * LANES, FLASH_BLOCK), lambda b, p: (b, p, 0)),
                  pair_spec],
        out_specs=pair_spec,
        out_shape=jax.ShapeDtypeStruct((MLA_HEADS // 2, tokens, 2 * MLA_V), BF16),
        scratch_shapes=[pltpu.VMEM((2, seq, 2 * LANES), BF16),
                        pltpu.VMEM((2, 2, FLASH_BLOCK, FLASH_BLOCK), F32),
                        pltpu.VMEM((2, 1, FLASH_BLOCK), F32),
                        pltpu.VMEM((2, MLA_V + 16, FLASH_BLOCK), F32)],
        compiler_params=pltpu.CompilerParams(
            dimension_semantics=("arbitrary", "arbitrary"), vmem_limit_bytes=VMEM_LIMIT_BYTES),
        name="mla_flash",
    )(qn, qr, kn, krp, vt, g)


def _dup_heads(t):
    parts = []
    for hh in range(t.shape[1] // SWA_HEAD_DIM):
        piece = t[:, hh * SWA_HEAD_DIM:(hh + 1) * SWA_HEAD_DIM]
        parts += [piece, piece]
    return jnp.concatenate(parts, axis=1)


def _swa_in_kernel(x_ref, ln_ref, win_ref, c_ref, s1_ref, s2_ref,
                   q_ref, kd_ref, vt_ref, g_ref):
    scale = SWA_HEAD_DIM ** -0.5 * math.log2(math.e)
    o1 = SWA_WIDTH
    o2 = o1 + SWA_KV_WIDTH
    o3 = o2 + SWA_KV_WIDTH
    half = x_ref.shape[0] // 2
    halves = (slice(0, half), slice(half, 2 * half))

    def norm(rs):
        return _rmsnorm(x_ref[rs, :], ln_ref[...]).astype(BF16)

    def tables(rs):
        return c_ref[rs, :], s1_ref[rs, :], s2_ref[rs, :]

    def finish_q(rs, q):
        q_ref[rs, :] = (_rope_partial(q, *tables(rs)) * scale).astype(BF16)

    def finish_kv(rs, kv, part):
        k = _rope_partial(kv[:, :SWA_KV_WIDTH], *tables(rs))
        kd_ref[rs, :] = _dup_heads(k).astype(BF16)
        vt = kv[:, SWA_KV_WIDTH:].T.astype(BF16)
        per_half = vt_ref.shape[0] // 2
        for cc in range(per_half):
            vt_ref[part * per_half + cc] = vt[:, cc * BLOCK:(cc + 1) * BLOCK]

    def finish_gate(rs, gate):
        g_ref[rs, :] = (gate * jax.nn.sigmoid(gate)).astype(BF16)

    ha = norm(halves[0])
    qa = _dot(ha, win_ref[:, :o1])
    hb = norm(halves[1])
    kva = _dot(ha, win_ref[:, o1:o3])
    ga = _dot(ha, win_ref[:, o3:])
    finish_q(halves[0], qa)
    qb = _dot(hb, win_ref[:, :o1])
    finish_kv(halves[0], kva, 0)
    finish_gate(halves[0], ga)
    kvb = _dot(hb, win_ref[:, o1:o3])
    gb = _dot(hb, win_ref[:, o3:])
    finish_q(halves[1], qb)
    finish_kv(halves[1], kvb, 1)
    finish_gate(halves[1], gb)


def _swa_in(x, layer_norm, layer, w_in, j, c_t, s1_t, s2_t):
    tokens = x.shape[0]
    rows = SWA_PROJ_ROWS
    row_spec = lambda width: pl.BlockSpec((rows, width), lambda i: (i, 0))
    consts = [(layer_norm, layer), (w_in, j)]
    out_widths = [SWA_WIDTH, SWA_KV_HEADS * LANES, None, SWA_WIDTH]
    vt_spec = pl.BlockSpec((rows // BLOCK, SWA_KV_WIDTH, BLOCK), lambda i: (i, 0, 0))
    vt_shape = jax.ShapeDtypeStruct((tokens // BLOCK, SWA_KV_WIDTH, BLOCK), BF16)
    return pl.pallas_call(
        _swa_in_kernel,
        grid=(tokens // rows,),
        in_specs=[row_spec(D_MODEL)] + [_layer_spec(c, ll) for c, ll in consts]
                 + [row_spec(LANES)] * 3,
        out_specs=[vt_spec if wd is None else row_spec(wd) for wd in out_widths],
        out_shape=[vt_shape if wd is None else jax.ShapeDtypeStruct((tokens, wd), BF16)
                   for wd in out_widths],
        compiler_params=pltpu.CompilerParams(
            dimension_semantics=("arbitrary",), vmem_limit_bytes=VMEM_LIMIT_BYTES),
        name="swa_in",
    )(x, *[c for c, _ in consts], c_t, s1_t, s2_t)


def _swa_attn_kernel(sink_ref, q_ref, kc_ref, kp_ref, vtc_ref, vtp_ref, g_ref, o_ref):
    first_has_prev = pl.program_id(1) > 0
    pairs = SWA_GROUP // 2
    key_i = lax.broadcasted_iota(jnp.int32, (BLOCK, pairs * BLOCK), 0)
    qry_i = lax.broadcasted_iota(jnp.int32, (BLOCK, pairs * BLOCK), 1) % BLOCK
    in_cur = key_i <= qry_i
    cur_w = jnp.where(in_cur, 1.0, 0.0).astype(BF16)
    low = _lane_index((BLOCK, LANES)) < SWA_HEAD_DIM
    zero = jnp.zeros((BLOCK, LANES), BF16)

    def block_rows(qb):
        return slice(qb * BLOCK, (qb + 1) * BLOCK)

    def scores(qb, kvh, sub):
        cols = slice(kvh * LANES, (kvh + 1) * LANES)
        kc = kc_ref[block_rows(qb), cols]
        kp = kp_ref[:, cols] if qb == 0 else kc_ref[block_rows(qb - 1), cols]
        if sub == 0:
            k_half = jnp.concatenate([jnp.where(low, kp, zero), jnp.where(low, kc, zero)], axis=0)
        else:
            k_half = jnp.concatenate([jnp.where(low, zero, kp), jnp.where(low, zero, kc)], axis=0)
        head0 = kvh * SWA_GROUP
        q_rows = jnp.concatenate(
            [q_ref[block_rows(qb),
                   (head0 + 2 * pp) * SWA_HEAD_DIM:(head0 + 2 * pp + 2) * SWA_HEAD_DIM]
             for pp in range(pairs)], axis=0)
        return _dot_nt(k_half, q_rows)

    def attend(qb, kvh, sub, st):
        s_prev = jnp.where(first_has_prev, st[:BLOCK], NEG_INF) if qb == 0 else st[:BLOCK]
        s = jnp.where(in_cur, st[BLOCK:], s_prev)
        sink = sink_ref[sub * SWA_KV_HEADS + kvh:sub * SWA_KV_HEADS + kvh + 1, :]
        m = jnp.maximum(jnp.max(s, axis=0, keepdims=True), sink)
        e = jnp.exp2(s - m)
        denom = jnp.sum(e, axis=0, keepdims=True) + jnp.exp2(sink - m)
        p = e.astype(BF16)
        p_cur = p * cur_w
        pt = jnp.concatenate([p - p_cur, p_cur], axis=0)
        vrows = slice(kvh * SWA_HEAD_DIM, (kvh + 1) * SWA_HEAD_DIM)
        vtp = vtp_ref[vrows, :] if qb == 0 else vtc_ref[qb - 1, vrows, :]
        vt = jnp.concatenate([vtp, vtc_ref[qb, vrows, :]], axis=1)
        return _dot(vt, pt) * (1.0 / denom)

    chains = [(qb, kvh, sub) for qb in range(SWA_Q_BLOCKS) for kvh in range(SWA_KV_HEADS)
              for sub in range(2)]
    st_next = scores(*chains[0])
    outs = {}
    for ci, (qb, kvh, sub) in enumerate(chains):
        st = st_next
        if ci + 1 < len(chains):
            st_next = scores(*chains[ci + 1])
        outs[sub] = attend(qb, kvh, sub, st)
        if sub == 1:
            head0 = kvh * SWA_GROUP
            for pp in range(pairs):
                qc = slice(pp * BLOCK, (pp + 1) * BLOCK)
                y = jnp.concatenate([outs[0][:, qc], outs[1][:, qc]], axis=0).T
                pcols = slice((head0 + 2 * pp) * SWA_HEAD_DIM,
                              (head0 + 2 * pp + 2) * SWA_HEAD_DIM)
                o_ref[block_rows(qb), pcols] = (
                    y * g_ref[block_rows(qb), pcols].astype(F32)).astype(BF16)


def _swa_sink_rows(sinks):
    s2 = (sinks * math.log2(math.e)).reshape(SWA_KV_HEADS, SWA_GROUP // 2, 2)
    rows = jnp.transpose(s2, (2, 0, 1)).reshape(2 * SWA_KV_HEADS, SWA_GROUP // 2, 1)
    return jnp.broadcast_to(rows, (2 * SWA_KV_HEADS, SWA_GROUP // 2, BLOCK)).reshape(
        2 * SWA_KV_HEADS, -1)


def _swa_attn(sinks, q, kd, vt, g, batch, seq):
    tokens = batch * seq
    nb = seq // BLOCK
    steps = nb // SWA_Q_BLOCKS
    rows = SWA_Q_BLOCKS * BLOCK
    cur = lambda b, i: (b * steps + i, 0)
    prev = lambda b, i: (b * nb + jnp.maximum(i * SWA_Q_BLOCKS - 1, 0), 0)
    cur_t = lambda b, i: (b * steps + i, 0, 0)
    prev_t = lambda b, i: (b * nb + jnp.maximum(i * SWA_Q_BLOCKS - 1, 0), 0, 0)
    kv_width = SWA_KV_HEADS * LANES
    sink_rows = _swa_sink_rows(sinks)
    return pl.pallas_call(
        _swa_attn_kernel,
        grid=(batch, steps),
        in_specs=[pl.BlockSpec(sink_rows.shape, lambda b, i: (0, 0)),
                  pl.BlockSpec((rows, SWA_WIDTH), cur),
                  pl.BlockSpec((rows, kv_width), cur),
                  pl.BlockSpec((BLOCK, kv_width), prev),
                  pl.BlockSpec((SWA_Q_BLOCKS, SWA_KV_WIDTH, BLOCK), cur_t),
                  pl.BlockSpec((None, SWA_KV_WIDTH, BLOCK), prev_t),
                  pl.BlockSpec((rows, SWA_WIDTH), cur)],
        out_specs=pl.BlockSpec((rows, SWA_WIDTH), cur),
        out_shape=jax.ShapeDtypeStruct((tokens, SWA_WIDTH), BF16),
        compiler_params=pltpu.CompilerParams(
            dimension_semantics=("arbitrary", "arbitrary"), vmem_limit_bytes=VMEM_LIMIT_BYTES),
        name="swa_attn",
    )(sink_rows, q, kd, kd, vt, vt, g)


def _out_kernel(x_ref, y_ref, w_ref, fn_ref, o_ref, w_sc, *, final):
    @pl.when(pl.program_id(0) == 0)
    def _():
        w_sc[...] = w_ref[...].astype(BF16)

    y = _load_groups(y_ref) if len(y_ref.shape) == 3 else y_ref[...]
    x_new = x_ref[...] + _dot(y, w_sc[...])
    o_ref[...] = _rmsnorm(x_new, fn_ref[...]) if final else x_new


def _out_proj(x, y, w_out, j, final_norm, final):
    tokens = x.shape[0]
    rows = OUT_ROWS
    row_spec = lambda width: pl.BlockSpec((rows, width), lambda i: (i, 0))
    if y.ndim == 3:
        y_spec = pl.BlockSpec((y.shape[0], rows, y.shape[2]), lambda i: (0, i, 0))
    else:
        y_spec = row_spec(y.shape[1])
    return pl.pallas_call(
        functools.partial(_out_kernel, final=final),
        grid=(tokens // rows,),
        in_specs=[row_spec(D_MODEL), y_spec, _layer_spec(w_out, j),
                  _const_spec(final_norm.shape)],
        out_specs=row_spec(D_MODEL),
        out_shape=jax.ShapeDtypeStruct((tokens, D_MODEL), F32),
        scratch_shapes=[pltpu.VMEM(w_out.shape[1:], BF16)],
        compiler_params=pltpu.CompilerParams(
            dimension_semantics=("arbitrary",), vmem_limit_bytes=VMEM_LIMIT_BYTES),
        name="out_proj",
    )(x, y, w_out, final_norm)


def kernel(x, positions, layer_norm, mla_w_in, mla_q_norm, mla_w_uq, mla_kv_norm, mla_w_ukv,
           mla_w_out, swa_w_in, swa_sinks, swa_w_out, final_norm):
    batch, seq, d_model = x.shape
    tokens = batch * seq
    assert d_model == D_MODEL and seq % FLASH_BLOCK == 0 and tokens % OUT_ROWS == 0
    assert FLASH_BLOCK % MLA_PROJ_ROWS == 0 and SWA_PROJ_ROWS % BLOCK == 0
    assert seq % (SWA_Q_BLOCKS * BLOCK) == 0
    mcos, msin, sc, s1, s2 = _rope_tables(positions)
    xt = x.reshape(tokens, d_model)
    fn = final_norm.reshape(1, -1)
    ln_stack = layer_norm[:, None, :]
    mla_w = {"w_in": jnp.swapaxes(mla_w_in, 1, 2).astype(BF16), "w_uq": mla_w_uq.astype(BF16),
             "w_ukv": mla_w_ukv.astype(BF16), "q_norm": mla_q_norm[:, None, :],
             "kv_norm": mla_kv_norm[:, None, :]}
    swa_w = swa_w_in.astype(BF16)
    for i in range(DEPTH):
        j = i // N_MIXERS
        final = i == DEPTH - 1
        if i % N_MIXERS == 0:
            qn, qr, kn, vt, krp, g = _mla_in(xt, ln_stack, i, mla_w, j, mcos, msin)
            y = _mla_flash(qn, qr, kn, krp, vt, g, batch, seq)
            w_out = mla_w_out
        else:
            q, kd, vt, g = _swa_in(xt, ln_stack, i, swa_w, j, sc, s1, s2)
            y = _swa_attn(swa_sinks[j], q, kd, vt, g, batch, seq)
            w_out = swa_w_out
        xt = _out_proj(xt, y, w_out, j, fn, final)
    return xt.reshape(batch, seq, d_model)
```

```python
import functools
import math

import jax
import jax.numpy as jnp
from jax import lax
from jax.experimental import pallas as pl
from jax.experimental.pallas import tpu as pltpu

D_MODEL = 2048
DEPTH = 4
N_MIXERS = 2
ROPE_THETA = 500000.0
NORM_EPS = 1e-6
BLOCK = 128
NEG_INF = -1e30

MLA_HEADS = 16
MLA_Q_RANK = 512
MLA_KV_RANK = 512
MLA_NOPE = 128
MLA_ROPE = 64
MLA_V = 128
MLA_WIDTH = MLA_HEADS * MLA_V

SWA_Q_HEADS = 32
SWA_KV_HEADS = 4
SWA_GROUP = SWA_Q_HEADS // SWA_KV_HEADS
SWA_HEAD_DIM = 64
SWA_ROPE_DIM = SWA_HEAD_DIM // 4
SWA_WIDTH = SWA_Q_HEADS * SWA_HEAD_DIM
SWA_KV_WIDTH = SWA_KV_HEADS * SWA_HEAD_DIM

LANES = 128
VMEM_LIMIT_BYTES = 58 * 1024 * 1024

MLA_PROJ_ROWS = 512
SWA_PROJ_ROWS = 512
OUT_ROWS = 512
FLASH_BLOCK = 512
FLASH_PAIRS = 2
SWA_Q_BLOCKS = 4

BF16 = jnp.bfloat16
F32 = jnp.float32


def _dot(a, b):
    return jnp.dot(a, b, preferred_element_type=F32)


def _dot_nt(a, b):
    return lax.dot_general(a, b, (((1,), (1,)), ((), ())), preferred_element_type=F32)


def _rmsnorm(xf, g):
    y = xf * lax.rsqrt(jnp.mean(xf * xf, axis=-1, keepdims=True) + NORM_EPS)
    return y * g


def _tile_lanes(t, width):
    reps = width // t.shape[1]
    return t if reps == 1 else jnp.concatenate([t] * reps, axis=1)


def _lane_index(shape):
    return lax.broadcasted_iota(jnp.int32, shape, 1)


def _rope_half64(x, cos_t, sin_t):
    w = x.shape[1]
    fwd = pltpu.roll(x, w - MLA_ROPE // 2, 1)
    bwd = pltpu.roll(x, MLA_ROPE // 2, 1)
    first_half = (_lane_index(x.shape) % MLA_ROPE) < (MLA_ROPE // 2)
    swapped = jnp.where(first_half, fwd, bwd)
    return x * _tile_lanes(cos_t, w) + swapped * _tile_lanes(sin_t, w)


def _rope_partial(x, c_t, s1_t, s2_t):
    w = x.shape[1]
    half = SWA_ROPE_DIM // 2
    fwd = pltpu.roll(x, w - half, 1)
    bwd = pltpu.roll(x, half, 1)
    return (x * _tile_lanes(c_t, w) + fwd * _tile_lanes(s1_t, w)
            + bwd * _tile_lanes(s2_t, w))


def _rope_table_kernel(pos_ref, const_ref, mcos_ref, msin_ref, sc_ref, s1_ref, s2_ref):
    pos = pos_ref[...].astype(F32)
    ang_m = pos * const_ref[0:1, :]
    mcos_ref[...] = jnp.cos(ang_m)
    msin_ref[...] = jnp.sin(ang_m) * const_ref[1:2, :]
    ang_s = pos * const_ref[2:3, :]
    sin_s = jnp.sin(ang_s)
    sc_ref[...] = jnp.cos(ang_s)
    s1_ref[...] = sin_s * const_ref[3:4, :]
    s2_ref[...] = sin_s * const_ref[4:5, :]


def _rope_constants():
    lane = jnp.arange(LANES)
    f_mla = ROPE_THETA ** (-jnp.arange(0, MLA_ROPE, 2, dtype=F32) / MLA_ROPE)
    f_swa = ROPE_THETA ** (-jnp.arange(0, SWA_ROPE_DIM, 2, dtype=F32) / SWA_ROPE_DIM)
    half_m = MLA_ROPE // 2
    half_s = SWA_ROPE_DIM // 2
    d_m = lane % MLA_ROPE
    d_s = lane % SWA_HEAD_DIM
    rows = [
        f_mla[d_m % half_m],
        jnp.where(d_m < half_m, -1.0, 1.0),
        jnp.where(d_s < SWA_ROPE_DIM, f_swa[d_s % half_s], 0.0),
        jnp.where(d_s < half_s, -1.0, 0.0),
        jnp.where((d_s >= half_s) & (d_s < SWA_ROPE_DIM), 1.0, 0.0),
    ]
    rows += [jnp.zeros((LANES,), F32)] * 3
    return jnp.stack([r.astype(F32) for r in rows])


def _rope_tables(positions):
    tokens = positions.size
    rows = 1024
    tab = jax.ShapeDtypeStruct((tokens, LANES), F32)
    spec = pl.BlockSpec((rows, LANES), lambda i: (i, 0))
    return pl.pallas_call(
        _rope_table_kernel,
        grid=(tokens // rows,),
        in_specs=[pl.BlockSpec((rows, 1), lambda i: (i, 0)),
                  pl.BlockSpec((8, LANES), lambda i: (0, 0))],
        out_specs=[spec] * 5,
        out_shape=[tab] * 5,
        name="rope_tables",
    )(positions.reshape(tokens, 1), _rope_constants())


def _mla_in_kernel(x_ref, ln_ref, win_ref, qnorm_ref, kvnorm_ref, wuq_ref, wukv_ref,
                   cos_ref, sin_ref, qn_ref, qr_ref, kn_ref, vt_ref, krp_ref, g_ref):
    scale = (MLA_NOPE + MLA_ROPE) ** -0.5 * math.log2(math.e)
    h = _rmsnorm(x_ref[...], ln_ref[...]).astype(BF16)
    cos_t = cos_ref[...]
    sin_t = sin_ref[...]
    rank2 = MLA_Q_RANK + MLA_KV_RANK
    heads = range(MLA_HEADS)

    c = _dot_nt(h, win_ref[:rank2, :])
    cq = _rmsnorm(c[:, :MLA_Q_RANK], qnorm_ref[...]).astype(BF16)
    ckv = _rmsnorm(c[:, MLA_Q_RANK:], kvnorm_ref[...]).astype(BF16)

    tail = _dot_nt(h, win_ref[rank2:, :])
    kr = tail[:, :MLA_ROPE]
    kr = _rope_half64(jnp.concatenate([kr, kr], axis=1), cos_t, sin_t)
    low = _lane_index(kr.shape) < MLA_ROPE
    krp_ref[:, :LANES] = jnp.where(low, kr, 0.0).astype(BF16)
    krp_ref[:, LANES:] = jnp.where(low, 0.0, kr).astype(BF16)

    gate = tail[:, MLA_ROPE:]
    _store_groups(g_ref, (gate * jax.nn.sigmoid(gate)).astype(BF16))

    q = _dot(cq, wuq_ref[...])
    qd = MLA_NOPE + MLA_ROPE
    qn = jnp.concatenate([q[:, hh * qd:hh * qd + MLA_NOPE] for hh in heads], axis=1)
    qr = jnp.concatenate([q[:, hh * qd + MLA_NOPE:(hh + 1) * qd] for hh in heads], axis=1)
    _store_groups(qn_ref, (qn * scale).astype(BF16))
    _store_groups(qr_ref, (_rope_half64(qr, cos_t, sin_t) * scale).astype(BF16))

    kv = _dot(ckv, wukv_ref[...])
    kd = MLA_NOPE + MLA_V
    kn = jnp.concatenate([kv[:, hh * kd:hh * kd + MLA_NOPE] for hh in heads], axis=1)
    v = jnp.concatenate([kv[:, hh * kd + MLA_NOPE:(hh + 1) * kd] for hh in heads], axis=1)
    _store_groups(kn_ref, kn.astype(BF16))
    vt_ref[0] = v.T.astype(BF16)


def _store_groups(ref, val):
    groups, _, width = ref.shape
    for gg in range(groups):
        ref[gg] = val[:, gg * width:(gg + 1) * width]


def _load_groups(ref):
    return jnp.concatenate([ref[gg] for gg in range(ref.shape[0])], axis=1)


def _const_spec(shape):
    return pl.BlockSpec(shape, lambda i: (0,) * len(shape), pipeline_mode=pl.Buffered(1))


def _layer_spec(stacked, layer):
    tail = (0,) * (stacked.ndim - 1)
    return pl.BlockSpec((None,) + stacked.shape[1:], lambda i: (layer,) + tail,
                        pipeline_mode=pl.Buffered(1))


def _mla_in(x, layer_norm, layer, w, j, cos_t, sin_t):
    tokens = x.shape[0]
    rows = MLA_PROJ_ROWS
    row_spec = lambda width: pl.BlockSpec((rows, width), lambda i: (i, 0))
    consts = [(layer_norm, layer), (w["w_in"], j), (w["q_norm"], j), (w["kv_norm"], j),
              (w["w_uq"], j), (w["w_ukv"], j)]
    pairs = MLA_HEADS // 2
    per_blk = FLASH_BLOCK // rows

    def pair_major(width):
        return (pl.BlockSpec((pairs, rows, width), lambda i: (0, i, 0)),
                jax.ShapeDtypeStruct((pairs, tokens, width), BF16))

    outs = [pair_major(2 * MLA_NOPE),
            pair_major(2 * MLA_ROPE),
            pair_major(2 * MLA_NOPE),
            (pl.BlockSpec((1, MLA_WIDTH, rows), lambda i: (i // per_blk, 0, i % per_blk)),
             jax.ShapeDtypeStruct((tokens // FLASH_BLOCK, MLA_WIDTH, FLASH_BLOCK), BF16)),
            (row_spec(2 * LANES), jax.ShapeDtypeStruct((tokens, 2 * LANES), BF16)),
            pair_major(2 * MLA_V)]
    return pl.pallas_call(
        _mla_in_kernel,
        grid=(tokens // rows,),
        in_specs=[row_spec(D_MODEL)] + [_layer_spec(c, ll) for c, ll in consts]
                 + [row_spec(LANES), row_spec(LANES)],
        out_specs=[o[0] for o in outs],
        out_shape=[o[1] for o in outs],
        compiler_params=pltpu.CompilerParams(
            dimension_semantics=("arbitrary",), vmem_limit_bytes=VMEM_LIMIT_BYTES),
        name="mla_in",
    )(x, *[c for c, _ in consts], cos_t, sin_t)


def _mla_flash_kernel(qn_ref, qr_ref, kn_ref, kr_ref, vt_ref, g_ref, o_ref,
                      kcat_sc, s_sc, m_sc, acc_sc, *, seq):
    blk = FLASH_BLOCK
    pair = tuple(range(2 * FLASH_PAIRS))
    prs = [hh // 2 for hh in pair]
    hcols = [slice((hh % 2) * LANES, (hh % 2 + 1) * LANES) for hh in pair]
    vrows = [slice(hh * MLA_V, (hh + 1) * MLA_V) for hh in pair]
    for hh in pair:
        kcat_sc[hh, :, :LANES] = kn_ref[prs[hh], :, hcols[hh]]
        kcat_sc[hh, :, LANES:] = kr_ref[:, hcols[hh]]
    ones_rows = jnp.ones((16, blk), BF16)
    causal_t = (lax.broadcasted_iota(jnp.int32, (blk, blk), 0)
                <= lax.broadcasted_iota(jnp.int32, (blk, blk), 1))

    for qi in range(seq // blk):
        rows = slice(qi * blk, (qi + 1) * blk)
        qs = [jnp.concatenate([qn_ref[prs[hh], rows, hcols[hh]], qr_ref[prs[hh], rows, :]],
                              axis=1) for hh in pair]
        def scores(hh, kb, slot, masked, qs=qs):
            kstart = pl.multiple_of(kb * blk, blk)
            st = _dot_nt(kcat_sc[hh, pl.ds(kstart, blk), :], qs[hh])
            s_sc[hh, slot] = jnp.where(causal_t, st, NEG_INF) if masked else st

        def accumulate(hh, kb, slot, first):
            st = s_sc[hh, slot]
            m_prev = jnp.float32(NEG_INF) if first else m_sc[hh]
            m_new = jnp.maximum(m_prev, jnp.max(st, axis=0, keepdims=True))
            p = jnp.exp2(st - m_new)
            vt_aug = jnp.concatenate([vt_ref[kb, vrows[hh], :], ones_rows], axis=0)
            pv = _dot(vt_aug, p.astype(BF16))
            acc_sc[hh] = pv if first else jnp.exp2(m_prev - m_new) * acc_sc[hh] + pv
            m_sc[hh] = m_new

        visits = [qi] + list(range(qi))
        for hh in pair:
            scores(hh, visits[0], 0, True)
        for i, kb in enumerate(visits):
            for hh in pair:
                if i + 1 < len(visits):
                    scores(hh, visits[i + 1], (i + 1) % 2, False)
                accumulate(hh, kb, i % 2, i == 0)

        for hh in pair:
            acc = acc_sc[hh]
            y = (acc[:MLA_V] * (1.0 / acc[MLA_V:MLA_V + 1])).T
            o_ref[prs[hh], rows, hcols[hh]] = (
                y * g_ref[prs[hh], rows, hcols[hh]].astype(F32)).astype(BF16)


def _mla_flash(qn, qr, kn, krp, vt, g, batch, seq):
    tokens = batch * seq
    nblk = seq // FLASH_BLOCK
    heads = 2 * FLASH_PAIRS
    pair_spec = pl.BlockSpec((FLASH_PAIRS, seq, 2 * LANES), lambda b, p: (p, b, 0))
    return pl.pallas_call(
        functools.partial(_mla_flash_kernel, seq=seq),
        grid=(batch, MLA_HEADS // heads),
        in_specs=[pair_spec,
                  pl.BlockSpec((FLASH_PAIRS, seq, LANES), lambda b, p: (p, b, 0)),
                  pair_spec,
                  pl.BlockSpec((seq, 2 * LANES), lambda b, p: (b, 0)),
                  pl.BlockSpec((nblk, heads * MLA_V, FLASH_BLOCK), lambda b, p: (b, p, 0)),
                  pair_spec],
        out_specs=pair_spec,
        out_shape=jax.ShapeDtypeStruct((MLA_HEADS // 2, tokens, 2 * MLA_V), BF16),
        scratch_shapes=[pltpu.VMEM((heads, seq, 2 * LANES), BF16),
                        pltpu.VMEM((heads, 2, FLASH_BLOCK, FLASH_BLOCK), F32),
                        pltpu.VMEM((heads, 1, FLASH_BLOCK), F32),
                        pltpu.VMEM((heads, MLA_V + 16, FLASH_BLOCK), F32)],
        compiler_params=pltpu.CompilerParams(
            dimension_semantics=("arbitrary", "arbitrary"), vmem_limit_bytes=VMEM_LIMIT_BYTES),
        name="mla_flash",
    )(qn, qr, kn, krp, vt, g)


def _dup_heads(t):
    parts = []
    for hh in range(t.shape[1] // SWA_HEAD_DIM):
        piece = t[:, hh * SWA_HEAD_DIM:(hh + 1) * SWA_HEAD_DIM]
        parts += [piece, piece]
    return jnp.concatenate(parts, axis=1)


def _swa_in_kernel(x_ref, ln_ref, win_ref, c_ref, s1_ref, s2_ref,
                   q_ref, kd_ref, vt_ref, g_ref):
    scale = SWA_HEAD_DIM ** -0.5 * math.log2(math.e)
    o1 = SWA_WIDTH
    o2 = o1 + SWA_KV_WIDTH
    o3 = o2 + SWA_KV_WIDTH
    half = x_ref.shape[0] // 2
    halves = (slice(0, half), slice(half, 2 * half))

    def norm(rs):
        return _rmsnorm(x_ref[rs, :], ln_ref[...]).astype(BF16)

    def tables(rs):
        return c_ref[rs, :], s1_ref[rs, :], s2_ref[rs, :]

    def finish_q(rs, q):
        q_ref[rs, :] = (_rope_partial(q, *tables(rs)) * scale).astype(BF16)

    def finish_kv(rs, kv, part):
        k = _rope_partial(kv[:, :SWA_KV_WIDTH], *tables(rs))
        kd_ref[rs, :] = _dup_heads(k).astype(BF16)
        vt = kv[:, SWA_KV_WIDTH:].T.astype(BF16)
        per_half = vt_ref.shape[0] // 2
        for cc in range(per_half):
            vt_ref[part * per_half + cc] = vt[:, cc * BLOCK:(cc + 1) * BLOCK]

    def finish_gate(rs, gate):
        g_ref[rs, :] = (gate * jax.nn.sigmoid(gate)).astype(BF16)

    ha = norm(halves[0])
    qa = _dot(ha, win_ref[:, :o1])
    hb = norm(halves[1])
    kva = _dot(ha, win_ref[:, o1:o3])
    ga = _dot(ha, win_ref[:, o3:])
    finish_q(halves[0], qa)
    qb = _dot(hb, win_ref[:, :o1])
    finish_kv(halves[0], kva, 0)
    finish_gate(halves[0], ga)
    kvb = _dot(hb, win_ref[:, o1:o3])
    gb = _dot(hb, win_ref[:, o3:])
    finish_q(halves[1], qb)
    finish_kv(halves[1], kvb, 1)
    finish_gate(halves[1], gb)


def _swa_in(x, layer_norm, layer, w_in, j, c_t, s1_t, s2_t):
    tokens = x.shape[0]
    rows = SWA_PROJ_ROWS
    row_spec = lambda width: pl.BlockSpec((rows, width), lambda i: (i, 0))
    consts = [(layer_norm, layer), (w_in, j)]
    out_widths = [SWA_WIDTH, SWA_KV_HEADS * LANES, None, SWA_WIDTH]
    vt_spec = pl.BlockSpec((rows // BLOCK, SWA_KV_WIDTH, BLOCK), lambda i: (i, 0, 0))
    vt_shape = jax.ShapeDtypeStruct((tokens // BLOCK, SWA_KV_WIDTH, BLOCK), BF16)
    return pl.pallas_call(
        _swa_in_kernel,
        grid=(tokens // rows,),
        in_specs=[row_spec(D_MODEL)] + [_layer_spec(c, ll) for c, ll in consts]
                 + [row_spec(LANES)] * 3,
        out_specs=[vt_spec if wd is None else row_spec(wd) for wd in out_widths],
        out_shape=[vt_shape if wd is None else jax.ShapeDtypeStruct((tokens, wd), BF16)
                   for wd in out_widths],
        compiler_params=pltpu.CompilerParams(
            dimension_semantics=("arbitrary",), vmem_limit_bytes=VMEM_LIMIT_BYTES),
        name="swa_in",
    )(x, *[c for c, _ in consts], c_t, s1_t, s2_t)


def _swa_attn_kernel(sink_ref, q_ref, kc_ref, kp_ref, vtc_ref, vtp_ref, g_ref, o_ref):
    first_has_prev = pl.program_id(1) > 0
    pairs = SWA_GROUP // 2
    key_i = lax.broadcasted_iota(jnp.int32, (BLOCK, pairs * BLOCK), 0)
    qry_i = lax.broadcasted_iota(jnp.int32, (BLOCK, pairs * BLOCK), 1) % BLOCK
    in_cur = key_i <= qry_i
    cur_w = jnp.where(in_cur, 1.0, 0.0).astype(BF16)
    low = _lane_index((BLOCK, LANES)) < SWA_HEAD_DIM
    zero = jnp.zeros((BLOCK, LANES), BF16)

    def block_rows(qb):
        return slice(qb * BLOCK, (qb + 1) * BLOCK)

    def scores(qb, kvh, sub):
        cols = slice(kvh * LANES, (kvh + 1) * LANES)
        kc = kc_ref[block_rows(qb), cols]
        kp = kp_ref[:, cols] if qb == 0 else kc_ref[block_rows(qb - 1), cols]
        if sub == 0:
            k_half = jnp.concatenate([jnp.where(low, kp, zero), jnp.where(low, kc, zero)], axis=0)
        else:
            k_half = jnp.concatenate([jnp.where(low, zero, kp), jnp.where(low, zero, kc)], axis=0)
        head0 = kvh * SWA_GROUP
        q_rows = jnp.concatenate(
            [q_ref[block_rows(qb),
                   (head0 + 2 * pp) * SWA_HEAD_DIM:(head0 + 2 * pp + 2) * SWA_HEAD_DIM]
             for pp in range(pairs)], axis=0)
        return _dot_nt(k_half, q_rows)

    def attend(qb, kvh, sub, st):
        s_prev = jnp.where(first_has_prev, st[:BLOCK], NEG_INF) if qb == 0 else st[:BLOCK]
        s = jnp.where(in_cur, st[BLOCK:], s_prev)
        sink = sink_ref[sub * SWA_KV_HEADS + kvh:sub * SWA_KV_HEADS + kvh + 1, :]
        m = jnp.maximum(jnp.max(s, axis=0, keepdims=True), sink)
        e = jnp.exp2(s - m)
        denom = jnp.sum(e, axis=0, keepdims=True) + jnp.exp2(sink - m)
        p = e.astype(BF16)
        p_cur = p * cur_w
        pt = jnp.concatenate([p - p_cur, p_cur], axis=0)
        vrows = slice(kvh * SWA_HEAD_DIM, (kvh + 1) * SWA_HEAD_DIM)
        vtp = vtp_ref[vrows, :] if qb == 0 else vtc_ref[qb - 1, vrows, :]
        vt = jnp.concatenate([vtp, vtc_ref[qb, vrows, :]], axis=1)
        return _dot(vt, pt) * (1.0 / denom)

    chains = [(qb, kvh, sub) for qb in range(SWA_Q_BLOCKS) for kvh in range(SWA_KV_HEADS)
              for sub in range(2)]
    st_next = scores(*chains[0])
    outs = {}
    for ci, (qb, kvh, sub) in enumerate(chains):
        st = st_next
        if ci + 1 < len(chains):
            st_next = scores(*chains[ci + 1])
        outs[sub] = attend(qb, kvh, sub, st)
        if sub == 1:
            head0 = kvh * SWA_GROUP
            for pp in range(pairs):
                qc = slice(pp * BLOCK, (pp + 1) * BLOCK)
                y = jnp.concatenate([outs[0][:, qc], outs[1][:, qc]], axis=0).T
                pcols = slice((head0 + 2 * pp) * SWA_HEAD_DIM,
                              (head0 + 2 * pp + 2) * SWA_HEAD_DIM)
                o_ref[block_rows(qb), pcols] = (
                    y * g_ref[block_rows(qb), pcols].astype(F32)).astype(BF16)


def _swa_sink_rows(sinks):
    s2 = (sinks * math.log2(math.e)).reshape(SWA_KV_HEADS, SWA_GROUP // 2, 2)
    rows = jnp.transpose(s2, (2, 0, 1)).reshape(2 * SWA_KV_HEADS, SWA_GROUP // 2, 1)
    return jnp.broadcast_to(rows, (2 * SWA_KV_HEADS, SWA_GROUP // 2, BLOCK)).reshape(
        2 * SWA_KV_HEADS, -1)


def _swa_attn(sinks, q, kd, vt, g, batch, seq):
    tokens = batch * seq
    nb = seq // BLOCK
    steps = nb // SWA_Q_BLOCKS
    rows = SWA_Q_BLOCKS * BLOCK
    cur = lambda b, i: (b * steps + i, 0)
    prev = lambda b, i: (b * nb + jnp.maximum(i * SWA_Q_BLOCKS - 1, 0), 0)
    cur_t = lambda b, i: (b * steps + i, 0, 0)
    prev_t = lambda b, i: (b * nb + jnp.maximum(i * SWA_Q_BLOCKS - 1, 0), 0, 0)
    kv_width = SWA_KV_HEADS * LANES
    sink_rows = _swa_sink_rows(sinks)
    return pl.pallas_call(
        _swa_attn_kernel,
        grid=(batch, steps),
        in_specs=[pl.BlockSpec(sink_rows.shape, lambda b, i: (0, 0)),
                  pl.BlockSpec((rows, SWA_WIDTH), cur),
                  pl.BlockSpec((rows, kv_width), cur),
                  pl.BlockSpec((BLOCK, kv_width), prev),
                  pl.BlockSpec((SWA_Q_BLOCKS, SWA_KV_WIDTH, BLOCK), cur_t),
                  pl.BlockSpec((None, SWA_KV_WIDTH, BLOCK), prev_t),
                  pl.BlockSpec((rows, SWA_WIDTH), cur)],
        out_specs=pl.BlockSpec((rows, SWA_WIDTH), cur),
        out_shape=jax.ShapeDtypeStruct((tokens, SWA_WIDTH), BF16),
        compiler_params=pltpu.CompilerParams(
            dimension_semantics=("arbitrary", "arbitrary"), vmem_limit_bytes=VMEM_LIMIT_BYTES),
        name="swa_attn",
    )(sink_rows, q, kd, kd, vt, vt, g)


def _out_kernel(x_ref, y_ref, w_ref, fn_ref, o_ref, w_sc, *, final):
    @pl.when(pl.program_id(0) == 0)
    def _():
        w_sc[...] = w_ref[...].astype(BF16)

    y = _load_groups(y_ref) if len(y_ref.shape) == 3 else y_ref[...]
    x_new = x_ref[...] + _dot(y, w_sc[...])
    o_ref[...] = _rmsnorm(x_new, fn_ref[...]) if final else x_new


def _out_proj(x, y, w_out, j, final_norm, final):
    tokens = x.shape[0]
    rows = OUT_ROWS
    row_spec = lambda width: pl.BlockSpec((rows, width), lambda i: (i, 0))
    if y.ndim == 3:
        y_spec = pl.BlockSpec((y.shape[0], rows, y.shape[2]), lambda i: (0, i, 0))
    else:
        y_spec = row_spec(y.shape[1])
    return pl.pallas_call(
        functools.partial(_out_kernel, final=final),
        grid=(tokens // rows,),
        in_specs=[row_spec(D_MODEL), y_spec, _layer_spec(w_out, j),
                  _const_spec(final_norm.shape)],
        out_specs=row_spec(D_MODEL),
        out_shape=jax.ShapeDtypeStruct((tokens, D_MODEL), F32),
        scratch_shapes=[pltpu.VMEM(w_out.shape[1:], BF16)],
        compiler_params=pltpu.CompilerParams(
            dimension_semantics=("arbitrary",), vmem_limit_bytes=VMEM_LIMIT_BYTES),
        name="out_proj",
    )(x, y, w_out, final_norm)


def kernel(x, positions, layer_norm, mla_w_in, mla_q_norm, mla_w_uq, mla_kv_norm, mla_w_ukv,
           mla_w_out, swa_w_in, swa_sinks, swa_w_out, final_norm):
    batch, seq, d_model = x.shape
    tokens = batch * seq
    assert d_model == D_MODEL and seq % FLASH_BLOCK == 0 and tokens % OUT_ROWS == 0
    assert FLASH_BLOCK % MLA_PROJ_ROWS == 0 and SWA_PROJ_ROWS % BLOCK == 0
    assert seq % (SWA_Q_BLOCKS * BLOCK) == 0
    mcos, msin, sc, s1, s2 = _rope_tables(positions)
    xt = x.reshape(tokens, d_model)
    fn = final_norm.reshape(1, -1)
    ln_stack = layer_norm[:, None, :]
    mla_w = {"w_in": jnp.swapaxes(mla_w_in, 1, 2).astype(BF16), "w_uq": mla_w_uq.astype(BF16),
             "w_ukv": mla_w_ukv.astype(BF16), "q_norm": mla_q_norm[:, None, :],
             "kv_norm": mla_kv_norm[:, None, :]}
    swa_w = swa_w_in.astype(BF16)
    for i in range(DEPTH):
        j = i // N_MIXERS
        final = i == DEPTH - 1
        if i % N_MIXERS == 0:
            qn, qr, kn, vt, krp, g = _mla_in(xt, ln_stack, i, mla_w, j, mcos, msin)
            y = _mla_flash(qn, qr, kn, krp, vt, g, batch, seq)
            w_out = mla_w_out
        else:
            q, kd, vt, g = _swa_in(xt, ln_stack, i, swa_w, j, sc, s1, s2)
            y = _swa_attn(swa_sinks[j], q, kd, vt, g, batch, seq)
            w_out = swa_w_out
        xt = _out_proj(xt, y, w_out, j, fn, final)
    return xt.reshape(batch, seq, d_model)
```

```python
import functools
import math

import jax
import jax.numpy as jnp
from jax import lax
from jax.experimental import pallas as pl
from jax.experimental.pallas import tpu as pltpu

D_MODEL = 2048
DEPTH = 4
N_MIXERS = 2
ROPE_THETA = 500000.0
NORM_EPS = 1e-6
BLOCK = 128
NEG_INF = -1e30

MLA_HEADS = 16
MLA_Q_RANK = 512
MLA_KV_RANK = 512
MLA_NOPE = 128
MLA_ROPE = 64
MLA_V = 128
MLA_WIDTH = MLA_HEADS * MLA_V

SWA_Q_HEADS = 32
SWA_KV_HEADS = 4
SWA_GROUP = SWA_Q_HEADS // SWA_KV_HEADS
SWA_HEAD_DIM = 64
SWA_ROPE_DIM = SWA_HEAD_DIM // 4
SWA_WIDTH = SWA_Q_HEADS * SWA_HEAD_DIM
SWA_KV_WIDTH = SWA_KV_HEADS * SWA_HEAD_DIM

LANES = 128
VMEM_LIMIT_BYTES = 58 * 1024 * 1024

MLA_PROJ_ROWS = 512
SWA_PROJ_ROWS = 512
OUT_ROWS = 512
FLASH_BLOCK = 512
FLASH_PAIRS = 2
SWA_Q_BLOCKS = 4

BF16 = jnp.bfloat16
F32 = jnp.float32


def _dot(a, b):
    return jnp.dot(a, b, preferred_element_type=F32)


def _dot_nt(a, b):
    return lax.dot_general(a, b, (((1,), (1,)), ((), ())), preferred_element_type=F32)


def _rmsnorm(xf, g):
    y = xf * lax.rsqrt(jnp.mean(xf * xf, axis=-1, keepdims=True) + NORM_EPS)
    return y * g


def _tile_lanes(t, width):
    reps = width // t.shape[1]
    return t if reps == 1 else jnp.concatenate([t] * reps, axis=1)


def _lane_index(shape):
    return lax.broadcasted_iota(jnp.int32, shape, 1)


def _rope_half64(x, cos_t, sin_t):
    w = x.shape[1]
    fwd = pltpu.roll(x, w - MLA_ROPE // 2, 1)
    bwd = pltpu.roll(x, MLA_ROPE // 2, 1)
    first_half = (_lane_index(x.shape) % MLA_ROPE) < (MLA_ROPE // 2)
    swapped = jnp.where(first_half, fwd, bwd)
    return x * _tile_lanes(cos_t, w) + swapped * _tile_lanes(sin_t, w)


def _rope_partial(x, c_t, s1_t, s2_t):
    w = x.shape[1]
    half = SWA_ROPE_DIM // 2
    fwd = pltpu.roll(x, w - half, 1)
    bwd = pltpu.roll(x, half, 1)
    return (x * _tile_lanes(c_t, w) + fwd * _tile_lanes(s1_t, w)
            + bwd * _tile_lanes(s2_t, w))


def _rope_table_kernel(pos_ref, const_ref, mcos_ref, msin_ref, sc_ref, s1_ref, s2_ref):
    pos = pos_ref[...].astype(F32)
    ang_m = pos * const_ref[0:1, :]
    mcos_ref[...] = jnp.cos(ang_m)
    msin_ref[...] = jnp.sin(ang_m) * const_ref[1:2, :]
    ang_s = pos * const_ref[2:3, :]
    sin_s = jnp.sin(ang_s)
    sc_ref[...] = jnp.cos(ang_s)
    s1_ref[...] = sin_s * const_ref[3:4, :]
    s2_ref[...] = sin_s * const_ref[4:5, :]


def _rope_constants():
    lane = jnp.arange(LANES)
    f_mla = ROPE_THETA ** (-jnp.arange(0, MLA_ROPE, 2, dtype=F32) / MLA_ROPE)
    f_swa = ROPE_THETA ** (-jnp.arange(0, SWA_ROPE_DIM, 2, dtype=F32) / SWA_ROPE_DIM)
    half_m = MLA_ROPE // 2
    half_s = SWA_ROPE_DIM // 2
    d_m = lane % MLA_ROPE
    d_s = lane % SWA_HEAD_DIM
    rows = [
        f_mla[d_m % half_m],
        jnp.where(d_m < half_m, -1.0, 1.0),
        jnp.where(d_s < SWA_ROPE_DIM, f_swa[d_s % half_s], 0.0),
        jnp.where(d_s < half_s, -1.0, 0.0),
        jnp.where((d_s >= half_s) & (d_s < SWA_ROPE_DIM), 1.0, 0.0),
    ]
    rows += [jnp.zeros((LANES,), F32)] * 3
    return jnp.stack([r.astype(F32) for r in rows])


def _rope_tables(positions):
    tokens = positions.size
    rows = 1024
    tab = jax.ShapeDtypeStruct((tokens, LANES), F32)
    spec = pl.BlockSpec((rows, LANES), lambda i: (i, 0))
    return pl.pallas_call(
        _rope_table_kernel,
        grid=(tokens // rows,),
        in_specs=[pl.BlockSpec((rows, 1), lambda i: (i, 0)),
                  pl.BlockSpec((8, LANES), lambda i: (0, 0))],
        out_specs=[spec] * 5,
        out_shape=[tab] * 5,
        name="rope_tables",
    )(positions.reshape(tokens, 1), _rope_constants())


def _mla_in_kernel(x_ref, ln_ref, win_ref, qnorm_ref, kvnorm_ref, wuq_ref, wukv_ref,
                   cos_ref, sin_ref, qn_ref, qr_ref, kn_ref, vt_ref, krp_ref, g_ref):
    scale = (MLA_NOPE + MLA_ROPE) ** -0.5 * math.log2(math.e)
    h = _rmsnorm(x_ref[...], ln_ref[...]).astype(BF16)
    cos_t = cos_ref[...]
    sin_t = sin_ref[...]
    rank2 = MLA_Q_RANK + MLA_KV_RANK
    heads = range(MLA_HEADS)

    c = _dot_nt(h, win_ref[:rank2, :])
    cq = _rmsnorm(c[:, :MLA_Q_RANK], qnorm_ref[...]).astype(BF16)
    ckv = _rmsnorm(c[:, MLA_Q_RANK:], kvnorm_ref[...]).astype(BF16)

    tail = _dot_nt(h, win_ref[rank2:, :])
    kr = tail[:, :MLA_ROPE]
    kr = _rope_half64(jnp.concatenate([kr, kr], axis=1), cos_t, sin_t)
    low = _lane_index(kr.shape) < MLA_ROPE
    krp_ref[:, :LANES] = jnp.where(low, kr, 0.0).astype(BF16)
    krp_ref[:, LANES:] = jnp.where(low, 0.0, kr).astype(BF16)

    gate = tail[:, MLA_ROPE:]
    _store_groups(g_ref, (gate * jax.nn.sigmoid(gate)).astype(BF16))

    q = _dot(cq, wuq_ref[...])
    qd = MLA_NOPE + MLA_ROPE
    qn = jnp.concatenate([q[:, hh * qd:hh * qd + MLA_NOPE] for hh in heads], axis=1)
    qr = jnp.concatenate([q[:, hh * qd + MLA_NOPE:(hh + 1) * qd] for hh in heads], axis=1)
    _store_groups(qn_ref, (qn * scale).astype(BF16))
    _store_groups(qr_ref, (_rope_half64(qr, cos_t, sin_t) * scale).astype(BF16))

    kv = _dot(ckv, wukv_ref[...])
    kd = MLA_NOPE + MLA_V
    kn = jnp.concatenate([kv[:, hh * kd:hh * kd + MLA_NOPE] for hh in heads], axis=1)
    v = jnp.concatenate([kv[:, hh * kd + MLA_NOPE:(hh + 1) * kd] for hh in heads], axis=1)
    _store_groups(kn_ref, kn.astype(BF16))
    vt_ref[0] = v.T.astype(BF16)


def _store_groups(ref, val):
    groups, _, width = ref.shape
    for gg in range(groups):
        ref[gg] = val[:, gg * width:(gg + 1) * width]


def _load_groups(ref):
    return jnp.concatenate([ref[gg] for gg in range(ref.shape[0])], axis=1)


def _const_spec(shape):
    return pl.BlockSpec(shape, lambda i: (0,) * len(shape), pipeline_mode=pl.Buffered(1))


def _layer_spec(stacked, layer):
    tail = (0,) * (stacked.ndim - 1)
    return pl.BlockSpec((None,) + stacked.shape[1:], lambda i: (layer,) + tail,
                        pipeline_mode=pl.Buffered(1))


def _mla_in(x, layer_norm, layer, w, j, cos_t, sin_t):
    tokens = x.shape[0]
    rows = MLA_PROJ_ROWS
    row_spec = lambda width: pl.BlockSpec((rows, width), lambda i: (i, 0))
    consts = [(layer_norm, layer), (w["w_in"], j), (w["q_norm"], j), (w["kv_norm"], j),
              (w["w_uq"], j), (w["w_ukv"], j)]
    pairs = MLA_HEADS // 2
    per_blk = FLASH_BLOCK // rows

    def pair_major(width):
        return (pl.BlockSpec((pairs, rows, width), lambda i: (0, i, 0)),
                jax.ShapeDtypeStruct((pairs, tokens, width), BF16))

    outs = [pair_major(2 * MLA_NOPE),
            pair_major(2 * MLA_ROPE),
            pair_major(2 * MLA_NOPE),
            (pl.BlockSpec((1, MLA_WIDTH, rows), lambda i: (i // per_blk, 0, i % per_blk)),
             jax.ShapeDtypeStruct((tokens // FLASH_BLOCK, MLA_WIDTH, FLASH_BLOCK), BF16)),
            (row_spec(2 * LANES), jax.ShapeDtypeStruct((tokens, 2 * LANES), BF16)),
            pair_major(2 * MLA_V)]
    return pl.pallas_call(
        _mla_in_kernel,
        grid=(tokens // rows,),
        in_specs=[row_spec(D_MODEL)] + [_layer_spec(c, ll) for c, ll in consts]
                 + [row_spec(LANES), row_spec(LANES)],
        out_specs=[o[0] for o in outs],
        out_shape=[o[1] for o in outs],
        compiler_params=pltpu.CompilerParams(
            dimension_semantics=("arbitrary",), vmem_limit_bytes=VMEM_LIMIT_BYTES),
        name="mla_in",
    )(x, *[c for c, _ in consts], cos_t, sin_t)


def _mla_flash_kernel(qn_ref, qr_ref, kn_ref, kr_ref, vt_ref, g_ref, o_ref,
                      kcat_sc, s_sc, m_sc, acc_sc, *, seq):
    blk = FLASH_BLOCK
    pair = tuple(range(2 * FLASH_PAIRS))
    prs = [hh // 2 for hh in pair]
    hcols = [slice((hh % 2) * LANES, (hh % 2 + 1) * LANES) for hh in pair]
    vrows = [slice(hh * MLA_V, (hh + 1) * MLA_V) for hh in pair]
    for hh in pair:
        kcat_sc[hh, :, :LANES] = kn_ref[prs[hh], :, hcols[hh]]
        kcat_sc[hh, :, LANES:] = kr_ref[:, hcols[hh]]
    ones_rows = jnp.ones((16, blk), BF16)
    causal_t = (lax.broadcasted_iota(jnp.int32, (blk, blk), 0)
                <= lax.broadcasted_iota(jnp.int32, (blk, blk), 1))

    for qi in range(seq // blk):
        rows = slice(qi * blk, (qi + 1) * blk)
        qs = [jnp.concatenate([qn_ref[prs[hh], rows, hcols[hh]], qr_ref[prs[hh], rows, :]],
                              axis=1) for hh in pair]
        def scores(hh, kb, slot, masked, qs=qs):
            kstart = pl.multiple_of(kb * blk, blk)
            st = _dot_nt(kcat_sc[hh, pl.ds(kstart, blk), :], qs[hh])
            s_sc[hh, slot] = jnp.where(causal_t, st, NEG_INF) if masked else st

        def accumulate(hh, kb, slot, first):
            st = s_sc[hh, slot]
            m_prev = jnp.float32(NEG_INF) if first else m_sc[hh]
            m_new = jnp.maximum(m_prev, jnp.max(st, axis=0, keepdims=True))
            p = jnp.exp2(st - m_new)
            vt_aug = jnp.concatenate([vt_ref[kb, vrows[hh], :], ones_rows], axis=0)
            pv = _dot(vt_aug, p.astype(BF16))
            acc_sc[hh] = pv if first else jnp.exp2(m_prev - m_new) * acc_sc[hh] + pv
            m_sc[hh] = m_new

        visits = [qi] + list(range(qi))
        for hh in pair:
            scores(hh, visits[0], 0, True)
        for i, kb in enumerate(visits):
            for hh in pair:
                if i + 1 < len(visits):
                    scores(hh, visits[i + 1], (i + 1) % 2, False)
                accumulate(hh, kb, i % 2, i == 0)

        for hh in pair:
            acc = acc_sc[hh]
            y = (acc[:MLA_V] * (1.0 / acc[MLA_V:MLA_V + 1])).T
            o_ref[prs[hh], rows, hcols[hh]] = (
                y * g_ref[prs[hh], rows, hcols[hh]].astype(F32)).astype(BF16)


def _mla_flash(qn, qr, kn, krp, vt, g, batch, seq):
    tokens = batch * seq
    nblk = seq // FLASH_BLOCK
    heads = 2 * FLASH_PAIRS
    pair_spec = pl.BlockSpec((FLASH_PAIRS, seq, 2 * LANES), lambda b, p: (p, b, 0))
    return pl.pallas_call(
        functools.partial(_mla_flash_kernel, seq=seq),
        grid=(batch, MLA_HEADS // heads),
        in_specs=[pair_spec,
                  pl.BlockSpec((FLASH_PAIRS, seq, LANES), lambda b, p: (p, b, 0)),
                  pair_spec,
                  pl.BlockSpec((seq, 2 * LANES), lambda b, p: (b, 0)),
                  pl.BlockSpec((nblk, heads * MLA_V, FLASH_BLOCK), lambda b, p: (b, p, 0)),
                  pair_spec],
        out_specs=pair_spec,
        out_shape=jax.ShapeDtypeStruct((MLA_HEADS // 2, tokens, 2 * MLA_V), BF16),
        scratch_shapes=[pltpu.VMEM((heads, seq, 2 * LANES), BF16),
                        pltpu.VMEM((heads, 2, FLASH_BLOCK, FLASH_BLOCK), F32),
                        pltpu.VMEM((heads, 1, FLASH_BLOCK), F32),
                        pltpu.VMEM((heads, MLA_V + 16, FLASH_BLOCK), F32)],
        compiler_params=pltpu.CompilerParams(
            dimension_semantics=("arbitrary", "arbitrary"), vmem_limit_bytes=VMEM_LIMIT_BYTES),
        name="mla_flash",
    )(qn, qr, kn, krp, vt, g)


def _dup_heads(t):
    parts = []
    for hh in range(t.shape[1] // SWA_HEAD_DIM):
        piece = t[:, hh * SWA_HEAD_DIM:(hh + 1) * SWA_HEAD_DIM]
        parts += [piece, piece]
    return jnp.concatenate(parts, axis=1)


def _swa_in_kernel(x_ref, ln_ref, win_ref, c_ref, s1_ref, s2_ref,
                   q_ref, kd_ref, vt_ref, g_ref):
    scale = SWA_HEAD_DIM ** -0.5 * math.log2(math.e)
    o1 = SWA_WIDTH
    o2 = o1 + SWA_KV_WIDTH
    o3 = o2 + SWA_KV_WIDTH
    half = x_ref.shape[0] // 2
    halves = (slice(0, half), slice(half, 2 * half))

    def norm(rs):
        return _rmsnorm(x_ref[rs, :], ln_ref[...]).astype(BF16)

    def tables(rs):
        return c_ref[rs, :], s1_ref[rs, :], s2_ref[rs, :]

    def finish_q(rs, q):
        q_ref[rs, :] = (_rope_partial(q, *tables(rs)) * scale).astype(BF16)

    def finish_kv(rs, kv, part):
        k = _rope_partial(kv[:, :SWA_KV_WIDTH], *tables(rs))
        kd_ref[rs, :] = _dup_heads(k).astype(BF16)
        vt = kv[:, SWA_KV_WIDTH:].T.astype(BF16)
        per_half = vt_ref.shape[0] // 2
        for cc in range(per_half):
            vt_ref[part * per_half + cc] = vt[:, cc * BLOCK:(cc + 1) * BLOCK]

    def finish_gate(rs, gate):
        g_ref[rs, :] = (gate * jax.nn.sigmoid(gate)).astype(BF16)

    ha = norm(halves[0])
    qa = _dot(ha, win_ref[:, :o1])
    hb = norm(halves[1])
    kva = _dot(ha, win_ref[:, o1:o3])
    ga = _dot(ha, win_ref[:, o3:])
    finish_q(halves[0], qa)
    qb = _dot(hb, win_ref[:, :o1])
    finish_kv(halves[0], kva, 0)
    finish_gate(halves[0], ga)
    kvb = _dot(hb, win_ref[:, o1:o3])
    gb = _dot(hb, win_ref[:, o3:])
    finish_q(halves[1], qb)
    finish_kv(halves[1], kvb, 1)
    finish_gate(halves[1], gb)


def _swa_in(x, layer_norm, layer, w_in, j, c_t, s1_t, s2_t):
    tokens = x.shape[0]
    rows = SWA_PROJ_ROWS
    row_spec = lambda width: pl.BlockSpec((rows, width), lambda i: (i, 0))
    consts = [(layer_norm, layer), (w_in, j)]
    out_widths = [SWA_WIDTH, SWA_KV_HEADS * LANES, None, SWA_WIDTH]
    vt_spec = pl.BlockSpec((rows // BLOCK, SWA_KV_WIDTH, BLOCK), lambda i: (i, 0, 0))
    vt_shape = jax.ShapeDtypeStruct((tokens // BLOCK, SWA_KV_WIDTH, BLOCK), BF16)
    return pl.pallas_call(
        _swa_in_kernel,
        grid=(tokens // rows,),
        in_specs=[row_spec(D_MODEL)] + [_layer_spec(c, ll) for c, ll in consts]
                 + [row_spec(LANES)] * 3,
        out_specs=[vt_spec if wd is None else row_spec(wd) for wd in out_widths],
        out_shape=[vt_shape if wd is None else jax.ShapeDtypeStruct((tokens, wd), BF16)
                   for wd in out_widths],
        compiler_params=pltpu.CompilerParams(
            dimension_semantics=("arbitrary",), vmem_limit_bytes=VMEM_LIMIT_BYTES),
        name="swa_in",
    )(x, *[c for c, _ in consts], c_t, s1_t, s2_t)


def _swa_attn_kernel(sink_ref, q_ref, kc_ref, kp_ref, vtc_ref, vtp_ref, g_ref, o_ref):
    first_has_prev = pl.program_id(1) > 0
    pairs = SWA_GROUP // 2
    key_i = lax.broadcasted_iota(jnp.int32, (BLOCK, pairs * BLOCK), 0)
    qry_i = lax.broadcasted_iota(jnp.int32, (BLOCK, pairs * BLOCK), 1) % BLOCK
    in_cur = key_i <= qry_i
    cur_w = jnp.where(in_cur, 1.0, 0.0).astype(BF16)
    low = _lane_index((BLOCK, LANES)) < SWA_HEAD_DIM
    zero = jnp.zeros((BLOCK, LANES), BF16)

    def block_rows(qb):
        return slice(qb * BLOCK, (qb + 1) * BLOCK)

    def scores(qb, kvh, sub):
        cols = slice(kvh * LANES, (kvh + 1) * LANES)
        kc = kc_ref[block_rows(qb), cols]
        kp = kp_ref[:, cols] if qb == 0 else kc_ref[block_rows(qb - 1), cols]
        if sub == 0:
            k_half = jnp.concatenate([jnp.where(low, kp, zero), jnp.where(low, kc, zero)], axis=0)
        else:
            k_half = jnp.concatenate([jnp.where(low, zero, kp), jnp.where(low, zero, kc)], axis=0)
        head0 = kvh * SWA_GROUP
        q_rows = jnp.concatenate(
            [q_ref[block_rows(qb),
                   (head0 + 2 * pp) * SWA_HEAD_DIM:(head0 + 2 * pp + 2) * SWA_HEAD_DIM]
             for pp in range(pairs)], axis=0)
        return _dot_nt(k_half, q_rows)

    def attend(qb, kvh, sub, st):
        s_prev = jnp.where(first_has_prev, st[:BLOCK], NEG_INF) if qb == 0 else st[:BLOCK]
        s = jnp.where(in_cur, st[BLOCK:], s_prev)
        sink = sink_ref[sub * SWA_KV_HEADS + kvh:sub * SWA_KV_HEADS + kvh + 1, :]
        m = jnp.maximum(jnp.max(s, axis=0, keepdims=True), sink)
        e = jnp.exp2(s - m)
        denom = jnp.sum(e, axis=0, keepdims=True) + jnp.exp2(sink - m)
        p = e.astype(BF16)
        p_cur = p * cur_w
        pt = jnp.concatenate([p - p_cur, p_cur], axis=0)
        vrows = slice(kvh * SWA_HEAD_DIM, (kvh + 1) * SWA_HEAD_DIM)
        vtp = vtp_ref[vrows, :] if qb == 0 else vtc_ref[qb - 1, vrows, :]
        vt = jnp.concatenate([vtp, vtc_ref[qb, vrows, :]], axis=1)
        return _dot(vt, pt) * (1.0 / denom)

    chains = [(qb, kvh, sub) for qb in range(SWA_Q_BLOCKS) for kvh in range(SWA_KV_HEADS)
              for sub in range(2)]
    ahead = 2
    pending = [scores(*chains[ci]) for ci in range(ahead)]
    outs = {}
    for ci, (qb, kvh, sub) in enumerate(chains):
        st = pending.pop(0)
        if ci + ahead < len(chains):
            pending.append(scores(*chains[ci + ahead]))
        outs[sub] = attend(qb, kvh, sub, st)
        if sub == 1:
            head0 = kvh * SWA_GROUP
            for pp in range(pairs):
                qc = slice(pp * BLOCK, (pp + 1) * BLOCK)
                y = jnp.concatenate([outs[0][:, qc], outs[1][:, qc]], axis=0).T
                pcols = slice((head0 + 2 * pp) * SWA_HEAD_DIM,
                              (head0 + 2 * pp + 2) * SWA_HEAD_DIM)
                o_ref[block_rows(qb), pcols] = (
                    y * g_ref[block_rows(qb), pcols].astype(F32)).astype(BF16)


def _swa_sink_rows(sinks):
    s2 = (sinks * math.log2(math.e)).reshape(SWA_KV_HEADS, SWA_GROUP // 2, 2)
    rows = jnp.transpose(s2, (2, 0, 1)).reshape(2 * SWA_KV_HEADS, SWA_GROUP // 2, 1)
    return jnp.broadcast_to(rows, (2 * SWA_KV_HEADS, SWA_GROUP // 2, BLOCK)).reshape(
        2 * SWA_KV_HEADS, -1)


def _swa_attn(sinks, q, kd, vt, g, batch, seq):
    tokens = batch * seq
    nb = seq // BLOCK
    steps = nb // SWA_Q_BLOCKS
    rows = SWA_Q_BLOCKS * BLOCK
    cur = lambda b, i: (b * steps + i, 0)
    prev = lambda b, i: (b * nb + jnp.maximum(i * SWA_Q_BLOCKS - 1, 0), 0)
    cur_t = lambda b, i: (b * steps + i, 0, 0)
    prev_t = lambda b, i: (b * nb + jnp.maximum(i * SWA_Q_BLOCKS - 1, 0), 0, 0)
    kv_width = SWA_KV_HEADS * LANES
    sink_rows = _swa_sink_rows(sinks)
    return pl.pallas_call(
        _swa_attn_kernel,
        grid=(batch, steps),
        in_specs=[pl.BlockSpec(sink_rows.shape, lambda b, i: (0, 0)),
                  pl.BlockSpec((rows, SWA_WIDTH), cur),
                  pl.BlockSpec((rows, kv_width), cur),
                  pl.BlockSpec((BLOCK, kv_width), prev),
                  pl.BlockSpec((SWA_Q_BLOCKS, SWA_KV_WIDTH, BLOCK), cur_t),
                  pl.BlockSpec((None, SWA_KV_WIDTH, BLOCK), prev_t),
                  pl.BlockSpec((rows, SWA_WIDTH), cur)],
        out_specs=pl.BlockSpec((rows, SWA_WIDTH), cur),
        out_shape=jax.ShapeDtypeStruct((tokens, SWA_WIDTH), BF16),
        compiler_params=pltpu.CompilerParams(
            dimension_semantics=("arbitrary", "arbitrary"), vmem_limit_bytes=VMEM_LIMIT_BYTES),
        name="swa_attn",
    )(sink_rows, q, kd, kd, vt, vt, g)


def _out_kernel(x_ref, y_ref, w_ref, fn_ref, o_ref, w_sc, *, final):
    @pl.when(pl.program_id(0) == 0)
    def _():
        w_sc[...] = w_ref[...].astype(BF16)

    y = _load_groups(y_ref) if len(y_ref.shape) == 3 else y_ref[...]
    x_new = x_ref[...] + _dot(y, w_sc[...])
    o_ref[...] = _rmsnorm(x_new, fn_ref[...]) if final else x_new


def _out_proj(x, y, w_out, j, final_norm, final):
    tokens = x.shape[0]
    rows = OUT_ROWS
    row_spec = lambda width: pl.BlockSpec((rows, width), lambda i: (i, 0))
    if y.ndim == 3:
        y_spec = pl.BlockSpec((y.shape[0], rows, y.shape[2]), lambda i: (0, i, 0))
    else:
        y_spec = row_spec(y.shape[1])
    return pl.pallas_call(
        functools.partial(_out_kernel, final=final),
        grid=(tokens // rows,),
        in_specs=[row_spec(D_MODEL), y_spec, _layer_spec(w_out, j),
                  _const_spec(final_norm.shape)],
        out_specs=row_spec(D_MODEL),
        out_shape=jax.ShapeDtypeStruct((tokens, D_MODEL), F32),
        scratch_shapes=[pltpu.VMEM(w_out.shape[1:], BF16)],
        compiler_params=pltpu.CompilerParams(
            dimension_semantics=("arbitrary",), vmem_limit_bytes=VMEM_LIMIT_BYTES),
        name="out_proj",
    )(x, y, w_out, final_norm)


def kernel(x, positions, layer_norm, mla_w_in, mla_q_norm, mla_w_uq, mla_kv_norm, mla_w_ukv,
           mla_w_out, swa_w_in, swa_sinks, swa_w_out, final_norm):
    batch, seq, d_model = x.shape
    tokens = batch * seq
    assert d_model == D_MODEL and seq % FLASH_BLOCK == 0 and tokens % OUT_ROWS == 0
    assert FLASH_BLOCK % MLA_PROJ_ROWS == 0 and SWA_PROJ_ROWS % BLOCK == 0
    assert seq % (SWA_Q_BLOCKS * BLOCK) == 0
    mcos, msin, sc, s1, s2 = _rope_tables(positions)
    xt = x.reshape(tokens, d_model)
    fn = final_norm.reshape(1, -1)
    ln_stack = layer_norm[:, None, :]
    mla_w = {"w_in": jnp.swapaxes(mla_w_in, 1, 2).astype(BF16), "w_uq": mla_w_uq.astype(BF16),
             "w_ukv": mla_w_ukv.astype(BF16), "q_norm": mla_q_norm[:, None, :],
             "kv_norm": mla_kv_norm[:, None, :]}
    swa_w = swa_w_in.astype(BF16)
    for i in range(DEPTH):
        j = i // N_MIXERS
        final = i == DEPTH - 1
        if i % N_MIXERS == 0:
            qn, qr, kn, vt, krp, g = _mla_in(xt, ln_stack, i, mla_w, j, mcos, msin)
            y = _mla_flash(qn, qr, kn, krp, vt, g, batch, seq)
            w_out = mla_w_out
        else:
            q, kd, vt, g = _swa_in(xt, ln_stack, i, swa_w, j, sc, s1, s2)
            y = _swa_attn(swa_sinks[j], q, kd, vt, g, batch, seq)
            w_out = swa_w_out
        xt = _out_proj(xt, y, w_out, j, fn, final)
    return xt.reshape(batch, seq, d_model)
```
